```python
import math
import jax, jax.numpy as jnp
from jax import lax
import numpy as np

D_MODEL = 1024
BATCH = 8
SEQ = 4096
DEPTH = 4

GDN_HEADS = 4
GDN_DK = 128
GDN_DV = 128
GDN_CONV = 4
GDN_CHUNK = 64
SWA_GROUPS = ((128, 1), (512, 4), (2048, 16))
SWA_HEADS = 4
SWA_DH = 64
ROPE_THETA = 10000.0
GLA_HEADS = 4
GLA_DK = 64
GLA_DV = 128
GLA_RANK = 16
GLA_TAU = 16.0
GLA_CHUNK = 64
MEM_LEN = 256
XA_HEADS = 4
XA_DH = 128
N_EXPERTS = 32
TOP_K = 4
D_EXPERT = 1024
SWIGLU_ALPHA = 1.702
SWIGLU_LIMIT = 7.0
MOE_BLOCK = 128
RMS_EPS = 1e-6
N_BRANCH = 3

GDN_QK_W = GDN_HEADS * GDN_DK
GDN_V_W = GDN_HEADS * GDN_DV
SWA_W = len(SWA_GROUPS) * SWA_HEADS * SWA_DH
SWA_OUT_W = SWA_HEADS * SWA_DH
GLA_QK_W = GLA_HEADS * GLA_DK
GLA_V_W = GLA_HEADS * GLA_DV
IN_SIZES = (GDN_QK_W, GDN_QK_W, GDN_V_W, GDN_V_W, GDN_HEADS, GDN_HEADS,
            SWA_W, SWA_W, SWA_W,
            GLA_QK_W, GLA_QK_W, GLA_V_W, GLA_V_W, GLA_RANK,
            N_BRANCH * D_MODEL)
N_IN = sum(IN_SIZES)
IN_OFFSETS = tuple(sum(IN_SIZES[:i + 1]) for i in range(len(IN_SIZES) - 1))

kernel_name = "hybrid_gdn_dilswa_gla_moe_block"


def rms_norm(x, gain):
    xf = x.astype(jnp.float32)
    y = xf * lax.rsqrt(jnp.mean(xf * xf, axis=-1, keepdims=True) + RMS_EPS)
    return (y * gain.astype(jnp.float32)).astype(x.dtype)


def l2norm(x):
    return x * lax.rsqrt(jnp.sum(x * x, axis=-1, keepdims=True) + 1e-6)


def rope(x, cos, sin):
    xf = x.astype(jnp.float32)
    x1, x2 = jnp.split(xf, 2, axis=-1)
    return jnp.concatenate([x1 * cos - x2 * sin, x2 * cos + x1 * sin], axis=-1).astype(x.dtype)


def causal_depthwise_conv(x, w):
    K, C = w.shape
    return lax.conv_general_dilated(x, w[:, None, :].astype(x.dtype), window_strides=(1,),
                                    padding=((K - 1, 0),), dimension_numbers=("NWC", "WIO", "NWC"),
                                    feature_group_count=C)


def to_heads(t, n_heads):
    b, s = t.shape[:2]
    return t.reshape(b, s, n_heads, -1).transpose(0, 2, 1, 3)


def gated_delta_rule(q, k, v, g, beta):
    out_dtype = v.dtype
    f32 = jnp.float32
    q, k, v, g, beta = (t.astype(f32) for t in (q, k, v, g, beta))
    Bz, H, S, dk = q.shape
    dv = v.shape[-1]
    C = GDN_CHUNK
    N = S // C
    q = l2norm(q) * dk ** -0.5
    k = l2norm(k)
    chunk = lambda t: t.reshape(Bz, H, N, C, *t.shape[3:])
    q, k, v, g, beta = (chunk(t) for t in (q, k, v, g, beta))
    g = jnp.cumsum(g, axis=-1)
    incl = jnp.tril(jnp.ones((C, C), bool))
    strict = jnp.tril(jnp.ones((C, C), bool), -1)
    decay = jnp.exp(jnp.where(incl, g[..., :, None] - g[..., None, :], -jnp.inf))
    k_beta = k * beta[..., None]
    lower = jnp.where(strict, jnp.einsum("bhnid,bhnjd->bhnij", k_beta, k) * decay, 0.0)
    tmat = lower + jnp.eye(C, dtype=f32)
    solve = lambda a, b: lax.linalg.triangular_solve(a, b, left_side=True, lower=True, unit_diagonal=True)
    u = solve(tmat, v * beta[..., None])
    w = solve(tmat, k_beta * jnp.exp(g)[..., None])
    intra = jnp.where(incl, jnp.einsum("bhnid,bhnjd->bhnij", q, k) * decay, 0.0)
    q_dec = q * jnp.exp(g)[..., None]
    g_last = g[..., -1]
    k_dec = k * jnp.exp(g_last[..., None] - g)[..., None]

    def step(state, inp):
        q_c, w_c, u_c, a_c, k_c, gl = inp
        v_new = u_c - jnp.einsum("bhcd,bhde->bhce", w_c, state)
        o = jnp.einsum("bhcd,bhde->bhce", q_c, state) + jnp.einsum("bhij,bhje->bhie", a_c, v_new)
        state = state * jnp.exp(gl)[..., None, None] + jnp.einsum("bhcd,bhce->bhde", k_c, v_new)
        return state, o

    xs = tuple(jnp.moveaxis(t, 2, 0) for t in (q_dec, w, u, intra, k_dec, g_last))
    _, o = lax.scan(step, jnp.zeros((Bz, H, dk, dv), f32), xs)
    return jnp.moveaxis(o, 0, 2).reshape(Bz, H, S, dv).astype(out_dtype)


def gla_chunked(q, k, v, log_a):
    out_dtype = v.dtype
    f32 = jnp.float32
    q, k, v, log_a = (t.astype(f32) for t in (q, k, v, log_a))
    Bz, H, S, dk = q.shape
    dv = v.shape[-1]
    C = GLA_CHUNK
    N = S // C
    q = q * dk ** -0.5
    chunk = lambda t: jnp.moveaxis(t.reshape(Bz, H, N, C, t.shape[-1]), 2, 0)
    incl = jnp.tril(jnp.ones((C, C), bool))[:, :, None]

    def step(state, inp):
        q_c, k_c, v_c, la_c = inp
        b = jnp.cumsum(la_c, axis=-2)
        b_last = b[..., -1:, :]
        decay = jnp.exp(jnp.where(incl, b[..., :, None, :] - b[..., None, :, :], -jnp.inf))
        scores = jnp.einsum("bhid,bhjd,bhijd->bhij", q_c, k_c, decay)
        o = (jnp.einsum("bhid,bhde->bhie", q_c * jnp.exp(b), state)
             + jnp.einsum("bhij,bhje->bhie", scores, v_c))
        state = (state * jnp.exp(b_last)[..., 0, :, None]
                 + jnp.einsum("bhjd,bhje->bhde", k_c * jnp.exp(b_last - b), v_c))
        return state, o

    _, o = lax.scan(step, jnp.zeros((Bz, H, dk, dv), f32), (chunk(q), chunk(k), chunk(v), chunk(log_a)))
    return jnp.moveaxis(o, 0, 2).reshape(Bz, H, S, dv).astype(out_dtype)


def dilated_window_attention(q, k, v, dilation, n_back):
    Bz, S, H, dh = q.shape
    L = S // dilation
    c = n_back
    nb = -(-L // c)
    Lp = nb * c
    regroup = lambda t: t.reshape(Bz, L, dilation, H, dh).transpose(0, 2, 3, 1, 4).astype(jnp.float32)
    qs, ks, vs = regroup(q), regroup(k), regroup(v)
    qb = jnp.pad(qs, ((0, 0), (0, 0), (0, 0), (0, Lp - L), (0, 0))).reshape(Bz, dilation, H, nb, c, dh)

    def kv_blocks(t):
        tp = jnp.pad(t, ((0, 0), (0, 0), (0, 0), (c, Lp - L), (0, 0)))
        prev = tp[..., :Lp, :].reshape(Bz, dilation, H, nb, c, dh)
        cur = tp[..., c:, :].reshape(Bz, dilation, H, nb, c, dh)
        return jnp.concatenate([prev, cur], axis=-2)

    kb, vb = kv_blocks(ks), kv_blocks(vs)
    qi = jnp.arange(nb)[:, None, None] * c + jnp.arange(c)[None, :, None]
    ki = jnp.arange(nb)[:, None, None] * c - c + jnp.arange(2 * c)[None, None, :]
    dist = qi - ki
    valid = (dist >= 0) & (dist <= n_back) & (ki >= 0)
    s = jnp.einsum("brhnqd,brhnkd->brhnqk", qb, kb) * dh ** -0.5
    s = jnp.where(valid, s, -jnp.inf)
    m = jnp.max(s, axis=-1, keepdims=True)
    p = jnp.exp(s - m)
    l = jnp.sum(p, axis=-1, keepdims=True)
    o = jnp.einsum("brhnqk,brhnkd->brhnqd", p / l, vb)
    lse = (m + jnp.log(l))[..., 0]
    o = o.reshape(Bz, dilation, H, Lp, dh)[:, :, :, :L].transpose(0, 3, 1, 2, 4).reshape(Bz, S, H, dh)
    lse = lse.reshape(Bz, dilation, H, Lp)[:, :, :, :L].transpose(0, 3, 1, 2).reshape(Bz, S, H)
    return o, lse


def hybrid_mixer(h, cos, sin, w_in, gate_bias, gdn_conv, gdn_a_log, gdn_dt_bias, gdn_norm,
                 swa_q_norm, swa_k_norm, gla_gate_up, gla_gate_bias, gla_norm,
                 w_branch_a, w_branch_b, w_branch_c, w_mix_out):
    Bz, S, _ = h.shape
    proj = h @ w_in
    (a_q, a_k, a_v, a_z, a_alpha, a_beta, b_q, b_k, b_v,
     c_q, c_k, c_v, c_r, c_low, gates) = jnp.split(proj, IN_OFFSETS, axis=-1)

    qkv = jax.nn.silu(causal_depthwise_conv(jnp.concatenate([a_q, a_k, a_v], axis=-1), gdn_conv))
    a_q, a_k, a_v = jnp.split(qkv, (GDN_QK_W, 2 * GDN_QK_W), axis=-1)
    beta = jax.nn.sigmoid(a_beta.astype(jnp.float32))
    g = -jnp.exp(gdn_a_log) * jax.nn.softplus(a_alpha.astype(jnp.float32) + gdn_dt_bias)
    o_a = gated_delta_rule(to_heads(a_q, GDN_HEADS), to_heads(a_k, GDN_HEADS), to_heads(a_v, GDN_HEADS),
                           g.transpose(0, 2, 1), beta.transpose(0, 2, 1)).transpose(0, 2, 1, 3)
    z = a_z.reshape(Bz, S, GDN_HEADS, GDN_DV)
    y_a = (rms_norm(o_a, gdn_norm) * jax.nn.silu(z)).reshape(Bz, S, GDN_V_W) @ w_branch_a

    shp = (Bz, S, len(SWA_GROUPS), SWA_HEADS, SWA_DH)
    qb = rope(rms_norm(b_q.reshape(shp), swa_q_norm), cos, sin)
    kb = rope(rms_norm(b_k.reshape(shp), swa_k_norm), cos, sin)
    vb = b_v.reshape(shp)
    outs, lses = [], []
    for gi, (window, dil) in enumerate(SWA_GROUPS):
        o_g, lse_g = dilated_window_attention(qb[:, :, gi], kb[:, :, gi], vb[:, :, gi], dil, window // dil)
        outs.append(o_g)
        lses.append(lse_g)
    wts = jax.nn.softmax(jnp.stack(lses, axis=0), axis=0)
    o_b = jnp.einsum("gbsh,gbshd->bshd", wts, jnp.stack(outs, axis=0)).astype(h.dtype)
    y_b = o_b.reshape(Bz, S, SWA_OUT_W) @ w_branch_b

    log_a = jax.nn.log_sigmoid((c_low @ gla_gate_up + gla_gate_bias).astype(jnp.float32)) / GLA_TAU
    o_c = gla_chunked(to_heads(c_q, GLA_HEADS), to_heads(c_k, GLA_HEADS), to_heads(c_v, GLA_HEADS),
                      to_heads(log_a, GLA_HEADS)).transpose(0, 2, 1, 3)
    r = c_r.reshape(Bz, S, GLA_HEADS, GLA_DV)
    y_c = (rms_norm(o_c, gla_norm) * jax.nn.silu(r)).reshape(Bz, S, GLA_V_W) @ w_branch_c

    gt = jax.nn.sigmoid(gates.reshape(Bz, S, N_BRANCH, D_MODEL) + gate_bias)
    y = gt[:, :, 0] * y_a + gt[:, :, 1] * y_b + gt[:, :, 2] * y_c
    return y @ w_mix_out


def memory_cross_attention(h, m, wq, wkv, q_gain, k_gain, wo):
    Bz, S, _ = h.shape
    M = m.shape[1]
    q = rms_norm((h @ wq).reshape(Bz, S, XA_HEADS, XA_DH), q_gain)
    kv = (m @ wkv).reshape(Bz, M, 2, XA_HEADS, XA_DH)
    k = rms_norm(kv[:, :, 0], k_gain)
    v = kv[:, :, 1]
    s = jnp.einsum("bshd,bmhd->bhsm", q.astype(jnp.float32), k.astype(jnp.float32)) * XA_DH ** -0.5
    p = jax.nn.softmax(s, axis=-1)
    o = jnp.einsum("bhsm,bmhd->bshd", p, v.astype(jnp.float32)).astype(h.dtype)
    return o.reshape(Bz, S, XA_HEADS * XA_DH) @ wo


def moe_ffn(h, router_w, router_b, w_in, b_in, w_out, b_out):
    Bz, S, D = h.shape
    T = Bz * S
    A = T * TOP_K
    xf = h.reshape(T, D)
    logits = (xf @ router_w + router_b).astype(jnp.float32)
    top_val, top_idx = lax.top_k(logits, TOP_K)
    gates = jax.nn.softmax(top_val, axis=-1).astype(h.dtype)
    e_flat = top_idx.reshape(A)
    order = jnp.argsort(e_flat)
    e_sorted = e_flat[order]
    tok_sorted = (order // TOP_K).astype(jnp.int32)
    gate_sorted = gates.reshape(A)[order]
    counts = jnp.bincount(e_flat, length=N_EXPERTS)
    padded = (counts + MOE_BLOCK - 1) // MOE_BLOCK * MOE_BLOCK
    starts = jnp.cumsum(counts) - counts
    pad_ends = jnp.cumsum(padded)
    pad_starts = pad_ends - padded
    dest = pad_starts[e_sorted] + jnp.arange(A) - starts[e_sorted]
    P = A + N_EXPERTS * MOE_BLOCK
    NB = P // MOE_BLOCK
    tok_buf = jnp.full((P,), T, jnp.int32).at[dest].set(tok_sorted)
    gate_buf = jnp.zeros((P,), h.dtype).at[dest].set(gate_sorted)
    block_expert = jnp.minimum(jnp.searchsorted(pad_ends, jnp.arange(NB) * MOE_BLOCK, side="right"),
                               N_EXPERTS - 1)
    xb = jnp.concatenate([xf, jnp.zeros((1, D), xf.dtype)], axis=0)[tok_buf].reshape(NB, MOE_BLOCK, D)

    def expert_block(args):
        xblk, e = args
        hh = xblk @ w_in[e] + b_in[e]
        glu = jnp.minimum(hh[:, :D_EXPERT], SWIGLU_LIMIT)
        lin = jnp.clip(hh[:, D_EXPERT:], -SWIGLU_LIMIT, SWIGLU_LIMIT)
        act = glu * jax.nn.sigmoid(SWIGLU_ALPHA * glu) * (lin + 1.0)
        return act @ w_out[e] + b_out[e]

    yb = lax.map(expert_block, (xb, block_expert)).reshape(P, D)
    y = jnp.zeros((T + 1, D), h.dtype).at[tok_buf].add(yb * gate_buf[:, None])
    return y[:T].reshape(Bz, S, D)


def setup_inputs(seed: int = 0) -> dict:
    key = jax.random.key(seed)
    ks = jax.random.split(key, 36)
    L, D = DEPTH, D_MODEL
    res = (3.0 * DEPTH) ** -0.5

    def nrm(k, shape, scale):
        return jax.random.normal(k, shape, jnp.float32) * scale

    def gain(k, shape):
        return 1.0 + nrm(k, shape, 0.05)

    dt = jnp.exp(jax.random.uniform(ks[6], (L, GDN_HEADS), jnp.float32, math.log(1e-3), math.log(1e-1)))
    start = jax.random.randint(ks[2], (BATCH, 1), 0, 4096, jnp.int32)
    return {
        "x": nrm(ks[0], (BATCH, SEQ, D), 1.0),
        "mem": nrm(ks[1], (BATCH, MEM_LEN, D), 1.0),
        "positions": start + jnp.arange(SEQ, dtype=jnp.int32)[None, :],
        "norm_mix": gain(ks[3], (L, D)),
        "w_in": nrm(ks[4], (L, D, N_IN), D ** -0.5),
        "gate_bias": nrm(ks[5], (L, N_BRANCH, D), 0.1),
        "gdn_conv": nrm(ks[7], (L, GDN_CONV, 2 * GDN_QK_W + GDN_V_W), GDN_CONV ** -0.5),
        "gdn_a_log": jnp.log(jax.random.uniform(ks[8], (L, GDN_HEADS), jnp.float32, 1.0, 16.0)),
        "gdn_dt_bias": dt + jnp.log(-jnp.expm1(-dt)),
        "gdn_norm": gain(ks[9], (L, GDN_DV)),
        "swa_q_norm": gain(ks[10], (L, SWA_DH)),
        "swa_k_norm": gain(ks[11], (L, SWA_DH)),
        "gla_gate_up": nrm(ks[12], (L, GLA_RANK, GLA_QK_W), GLA_RANK ** -0.5),
        "gla_gate_bias": nrm(ks[13], (L, GLA_QK_W), 0.5),
        "gla_norm": gain(ks[14], (L, GLA_DV)),
        "w_branch_a": nrm(ks[15], (L, GDN_V_W, D), GDN_V_W ** -0.5),
        "w_branch_b": nrm(ks[16], (L, SWA_OUT_W, D), SWA_OUT_W ** -0.5),
        "w_branch_c": nrm(ks[17], (L, GLA_V_W, D), GLA_V_W ** -0.5),
        "w_mix_out": nrm(ks[18], (L, D, D), D ** -0.5 * res),
        "norm_cross": gain(ks[19], (L, D)),
        "norm_mem": gain(ks[20], (L, D)),
        "xa_wq": nrm(ks[21], (L, D, XA_HEADS * XA_DH), D ** -0.5),
        "xa_wkv": nrm(ks[22], (L, D, 2 * XA_HEADS * XA_DH), D ** -0.5),
        "xa_q_norm": gain(ks[23], (L, XA_DH)),
        "xa_k_norm": gain(ks[24], (L, XA_DH)),
        "xa_wo": nrm(ks[25], (L, XA_HEADS * XA_DH, D), (XA_HEADS * XA_DH) ** -0.5 * res),
        "norm_ffn": gain(ks[26], (L, D)),
        "router_w": nrm(ks[27], (L, D, N_EXPERTS), D ** -0.5),
        "router_b": nrm(ks[28], (L, N_EXPERTS), 0.01),
        "moe_w_in": nrm(ks[29], (L, N_EXPERTS, D, 2 * D_EXPERT), D ** -0.5),
        "moe_b_in": nrm(ks[30], (L, N_EXPERTS, 2 * D_EXPERT), 0.02),
        "moe_w_out": nrm(ks[31], (L, N_EXPERTS, D_EXPERT, D), D_EXPERT ** -0.5 * res),
        "moe_b_out": nrm(ks[32], (L, N_EXPERTS, D), 0.02),
    }


def reference(x, mem, positions, norm_mix, w_in, gate_bias, gdn_conv, gdn_a_log, gdn_dt_bias, gdn_norm,
              swa_q_norm, swa_k_norm, gla_gate_up, gla_gate_bias, gla_norm, w_branch_a, w_branch_b,
              w_branch_c, w_mix_out, norm_cross, norm_mem, xa_wq, xa_wkv, xa_q_norm, xa_k_norm, xa_wo,
              norm_ffn, router_w, router_b, moe_w_in, moe_b_in, moe_w_out, moe_b_out):
    inv_freq = ROPE_THETA ** (-jnp.arange(0, SWA_DH, 2, dtype=jnp.float32) / SWA_DH)
    ang = positions.astype(jnp.float32)[..., None] * inv_freq
    cos = jnp.cos(ang)[:, :, None, None, :]
    sin = jnp.sin(ang)[:, :, None, None, :]
    for l in range(DEPTH):
        h = rms_norm(x, norm_mix[l])
        x = x + hybrid_mixer(h, cos, sin, w_in[l], gate_bias[l], gdn_conv[l], gdn_a_log[l], gdn_dt_bias[l],
                             gdn_norm[l], swa_q_norm[l], swa_k_norm[l], gla_gate_up[l], gla_gate_bias[l],
                             gla_norm[l], w_branch_a[l], w_branch_b[l], w_branch_c[l], w_mix_out[l])
        h = rms_norm(x, norm_cross[l])
        m = rms_norm(mem, norm_mem[l])
        x = x + memory_cross_attention(h, m, xa_wq[l], xa_wkv[l], xa_q_norm[l], xa_k_norm[l], xa_wo[l])
        h = rms_norm(x, norm_ffn[l])
        x = x + moe_ffn(h, router_w[l], router_b[l], moe_w_in[l], moe_b_in[l], moe_w_out[l], moe_b_out[l])
    return x
```

```python
import functools

import jax
import jax.numpy as jnp
from jax import lax
from jax.experimental import pallas as pl
from jax.experimental.pallas import tpu as pltpu

F32 = jnp.float32
BF16 = jnp.bfloat16
HIGHEST = lax.Precision.HIGHEST

RMS_EPS = 1e-6
L2_EPS = 1e-6
LANES = 128
VMEM_LIMIT = 56 * 1024 * 1024

D_MODEL = 1024
GDN_HEADS, GDN_D, GDN_CONV, CHUNK = 4, 128, 4, 64
SWA_GROUPS = ((128, 1), (512, 4), (2048, 16))
SWA_HEADS, SWA_DH, SWA_BACK = 4, 64, 128
ROPE_THETA = 10000.0
GLA_HEADS, GLA_DK, GLA_DV, GLA_RANK, GLA_TAU = 4, 64, 128, 16, 16.0
GLA_SUB = 8
XA_HEADS, XA_DH = 4, 128
N_EXPERTS, TOP_K, D_EXPERT = 32, 4, 1024
SWIGLU_ALPHA, SWIGLU_LIMIT = 1.702, 7.0
MOE_BM = 256

COL_GATES = 0
COL_AQ, COL_AK, COL_AV, COL_AZ = 3072, 3584, 4096, 4608
COL_BQ, COL_BK, COL_BV = 5120, 5888, 6656
COL_SMALL = 7424
COL_CV, COL_CR, COL_CQ, COL_CK = 7680, 8192, 8704, 8960
N_PACKED = 9216


def _cparams(*sem):
    return pltpu.CompilerParams(dimension_semantics=sem, vmem_limit_bytes=VMEM_LIMIT)


def _sigmoid(x):
    return 1.0 / (1.0 + jnp.exp(-x))


def _silu(x):
    return x * _sigmoid(x)


def _softplus(x):
    return jnp.maximum(x, 0.0) + jnp.log(1.0 + jnp.exp(-jnp.abs(x)))


def _dot(a, b, precision=None):
    return jnp.dot(a, b, preferred_element_type=F32, precision=precision)


def _dot_nt(a, b, precision=None):
    return lax.dot_general(a, b, (((1,), (1,)), ((), ())), preferred_element_type=F32, precision=precision)


def _dot_tn(a, b, precision=None):
    return lax.dot_general(a, b, (((0,), (0,)), ((), ())), preferred_element_type=F32, precision=precision)


def _iota(shape, axis):
    return lax.broadcasted_iota(jnp.int32, shape, axis)


def _norm_matmul_body(x_ref, g_ref, w_ref, o_ref, h_ref):
    @pl.when(pl.program_id(1) == 0)
    def _():
        x = x_ref[...]
        ms = jnp.mean(x * x, axis=-1, keepdims=True)
        h_ref[...] = (x * lax.rsqrt(ms + RMS_EPS) * g_ref[...]).astype(h_ref.dtype)

    o_ref[...] = _dot(h_ref[...], w_ref[...]).astype(o_ref.dtype)


def norm_matmul(x, gain, w, *, tm, tn, name):
    T, D = x.shape
    N = w.shape[1]
    return pl.pallas_call(
        _norm_matmul_body,
        grid=(T // tm, N // tn),
        in_specs=[pl.BlockSpec((tm, D), lambda i, j: (i, 0)),
                  pl.BlockSpec((1, D), lambda i, j: (0, 0)),
                  pl.BlockSpec((D, tn), lambda i, j: (0, j))],
        out_specs=pl.BlockSpec((tm, tn), lambda i, j: (i, j)),
        out_shape=jax.ShapeDtypeStruct((T, N), F32),
        scratch_shapes=[pltpu.VMEM((tm, D), BF16)],
        compiler_params=_cparams("parallel", "arbitrary"),
        name=name,
    )(x, gain.reshape(1, D), w)


def _gdn_body(q_ref, k_ref, v_ref, z_ref, sm_ref, cq_ref, ck_ref, cv_ref, alog_ref, dtb_ref, gn_ref,
              o_ref, state_ref, tail_ref, qs_ref, ks_ref, vs_ref, gs_ref, bs_ref, os_ref, *, rows):
    h = pl.program_id(1)
    blk = pl.program_id(2)
    nchunk = rows // CHUNK

    @pl.when(blk == 0)
    def _():
        state_ref[...] = jnp.zeros_like(state_ref)
        tail_ref[...] = jnp.zeros_like(tail_ref)

    def conv_silu(x_ref, w_ref, slot):
        x = x_ref[...]
        xp = jnp.concatenate([tail_ref[slot], x], axis=0)
        w = w_ref[...]
        acc = x * w[GDN_CONV - 1:GDN_CONV, :]
        for s in range(1, GDN_CONV):
            acc = acc + pltpu.roll(xp, s, 0)[8:] * w[GDN_CONV - 1 - s:GDN_CONV - s, :]
        tail_ref[slot] = x[rows - 8:rows]
        return _silu(acc)

    q = conv_silu(q_ref, cq_ref, 0)
    k = conv_silu(k_ref, ck_ref, 1)
    v = conv_silu(v_ref, cv_ref, 2)
    q = q * lax.rsqrt(jnp.sum(q * q, axis=-1, keepdims=True) + L2_EPS) * (GDN_D ** -0.5)
    k = k * lax.rsqrt(jnp.sum(k * k, axis=-1, keepdims=True) + L2_EPS)

    lane = _iota((1, LANES), 1)
    sm = sm_ref[...]
    alpha = jnp.sum(jnp.where(lane == h, sm, 0.0), axis=-1, keepdims=True)
    beta_raw = jnp.sum(jnp.where(lane == h + GDN_HEADS, sm, 0.0), axis=-1, keepdims=True)
    a_log = jnp.sum(jnp.where(lane == h, alog_ref[...], 0.0), axis=-1, keepdims=True)
    dt_b = jnp.sum(jnp.where(lane == h, dtb_ref[...], 0.0), axis=-1, keepdims=True)
    g = -jnp.exp(a_log) * _softplus(alpha + dt_b)
    beta = _sigmoid(beta_raw)

    qs_ref[...] = q
    ks_ref[...] = k
    vs_ref[...] = v
    gs_ref[...] = jnp.broadcast_to(g, (rows, LANES))
    bs_ref[...] = jnp.broadcast_to(beta, (rows, LANES))

    ri = _iota((CHUNK, CHUNK), 0)
    ci = _iota((CHUNK, CHUNK), 1)
    incl = ri >= ci
    strict = ri > ci
    tril = incl.astype(F32)
    eye = (ri == ci).astype(F32)
    ones_cl = jnp.ones((CHUNK, LANES), F32)

    def chunk_step(c, carry):
        r0 = pl.multiple_of(c * CHUNK, CHUNK)
        qc = qs_ref[pl.ds(r0, CHUNK), :]
        kc = ks_ref[pl.ds(r0, CHUNK), :]
        vc = vs_ref[pl.ds(r0, CHUNK), :]
        gb = gs_ref[pl.ds(r0, CHUNK), :]
        bb = bs_ref[pl.ds(r0, CHUNK), :]
        gc = _dot(tril, gb, HIGHEST)
        g_i = gc[:, :CHUNK]
        g_j = _dot_nt(ones_cl, gc, HIGHEST) * (1.0 / LANES)
        decay = jnp.where(incl, jnp.exp(jnp.minimum(g_i - g_j, 0.0)), 0.0)
        k_beta = kc * bb
        kb16 = k_beta.astype(BF16)
        k16 = kc.astype(BF16)
        lower = jnp.where(strict, _dot_nt(kb16, k16) * decay, 0.0)
        a = -lower
        inv = eye + a
        pw = a
        for _ in range(5):
            pw = _dot(pw, pw, HIGHEST)
            inv = inv + _dot(inv, pw, HIGHEST)
        eg = jnp.exp(gc)
        u = _dot(inv, vc * bb, HIGHEST)
        w = _dot(inv, k_beta * eg, HIGHEST)
        intra = jnp.where(incl, _dot_nt(qc.astype(BF16), k16) * decay, 0.0)
        g_last = gc[CHUNK - 1:CHUNK, :]
        q_dec = (qc * eg).astype(BF16)
        k_dec = (kc * jnp.exp(g_last - gc)).astype(BF16)
        st = state_ref[...]
        st16 = st.astype(BF16)
        v_new = u - _dot(w.astype(BF16), st16)
        v16 = v_new.astype(BF16)
        o = _dot(q_dec, st16) + _dot(intra.astype(BF16), v16)
        state_ref[...] = st * jnp.exp(g_last[:, :1]) + _dot_tn(k_dec, v16)
        os_ref[pl.ds(r0, CHUNK), :] = o
        return carry

    lax.fori_loop(0, nchunk, chunk_step, 0)

    o = os_ref[...]
    ms = jnp.mean(o * o, axis=-1, keepdims=True)
    y = o * lax.rsqrt(ms + RMS_EPS) * gn_ref[...]
    o_ref[...] = (y * _silu(z_ref[...])).astype(o_ref.dtype)


def gdn_mixer(proj, conv_w, a_log, dt_bias, gnorm, *, batch, seq, rows):
    T = batch * seq
    nblk = seq // rows
    cb = lambda base: (lambda b, h, i: (b * nblk + i, base // LANES + h))
    pad = lambda v: jnp.zeros((1, LANES), F32).at[0, :GDN_HEADS].set(v)
    wspec = lambda base: pl.BlockSpec((GDN_CONV, LANES), lambda b, h, i: (0, base // LANES + h))
    vec = pl.BlockSpec((1, LANES), lambda b, h, i: (0, 0))
    return pl.pallas_call(
        functools.partial(_gdn_body, rows=rows),
        grid=(batch, GDN_HEADS, nblk),
        in_specs=[pl.BlockSpec((rows, LANES), cb(COL_AQ)),
                  pl.BlockSpec((rows, LANES), cb(COL_AK)),
                  pl.BlockSpec((rows, LANES), cb(COL_AV)),
                  pl.BlockSpec((rows, LANES), cb(COL_AZ)),
                  pl.BlockSpec((rows, LANES), lambda b, h, i: (b * nblk + i, COL_SMALL // LANES)),
                  wspec(0), wspec(512), wspec(1024), vec, vec, vec],
        out_specs=pl.BlockSpec((rows, LANES), lambda b, h, i: (b * nblk + i, h)),
        out_shape=jax.ShapeDtypeStruct((T, GDN_HEADS * GDN_D), BF16),
        scratch_shapes=[pltpu.VMEM((GDN_D, GDN_D), F32),
                        pltpu.VMEM((3, 8, LANES), F32),
                        pltpu.VMEM((rows, LANES), F32), pltpu.VMEM((rows, LANES), F32),
                        pltpu.VMEM((rows, LANES), F32), pltpu.VMEM((rows, LANES), F32),
                        pltpu.VMEM((rows, LANES), F32), pltpu.VMEM((rows, LANES), F32)],
        compiler_params=_cparams("parallel", "parallel", "arbitrary"),
        name="gdn_mixer",
    )(proj, proj, proj, proj, proj, conv_w, conv_w, conv_w, pad(a_log), pad(dt_bias),
      gnorm.reshape(1, GDN_D))


def _qk_rope_body(q_ref, k_ref, cos_ref, sin_ref, gq_ref, gk_ref, gm_ref, qo_ref, ko_ref):
    lane = _iota((1, 2 * LANES), 1)
    first_half = (lane % SWA_DH) < (SWA_DH // 2)
    cosf = cos_ref[...]
    sins = sin_ref[...]

    def norm_rope(x, gain):
        ms = _dot(x * x, gm_ref[...], HIGHEST)
        xn = x * lax.rsqrt(ms + RMS_EPS) * gain
        other = jnp.where(first_half, pltpu.roll(xn, 2 * LANES - SWA_DH // 2, 1),
                          pltpu.roll(xn, SWA_DH // 2, 1))
        return xn * cosf + other * sins

    qo_ref[...] = (norm_rope(q_ref[...], gq_ref[...]) * (SWA_DH ** -0.5)).astype(qo_ref.dtype)
    ko_ref[...] = norm_rope(k_ref[...], gk_ref[...]).astype(ko_ref.dtype)


def qk_norm_rope(proj, cosf, sins, q_gain, k_gain, *, tm):
    T = proj.shape[0]
    W = SWA_HEADS * SWA_DH
    grp = jnp.arange(W) // SWA_DH
    gmean = (grp[:, None] == grp[None, :]).astype(F32) / SWA_DH
    tile = lambda v: jnp.tile(v, SWA_HEADS).reshape(1, W)
    vec = pl.BlockSpec((1, W), lambda g, i: (0, 0))
    out = jax.ShapeDtypeStruct((len(SWA_GROUPS), T, W), BF16)
    return pl.pallas_call(
        _qk_rope_body,
        grid=(len(SWA_GROUPS), T // tm),
        in_specs=[pl.BlockSpec((tm, W), lambda g, i: (i, COL_BQ // W + g)),
                  pl.BlockSpec((tm, W), lambda g, i: (i, COL_BK // W + g)),
                  pl.BlockSpec((tm, W), lambda g, i: (i, 0)),
                  pl.BlockSpec((tm, W), lambda g, i: (i, 0)),
                  vec, vec, pl.BlockSpec((W, W), lambda g, i: (0, 0))],
        out_specs=[pl.BlockSpec((None, tm, W), lambda g, i: (g, i, 0)),
                   pl.BlockSpec((None, tm, W), lambda g, i: (g, i, 0))],
        out_shape=[out, out],
        compiler_params=_cparams("parallel", "parallel"),
        name="qk_norm_rope",
    )(proj, proj, cosf, sins, tile(q_gain), tile(k_gain), gmean)


def _band_attn_body(q_ref, kp_ref, kc_ref, vp_ref, vc_ref, o_ref, lse_ref, *, blocks_per_residue):
    i = pl.program_id(1)
    first = (i % blocks_per_residue) == 0
    c = SWA_BACK
    W = SWA_HEADS * SWA_DH
    q = q_ref[...]
    kcat = jnp.concatenate([kp_ref[...], kc_ref[...]], axis=0)
    vcat = jnp.concatenate([vp_ref[...], vc_ref[...]], axis=0)
    a = _iota((c, 2 * c), 0)
    b = _iota((c, 2 * c), 1)
    dist = a + c - b
    valid = (dist >= 0) & (dist <= SWA_BACK) & ((b >= c) | jnp.logical_not(first))
    lane = _iota((1, W), 1)
    o_acc = jnp.zeros((c, W), F32)
    lse_acc = jnp.zeros((c, W), F32)
    for hd in range(SWA_HEADS):
        hmask = (lane // SWA_DH) == hd
        qh = jnp.where(hmask, q, jnp.zeros_like(q))
        s = jnp.where(valid, _dot_nt(qh, kcat), -jnp.inf)
        m = jnp.max(s, axis=-1, keepdims=True)
        p = jnp.exp(s - m)
        l = jnp.sum(p, axis=-1, keepdims=True)
        pv = _dot(p.astype(BF16), vcat) / l
        o_acc = jnp.where(hmask, pv, o_acc)
        lse_acc = jnp.where(hmask, m + jnp.log(l), lse_acc)
    o_ref[...] = o_acc
    lse_ref[...] = lse_acc


def band_attention(q, k, v, *, dilation):
    B, S, W = q.shape
    c = SWA_BACK
    nblk = S // c
    cur = pl.BlockSpec((None, c, W), lambda b, i: (b, i, 0))
    prev = pl.BlockSpec((None, c, W), lambda b, i: (b, jnp.maximum(i - 1, 0), 0))
    out = jax.ShapeDtypeStruct((B, S, W), F32)
    return pl.pallas_call(
        functools.partial(_band_attn_body, blocks_per_residue=nblk // dilation),
        grid=(B, nblk),
        in_specs=[cur, prev, cur, prev, cur],
        out_specs=[cur, cur],
        out_shape=[out, out],
        compiler_params=_cparams("parallel", "parallel"),
        name=f"band_attention_d{dilation}",
    )(q, k, k, v, v)


def _gla_body(q_ref, k_ref, v_ref, r_ref, sm_ref, wg_ref, bg_ref, gn_ref, o_ref,
              state_ref, qs_ref, ks_ref, la_ref, os_ref, *, rows):
    blk = pl.program_id(1)
    nchunk = rows // CHUNK
    C = CHUNK

    @pl.when(blk == 0)
    def _():
        state_ref[...] = jnp.zeros_like(state_ref)

    x = _dot(sm_ref[...].astype(BF16), wg_ref[...]) + bg_ref[...]
    la_ref[...] = (jnp.minimum(x, 0.0) - jnp.log(1.0 + jnp.exp(-jnp.abs(x)))) * (1.0 / GLA_TAU)
    qs_ref[...] = q_ref[...] * (GLA_DK ** -0.5)
    ks_ref[...] = k_ref[...]

    ri = _iota((C, C), 0)
    ci = _iota((C, C), 1)
    tril = (ri >= ci).astype(F32)
    lane = _iota((1, LANES), 1)
    lrow = _iota((LANES, C), 0)
    gsum = [((lrow // GLA_DK) == h2).astype(BF16) for h2 in range(2)]
    hmask = [(lane // GLA_DK) == h2 for h2 in range(2)]
    nsub = C // GLA_SUB

    def chunk_step(c, carry):
        r0 = pl.multiple_of(c * C, C)
        for pair in range(GLA_HEADS // 2):
            cols = pl.ds(pair * LANES, LANES)
            qc = qs_ref[pl.ds(r0, C), cols]
            kc = ks_ref[pl.ds(r0, C), cols]
            la = la_ref[pl.ds(r0, C), cols]
            bcum = _dot(tril, la, HIGHEST)
            b_last = bcum[C - 1:C, :]
            s_rows = [[jnp.zeros((GLA_SUB, C), F32)] for _ in range(2)]
            for sb in range(1, nsub):
                lo = sb * GLA_SUB
                bref = bcum[lo - 1:lo, :]
                q_sb = (qc[lo:lo + GLA_SUB] * jnp.exp(bcum[lo:lo + GLA_SUB] - bref))
                k_sb = (kc * jnp.exp(jnp.minimum(bref - bcum, 0.0))).astype(BF16)
                for h2 in range(2):
                    qm = jnp.where(hmask[h2], q_sb, 0.0).astype(BF16)
                    s_rows[h2].append(_dot_nt(qm, k_sb))
            scores = []
            for h2 in range(2):
                s_off = jnp.concatenate(s_rows[h2], axis=0)
                scores.append(jnp.where((ri // GLA_SUB) > (ci // GLA_SUB), s_off, 0.0))
            for off in range(GLA_SUB):
                if off == 0:
                    prod = qc * kc
                else:
                    k_sh = pltpu.roll(kc, off, 0)
                    b_sh = pltpu.roll(bcum, off, 0)
                    prod = qc * k_sh * jnp.exp(jnp.minimum(bcum - b_sh, 0.0))
                p16 = prod.astype(BF16)
                on_diag = ((ri - ci) == off) & ((ri // GLA_SUB) == (ci // GLA_SUB))
                for h2 in range(2):
                    d = _dot(p16, gsum[h2])
                    scores[h2] = jnp.where(on_diag, d, scores[h2])
            q_dec = qc * jnp.exp(bcum)
            k_dec = kc * jnp.exp(b_last - bcum)
            dec_last = jnp.exp(b_last)
            for h2 in range(2):
                hd = pair * 2 + h2
                vc = v_ref[pl.ds(r0, C), pl.ds(hd * GLA_DV, GLA_DV)].astype(BF16)
                st = state_ref[hd]
                qd = jnp.where(hmask[h2], q_dec, 0.0).astype(BF16)
                kd = jnp.where(hmask[h2], k_dec, 0.0).astype(BF16)
                o = _dot(qd, st.astype(BF16)) + _dot(scores[h2].astype(BF16), vc)
                dl_col = jnp.sum(jnp.where(_iota((LANES, LANES), 0) == _iota((LANES, LANES), 1),
                                           jnp.broadcast_to(dec_last, (LANES, LANES)), 0.0),
                                 axis=-1, keepdims=True)
                state_ref[hd] = st * dl_col + _dot_tn(kd, vc)
                os_ref[pl.ds(r0, C), pl.ds(hd * GLA_DV, GLA_DV)] = o
        return carry

    lax.fori_loop(0, nchunk, chunk_step, 0)

    for hd in range(GLA_HEADS):
        cols = pl.ds(hd * GLA_DV, GLA_DV)
        o = os_ref[:, cols]
        ms = jnp.mean(o * o, axis=-1, keepdims=True)
        y = o * lax.rsqrt(ms + RMS_EPS) * gn_ref[...]
        o_ref[:, cols] = (y * _silu(r_ref[:, cols])).astype(o_ref.dtype)


def gla_mixer(proj, gate_up, gate_bias, gnorm, *, batch, seq, rows):
    T = batch * seq
    nblk = seq // rows
    QW = GLA_HEADS * GLA_DK
    VW = GLA_HEADS * GLA_DV
    row = lambda b, i: b * nblk + i
    wg = jnp.zeros((LANES, QW), F32).at[8:8 + GLA_RANK].set(gate_up).astype(BF16)
    return pl.pallas_call(
        functools.partial(_gla_body, rows=rows),
        grid=(batch, nblk),
        in_specs=[pl.BlockSpec((rows, QW), lambda b, i: (row(b, i), COL_CQ // QW)),
                  pl.BlockSpec((rows, QW), lambda b, i: (row(b, i), COL_CK // QW)),
                  pl.BlockSpec((rows, VW), lambda b, i: (row(b, i), COL_CV // VW)),
                  pl.BlockSpec((rows, VW), lambda b, i: (row(b, i), COL_CR // VW)),
                  pl.BlockSpec((rows, LANES), lambda b, i: (row(b, i), COL_SMALL // LANES)),
                  pl.BlockSpec((LANES, QW), lambda b, i: (0, 0)),
                  pl.BlockSpec((1, QW), lambda b, i: (0, 0)),
                  pl.BlockSpec((1, GLA_DV), lambda b, i: (0, 0))],
        out_specs=pl.BlockSpec((rows, VW), lambda b, i: (row(b, i), 0)),
        out_shape=jax.ShapeDtypeStruct((T, VW), BF16),
        scratch_shapes=[pltpu.VMEM((GLA_HEADS, LANES, GLA_DV), F32),
                        pltpu.VMEM((rows, QW), F32), pltpu.VMEM((rows, QW), F32),
                        pltpu.VMEM((rows, QW), F32), pltpu.VMEM((rows, VW), F32)],
        compiler_params=_cparams("parallel", "arbitrary"),
        name="gla_mixer",
    )(proj, proj, proj, proj, proj, wg, gate_bias.reshape(1, QW), gnorm.reshape(1, GLA_DV))


def _mix_out_body(x_ref, g0_ref, g1_ref, g2_ref, gb_ref, ya_ref, yc_ref,
                  o0_ref, o1_ref, o2_ref, l0_ref, l1_ref, l2_ref,
                  wa_ref, wb_ref, wc_ref, wo_ref, out_ref):
    l0, l1, l2 = l0_ref[...], l1_ref[...], l2_ref[...]
    m = jnp.maximum(jnp.maximum(l0, l1), l2)
    e0, e1, e2 = jnp.exp(l0 - m), jnp.exp(l1 - m), jnp.exp(l2 - m)
    ob = (e0 * o0_ref[...] + e1 * o1_ref[...] + e2 * o2_ref[...]) / (e0 + e1 + e2)
    gb = gb_ref[...]
    y = _sigmoid(g0_ref[...] + gb[0:1]) * _dot(ya_ref[...], wa_ref[...])
    y = y + _sigmoid(g1_ref[...] + gb[1:2]) * _dot(ob.astype(BF16), wb_ref[...])
    y = y + _sigmoid(g2_ref[...] + gb[2:3]) * _dot(yc_ref[...], wc_ref[...])
    out_ref[...] = x_ref[...] + _dot(y.astype(BF16), wo_ref[...])


def mix_out(x, proj, gate_bias, ya, yc, o_grp, lse_grp, wa, wb, wc, wo, *, tm):
    T, D = x.shape
    W = SWA_HEADS * SWA_DH
    rowblk = lambda w: pl.BlockSpec((tm, w), lambda i: (i, 0))
    full = lambda a: pl.BlockSpec(a.shape, lambda i: (0, 0))
    gate = lambda n: pl.BlockSpec((tm, D), lambda i: (i, COL_GATES // D + n))
    return pl.pallas_call(
        _mix_out_body,
        grid=(T // tm,),
        in_specs=[rowblk(D), gate(0), gate(1), gate(2), full(gate_bias), rowblk(ya.shape[1]),
                  rowblk(yc.shape[1]), rowblk(W), rowblk(W), rowblk(W), rowblk(W), rowblk(W), rowblk(W),
                  full(wa), full(wb), full(wc), full(wo)],
        out_specs=rowblk(D),
        out_shape=jax.ShapeDtypeStruct((T, D), F32),
        compiler_params=_cparams("parallel"),
        name="mix_out",
    )(x, proj, proj, proj, gate_bias, ya, yc, *o_grp, *lse_grp, wa, wb, wc, wo)


def _cross_attn_body(x_ref, gx_ref, wq_ref, kv_ref, gq_ref, gk_ref, wo_ref, out_ref):
    x = x_ref[...]
    ms = jnp.mean(x * x, axis=-1, keepdims=True)
    h = (x * lax.rsqrt(ms + RMS_EPS) * gx_ref[...]).astype(BF16)
    q = _dot(h, wq_ref[...])
    kv = kv_ref[...]
    KW = XA_HEADS * XA_DH
    outs = []
    for hd in range(XA_HEADS):
        qh = q[:, hd * XA_DH:(hd + 1) * XA_DH]
        kh = kv[:, hd * XA_DH:(hd + 1) * XA_DH]
        vh = kv[:, KW + hd * XA_DH:KW + (hd + 1) * XA_DH]
        qn = qh * lax.rsqrt(jnp.mean(qh * qh, axis=-1, keepdims=True) + RMS_EPS) * gq_ref[...]
        kn = kh * lax.rsqrt(jnp.mean(kh * kh, axis=-1, keepdims=True) + RMS_EPS) * gk_ref[...]
        s = _dot_nt(qn.astype(BF16), kn.astype(BF16)) * (XA_DH ** -0.5)
        m = jnp.max(s, axis=-1, keepdims=True)
        p = jnp.exp(s - m)
        l = jnp.sum(p, axis=-1, keepdims=True)
        outs.append((_dot(p.astype(BF16), vh.astype(BF16)) / l).astype(BF16))
    o = jnp.concatenate(outs, axis=-1)
    out_ref[...] = x + _dot(o, wo_ref[...])


def cross_attention(x, kv, gx, wq, gq, gk, wo, *, batch, seq, mem_len, tm):
    T, D = x.shape
    per_batch = seq // tm
    full = lambda a: pl.BlockSpec(a.shape, lambda i: (0, 0))
    gx, gq, gk = gx.reshape(1, D), gq.reshape(1, XA_DH), gk.reshape(1, XA_DH)
    return pl.pallas_call(
        _cross_attn_body,
        grid=(T // tm,),
        in_specs=[pl.BlockSpec((tm, D), lambda i: (i, 0)), full(gx), full(wq),
                  pl.BlockSpec((mem_len, kv.shape[1]), lambda i: (i // per_batch, 0)),
                  full(gq), full(gk), full(wo)],
        out_specs=pl.BlockSpec((tm, D), lambda i: (i, 0)),
        out_shape=jax.ShapeDtypeStruct((T, D), F32),
        compiler_params=_cparams("parallel"),
        name="cross_attention",
    )(x, gx, wq, kv, gq, gk, wo)


def _router_body(x_ref, gx_ref, wr_ref, br_ref, h_ref, idx_ref, gate_ref):
    x = x_ref[...]
    ms = jnp.mean(x * x, axis=-1, keepdims=True)
    h = x * lax.rsqrt(ms + RMS_EPS) * gx_ref[...]
    h_ref[...] = h
    lane = _iota((1, LANES), 1)
    logits = _dot(h, wr_ref[...], HIGHEST) + br_ref[...]
    logits = jnp.where(lane < N_EXPERTS, logits, -jnp.inf)
    idx_out = jnp.zeros(logits.shape, jnp.int32)
    val_out = jnp.full(logits.shape, -jnp.inf, F32)
    for k in range(TOP_K):
        m = jnp.max(logits, axis=-1, keepdims=True)
        sel = jnp.min(jnp.where(logits == m, lane, LANES), axis=-1, keepdims=True)
        idx_out = jnp.where(lane == k, sel, idx_out)
        val_out = jnp.where(lane == k, m, val_out)
        logits = jnp.where(lane == sel, -jnp.inf, logits)
    top = jnp.max(val_out, axis=-1, keepdims=True)
    e = jnp.exp(val_out - top)
    idx_ref[...] = idx_out
    gate_ref[...] = e / jnp.sum(e, axis=-1, keepdims=True)


def moe_router(x, gx, wr, br, *, tm):
    T, D = x.shape
    wr_p = jnp.zeros((D, LANES), F32).at[:, :N_EXPERTS].set(wr)
    br_p = jnp.zeros((1, LANES), F32).at[0, :N_EXPERTS].set(br)
    full = lambda a: pl.BlockSpec(a.shape, lambda i: (0, 0))
    gx = gx.reshape(1, D)
    return pl.pallas_call(
        _router_body,
        grid=(T // tm,),
        in_specs=[pl.BlockSpec((tm, D), lambda i: (i, 0)), full(gx), full(wr_p), full(br_p)],
        out_specs=[pl.BlockSpec((tm, D), lambda i: (i, 0)),
                   pl.BlockSpec((tm, LANES), lambda i: (i, 0)),
                   pl.BlockSpec((tm, LANES), lambda i: (i, 0))],
        out_shape=[jax.ShapeDtypeStruct((T, D), F32),
                   jax.ShapeDtypeStruct((T, LANES), jnp.int32),
                   jax.ShapeDtypeStruct((T, LANES), F32)],
        compiler_params=_cparams("parallel"),
        name="moe_router",
    )(x, gx, wr_p, br_p)


def _gather_rows_body(idx_ref, src_ref, out_ref, sem, *, chunk):
    base = pl.program_id(0) * chunk

    def row_copy(j):
        return pltpu.make_async_copy(src_ref.at[idx_ref[0, 0, j]], out_ref.at[base + j], sem)

    def issue(j, carry):
        row_copy(j).start()
        return carry

    def drain(j, carry):
        row_copy(j).wait()
        return carry

    lax.fori_loop(0, chunk, issue, 0)
    lax.fori_loop(0, chunk, drain, 0)


def gather_rows(src, idx, *, chunk, name):
    N, D = src.shape
    P = idx.shape[0]
    sub = D // LANES
    out = pl.pallas_call(
        functools.partial(_gather_rows_body, chunk=chunk),
        grid=(P // chunk,),
        in_specs=[pl.BlockSpec((1, 1, chunk), lambda i: (i, 0, 0), memory_space=pltpu.SMEM),
                  pl.BlockSpec(memory_space=pl.ANY)],
        out_specs=pl.BlockSpec(memory_space=pl.ANY),
        out_shape=jax.ShapeDtypeStruct((P, sub, LANES), F32),
        scratch_shapes=[pltpu.SemaphoreType.DMA(())],
        compiler_params=_cparams("arbitrary"),
        name=name,
    )(idx.reshape(P // chunk, 1, chunk), src.reshape(N, sub, LANES))
    return out.reshape(P, D)


def _expert_body(be_ref, nused_ref, x_ref, wi_ref, bi_ref, wo_ref, bo_ref, y_ref):
    @pl.when(pl.program_id(0) < nused_ref[0])
    def _():
        hh = _dot(x_ref[...].astype(BF16), wi_ref[...]) + bi_ref[...]
        glu = jnp.minimum(hh[:, :D_EXPERT], SWIGLU_LIMIT)
        lin = jnp.clip(hh[:, D_EXPERT:], -SWIGLU_LIMIT, SWIGLU_LIMIT)
        act = glu * _sigmoid(SWIGLU_ALPHA * glu) * (lin + 1.0)
        y_ref[...] = _dot(act.astype(BF16), wo_ref[...]) + bo_ref[...]

    @pl.when(pl.program_id(0) >= nused_ref[0])
    def _():
        y_ref[...] = jnp.zeros_like(y_ref)


def expert_ffn(xs, block_expert, n_used, w_in, b_in, w_out, b_out):
    P, D = xs.shape
    E, _, F2 = w_in.shape
    grid_spec = pltpu.PrefetchScalarGridSpec(
        num_scalar_prefetch=2,
        grid=(P // MOE_BM,),
        in_specs=[pl.BlockSpec((MOE_BM, D), lambda i, be, nu: (i, 0)),
                  pl.BlockSpec((None, D, F2), lambda i, be, nu: (be[i], 0, 0)),
                  pl.BlockSpec((None, 1, F2), lambda i, be, nu: (be[i], 0, 0)),
                  pl.BlockSpec((None, F2 // 2, D), lambda i, be, nu: (be[i], 0, 0)),
                  pl.BlockSpec((None, 1, D), lambda i, be, nu: (be[i], 0, 0))],
        out_specs=pl.BlockSpec((MOE_BM, D), lambda i, be, nu: (i, 0)),
    )
    return pl.pallas_call(
        _expert_body,
        grid_spec=grid_spec,
        out_shape=jax.ShapeDtypeStruct((P, D), F32),
        compiler_params=_cparams("arbitrary"),
        name="expert_ffn",
    )(block_expert, n_used, xs, w_in, b_in.reshape(E, 1, F2), w_out, b_out.reshape(E, 1, D))


def _moe_combine_body(x_ref, y_ref, gate_ref, out_ref):
    D = x_ref.shape[1]
    g = gate_ref[...]
    acc = x_ref[...]
    for k in range(TOP_K):
        acc = acc + g[:, k:k + 1] * y_ref[:, k * D:(k + 1) * D]
    out_ref[...] = acc


def moe_combine(x, y_tok, gates, *, tm):
    T, D = x.shape
    return pl.pallas_call(
        _moe_combine_body,
        grid=(T // tm,),
        in_specs=[pl.BlockSpec((tm, D), lambda i: (i, 0)),
                  pl.BlockSpec((tm, TOP_K * D), lambda i: (i, 0)),
                  pl.BlockSpec((tm, LANES), lambda i: (i, 0))],
        out_specs=pl.BlockSpec((tm, D), lambda i: (i, 0)),
        out_shape=jax.ShapeDtypeStruct((T, D), F32),
        compiler_params=_cparams("parallel"),
        name="moe_combine",
    )(x, y_tok, gates)


def _pack_w_in(w_in):
    L, D, _ = w_in.shape
    o_alpha = 2048
    o_b = 2056
    o_c = o_b + 2304
    o_low = o_c + 1536
    o_gates = o_low + GLA_RANK
    c = w_in[:, :, o_c:o_low]
    parts = [w_in[:, :, o_gates:],
             w_in[:, :, 0:2048],
             w_in[:, :, o_b:o_c],
             w_in[:, :, o_alpha:o_b], w_in[:, :, o_low:o_gates],
             jnp.zeros((L, D, 256 - 8 - GLA_RANK), w_in.dtype),
             c[:, :, 512:1024], c[:, :, 1024:1536], c[:, :, 0:256], c[:, :, 256:512]]
    packed = jnp.concatenate(parts, axis=-1)
    assert packed.shape[-1] == N_PACKED
    return packed.astype(BF16)


def _to_residue_major(t, dilation):
    if dilation == 1:
        return t
    *lead, S, W = t.shape
    return jnp.swapaxes(t.reshape(*lead, S // dilation, dilation, W), -2, -3).reshape(*lead, S, W)


def _from_residue_major(t, dilation):
    if dilation == 1:
        return t
    *lead, S, W = t.shape
    return jnp.swapaxes(t.reshape(*lead, dilation, S // dilation, W), -2, -3).reshape(*lead, S, W)


def _routing_tables(idx, n_tokens):
    A = n_tokens * TOP_K
    P = A + N_EXPERTS * MOE_BM
    e_flat = idx.reshape(A)
    onehot = (e_flat[:, None] == jnp.arange(N_EXPERTS, dtype=jnp.int32)[None, :]).astype(jnp.int32)
    csum = jnp.cumsum(onehot, axis=0)
    rank = jnp.sum(onehot * csum, axis=1) - 1
    counts = csum[-1]
    padded = (counts + MOE_BM - 1) // MOE_BM * MOE_BM
    pad_ends = jnp.cumsum(padded)
    pad_starts = pad_ends - padded
    dest = (pad_starts[e_flat] + rank).astype(jnp.int32)
    tok_buf = jnp.zeros((P,), jnp.int32).at[dest].set(jnp.arange(A, dtype=jnp.int32) // TOP_K)
    block_expert = jnp.minimum(
        jnp.searchsorted(pad_ends, jnp.arange(P // MOE_BM, dtype=jnp.int32) * MOE_BM, side="right"),
        N_EXPERTS - 1).astype(jnp.int32)
    n_used = (pad_ends[-1:] // MOE_BM).astype(jnp.int32)
    return dest, tok_buf, block_expert, n_used


def kernel(x, mem, positions, norm_mix, w_in, gate_bias, gdn_conv, gdn_a_log, gdn_dt_bias, gdn_norm, swa_q_norm, swa_k_norm, gla_gate_up, gla_gate_bias, gla_norm, w_branch_a, w_branch_b, w_branch_c, w_mix_out, norm_cross, norm_mem, xa_wq, xa_wkv, xa_q_norm, xa_k_norm, xa_wo, norm_ffn, router_w, router_b, moe_w_in, moe_b_in, moe_w_out, moe_b_out):
    B, S, D = x.shape
    T = B * S
    M = mem.shape[1]
    depth = w_in.shape[0]
    W = SWA_HEADS * SWA_DH

    inv_freq = ROPE_THETA ** (-jnp.arange(0, SWA_DH, 2, dtype=F32) / SWA_DH)
    ang = positions.astype(F32).reshape(T, 1) * inv_freq[None, :]
    cos, sin = jnp.cos(ang), jnp.sin(ang)
    cosf = jnp.tile(jnp.concatenate([cos, cos], axis=-1), (1, SWA_HEADS))
    sins = jnp.tile(jnp.concatenate([-sin, sin], axis=-1), (1, SWA_HEADS))

    w_in_p = _pack_w_in(w_in)
    xf = x.reshape(T, D)
    memf = mem.reshape(B * M, D)

    for l in range(depth):
        proj = norm_matmul(xf, norm_mix[l], w_in_p[l], tm=1024, tn=1536, name="in_proj")
        ya = gdn_mixer(proj, gdn_conv[l], gdn_a_log[l], gdn_dt_bias[l], gdn_norm[l], batch=B, seq=S, rows=512)
        qn, kn = qk_norm_rope(proj, cosf, sins, swa_q_norm[l], swa_k_norm[l], tm=1024)
        o_grp, lse_grp = [], []
        for gi, (window, dil) in enumerate(SWA_GROUPS):
            assert window // dil == SWA_BACK
            vg = proj[:, COL_BV + gi * W:COL_BV + (gi + 1) * W].astype(BF16)
            rm = lambda t: _to_residue_major(t.reshape(B, S, W), dil)
            o_g, lse_g = band_attention(rm(qn[gi]), rm(kn[gi]), rm(vg), dilation=dil)
            o_grp.append(_from_residue_major(o_g, dil).reshape(T, W))
            lse_grp.append(_from_residue_major(lse_g, dil).reshape(T, W))
        yc = gla_mixer(proj, gla_gate_up[l], gla_gate_bias[l], gla_norm[l], batch=B, seq=S, rows=512)
        xf = mix_out(xf, proj, gate_bias[l], ya, yc, o_grp, lse_grp,
                     w_branch_a[l].astype(BF16), w_branch_b[l].astype(BF16), w_branch_c[l].astype(BF16),
                     w_mix_out[l].astype(BF16), tm=512)

        kv = norm_matmul(memf, norm_mem[l], xa_wkv[l].astype(BF16), tm=min(1024, B * M), tn=1024, name="mem_kv")
        xf = cross_attention(xf, kv, norm_cross[l], xa_wq[l].astype(BF16), xa_q_norm[l], xa_k_norm[l],
                             xa_wo[l].astype(BF16), batch=B, seq=S, mem_len=M, tm=512)

        h, idx, gates = moe_router(xf, norm_ffn[l], router_w[l], router_b[l], tm=1024)
        dest, tok_buf, block_expert, n_used = _routing_tables(idx[:, :TOP_K], T)
        xs = gather_rows(h, tok_buf, chunk=256, name="moe_gather_in")
        yb = expert_ffn(xs, block_expert, n_used, moe_w_in[l].astype(BF16), moe_b_in[l],
                        moe_w_out[l].astype(BF16), moe_b_out[l])
        y_tok = gather_rows(yb, dest, chunk=256, name="moe_gather_out").reshape(T, TOP_K * D)
        xf = moe_combine(xf, y_tok, gates, tm=512)

    return xf.reshape(B, S, D)
```

```python
import functools

import jax
import jax.numpy as jnp
from jax import lax
from jax.experimental import pallas as pl
from jax.experimental.pallas import tpu as pltpu

F32 = jnp.float32
BF16 = jnp.bfloat16
HIGHEST = lax.Precision.HIGHEST

RMS_EPS = 1e-6
L2_EPS = 1e-6
LANES = 128
VMEM_LIMIT = 56 * 1024 * 1024

D_MODEL = 1024
GDN_HEADS, GDN_D, GDN_CONV, CHUNK = 4, 128, 4, 64
SWA_GROUPS = ((128, 1), (512, 4), (2048, 16))
SWA_HEADS, SWA_DH, SWA_BACK = 4, 64, 128
ROPE_THETA = 10000.0
GLA_HEADS, GLA_DK, GLA_DV, GLA_RANK, GLA_TAU = 4, 64, 128, 16, 16.0
GLA_SUB = 8
XA_HEADS, XA_DH = 4, 128
N_EXPERTS, TOP_K, D_EXPERT = 32, 4, 1024
SWIGLU_ALPHA, SWIGLU_LIMIT = 1.702, 7.0
MOE_BM = 256

COL_GATES = 0
COL_AQ, COL_AK, COL_AV, COL_AZ = 3072, 3584, 4096, 4608
COL_BQ, COL_BK, COL_BV = 5120, 5888, 6656
COL_SMALL = 7424
COL_CV, COL_CR, COL_CQ, COL_CK = 7680, 8192, 8704, 8960
N_PACKED = 9216


def _cparams(*sem):
    return pltpu.CompilerParams(dimension_semantics=sem, vmem_limit_bytes=VMEM_LIMIT)


def _sigmoid(x):
    return 1.0 / (1.0 + jnp.exp(-x))


def _silu(x):
    return x * _sigmoid(x)


def _softplus(x):
    return jnp.maximum(x, 0.0) + jnp.log(1.0 + jnp.exp(-jnp.abs(x)))


def _dot(a, b, precision=None):
    return jnp.dot(a, b, preferred_element_type=F32, precision=precision)


def _dot_nt(a, b, precision=None):
    return lax.dot_general(a, b, (((1,), (1,)), ((), ())), preferred_element_type=F32, precision=precision)


def _dot_tn(a, b, precision=None):
    return lax.dot_general(a, b, (((0,), (0,)), ((), ())), preferred_element_type=F32, precision=precision)


def _iota(shape, axis):
    return lax.broadcasted_iota(jnp.int32, shape, axis)


def _norm_matmul_body(x_ref, g_ref, w_ref, o_ref, h_ref):
    @pl.when(pl.program_id(1) == 0)
    def _():
        x = x_ref[...]
        ms = jnp.mean(x * x, axis=-1, keepdims=True)
        h_ref[...] = (x * lax.rsqrt(ms + RMS_EPS) * g_ref[...]).astype(h_ref.dtype)

    o_ref[...] = _dot(h_ref[...], w_ref[...]).astype(o_ref.dtype)


def norm_matmul(x, gain, w, *, tm, tn, name):
    T, D = x.shape
    N = w.shape[1]
    return pl.pallas_call(
        _norm_matmul_body,
        grid=(T // tm, N // tn),
        in_specs=[pl.BlockSpec((tm, D), lambda i, j: (i, 0)),
                  pl.BlockSpec((1, D), lambda i, j: (0, 0)),
                  pl.BlockSpec((D, tn), lambda i, j: (0, j))],
        out_specs=pl.BlockSpec((tm, tn), lambda i, j: (i, j)),
        out_shape=jax.ShapeDtypeStruct((T, N), F32),
        scratch_shapes=[pltpu.VMEM((tm, D), BF16)],
        compiler_params=_cparams("parallel", "arbitrary"),
        name=name,
    )(x, gain.reshape(1, D), w)


def _gdn_body(q_ref, k_ref, v_ref, z_ref, sm_ref, cq_ref, ck_ref, cv_ref, alog_ref, dtb_ref, gn_ref,
              o_ref, state_ref, tail_ref, qs_ref, ks_ref, vs_ref, gs_ref, bs_ref, os_ref, *, rows):
    h = pl.program_id(1)
    blk = pl.program_id(2)
    nchunk = rows // CHUNK

    @pl.when(blk == 0)
    def _():
        state_ref[...] = jnp.zeros_like(state_ref)
        tail_ref[...] = jnp.zeros_like(tail_ref)

    def conv_silu(x_ref, w_ref, slot):
        x = x_ref[...]
        xp = jnp.concatenate([tail_ref[slot], x], axis=0)
        w = w_ref[...]
        acc = x * w[GDN_CONV - 1:GDN_CONV, :]
        for s in range(1, GDN_CONV):
            acc = acc + pltpu.roll(xp, s, 0)[8:] * w[GDN_CONV - 1 - s:GDN_CONV - s, :]
        tail_ref[slot] = x[rows - 8:rows]
        return _silu(acc)

    q = conv_silu(q_ref, cq_ref, 0)
    k = conv_silu(k_ref, ck_ref, 1)
    v = conv_silu(v_ref, cv_ref, 2)
    q = q * lax.rsqrt(jnp.sum(q * q, axis=-1, keepdims=True) + L2_EPS) * (GDN_D ** -0.5)
    k = k * lax.rsqrt(jnp.sum(k * k, axis=-1, keepdims=True) + L2_EPS)

    lane = _iota((1, LANES), 1)
    sm = sm_ref[...]
    alpha = jnp.sum(jnp.where(lane == h, sm, 0.0), axis=-1, keepdims=True)
    beta_raw = jnp.sum(jnp.where(lane == h + GDN_HEADS, sm, 0.0), axis=-1, keepdims=True)
    a_log = jnp.sum(jnp.where(lane == h, alog_ref[...], 0.0), axis=-1, keepdims=True)
    dt_b = jnp.sum(jnp.where(lane == h, dtb_ref[...], 0.0), axis=-1, keepdims=True)
    g = -jnp.exp(a_log) * _softplus(alpha + dt_b)
    beta = _sigmoid(beta_raw)

    qs_ref[...] = q
    ks_ref[...] = k
    vs_ref[...] = v
    gs_ref[...] = jnp.broadcast_to(g, (rows, LANES))
    bs_ref[...] = jnp.broadcast_to(beta, (rows, LANES))

    ri = _iota((CHUNK, CHUNK), 0)
    ci = _iota((CHUNK, CHUNK), 1)
    incl = ri >= ci
    strict = ri > ci
    tril = incl.astype(F32)
    eye = (ri == ci).astype(F32)
    ones_cl = jnp.ones((CHUNK, LANES), F32)

    def chunk_step(c, carry):
        r0 = pl.multiple_of(c * CHUNK, CHUNK)
        qc = qs_ref[pl.ds(r0, CHUNK), :]
        kc = ks_ref[pl.ds(r0, CHUNK), :]
        vc = vs_ref[pl.ds(r0, CHUNK), :]
        gb = gs_ref[pl.ds(r0, CHUNK), :]
        bb = bs_ref[pl.ds(r0, CHUNK), :]
        gc = _dot(tril, gb, HIGHEST)
        g_i = gc[:, :CHUNK]
        g_j = _dot_nt(ones_cl, gc, HIGHEST) * (1.0 / LANES)
        decay = jnp.where(incl, jnp.exp(jnp.minimum(g_i - g_j, 0.0)), 0.0)
        k_beta = kc * bb
        kb16 = k_beta.astype(BF16)
        k16 = kc.astype(BF16)
        lower = jnp.where(strict, _dot_nt(kb16, k16) * decay, 0.0)
        a = -lower
        inv = eye + a
        pw = a
        for _ in range(5):
            pw = _dot(pw, pw, HIGHEST)
            inv = inv + _dot(inv, pw, HIGHEST)
        eg = jnp.exp(gc)
        u = _dot(inv, vc * bb, HIGHEST)
        w = _dot(inv, k_beta * eg, HIGHEST)
        intra = jnp.where(incl, _dot_nt(qc.astype(BF16), k16) * decay, 0.0)
        g_last = gc[CHUNK - 1:CHUNK, :]
        q_dec = (qc * eg).astype(BF16)
        k_dec = (kc * jnp.exp(g_last - gc)).astype(BF16)
        st = state_ref[...]
        st16 = st.astype(BF16)
        v_new = u - _dot(w.astype(BF16), st16)
        v16 = v_new.astype(BF16)
        o = _dot(q_dec, st16) + _dot(intra.astype(BF16), v16)
        state_ref[...] = st * jnp.exp(g_last[:, :1]) + _dot_tn(k_dec, v16)
        os_ref[pl.ds(r0, CHUNK), :] = o
        return carry

    lax.fori_loop(0, nchunk, chunk_step, 0)

    o = os_ref[...]
    ms = jnp.mean(o * o, axis=-1, keepdims=True)
    y = o * lax.rsqrt(ms + RMS_EPS) * gn_ref[...]
    o_ref[...] = (y * _silu(z_ref[...])).astype(o_ref.dtype)


def gdn_mixer(proj, conv_w, a_log, dt_bias, gnorm, *, batch, seq, rows):
    T = batch * seq
    nblk = seq // rows
    cb = lambda base: (lambda b, h, i: (b * nblk + i, base // LANES + h))
    pad = lambda v: jnp.zeros((1, LANES), F32).at[0, :GDN_HEADS].set(v)
    wspec = lambda base: pl.BlockSpec((GDN_CONV, LANES), lambda b, h, i: (0, base // LANES + h))
    vec = pl.BlockSpec((1, LANES), lambda b, h, i: (0, 0))
    return pl.pallas_call(
        functools.partial(_gdn_body, rows=rows),
        grid=(batch, GDN_HEADS, nblk),
        in_specs=[pl.BlockSpec((rows, LANES), cb(COL_AQ)),
                  pl.BlockSpec((rows, LANES), cb(COL_AK)),
                  pl.BlockSpec((rows, LANES), cb(COL_AV)),
                  pl.BlockSpec((rows, LANES), cb(COL_AZ)),
                  pl.BlockSpec((rows, LANES), lambda b, h, i: (b * nblk + i, COL_SMALL // LANES)),
                  wspec(0), wspec(512), wspec(1024), vec, vec, vec],
        out_specs=pl.BlockSpec((rows, LANES), lambda b, h, i: (b * nblk + i, h)),
        out_shape=jax.ShapeDtypeStruct((T, GDN_HEADS * GDN_D), BF16),
        scratch_shapes=[pltpu.VMEM((GDN_D, GDN_D), F32),
                        pltpu.VMEM((3, 8, LANES), F32),
                        pltpu.VMEM((rows, LANES), F32), pltpu.VMEM((rows, LANES), F32),
                        pltpu.VMEM((rows, LANES), F32), pltpu.VMEM((rows, LANES), F32),
                        pltpu.VMEM((rows, LANES), F32), pltpu.VMEM((rows, LANES), F32)],
        compiler_params=_cparams("parallel", "parallel", "arbitrary"),
        name="gdn_mixer",
    )(proj, proj, proj, proj, proj, conv_w, conv_w, conv_w, pad(a_log), pad(dt_bias),
      gnorm.reshape(1, GDN_D))


def _qk_rope_body(q_ref, k_ref, cos_ref, sin_ref, gq_ref, gk_ref, gm_ref, qo_ref, ko_ref):
    lane = _iota((1, 2 * LANES), 1)
    first_half = (lane % SWA_DH) < (SWA_DH // 2)
    cosf = cos_ref[...]
    sins = sin_ref[...]

    def norm_rope(x, gain):
        ms = _dot(x * x, gm_ref[...], HIGHEST)
        xn = x * lax.rsqrt(ms + RMS_EPS) * gain
        other = jnp.where(first_half, pltpu.roll(xn, 2 * LANES - SWA_DH // 2, 1),
                          pltpu.roll(xn, SWA_DH // 2, 1))
        return xn * cosf + other * sins

    qo_ref[...] = (norm_rope(q_ref[...], gq_ref[...]) * (SWA_DH ** -0.5)).astype(qo_ref.dtype)
    ko_ref[...] = norm_rope(k_ref[...], gk_ref[...]).astype(ko_ref.dtype)


def qk_norm_rope(proj, cosf, sins, q_gain, k_gain, *, tm):
    T = proj.shape[0]
    W = SWA_HEADS * SWA_DH
    grp = jnp.arange(W) // SWA_DH
    gmean = (grp[:, None] == grp[None, :]).astype(F32) / SWA_DH
    tile = lambda v: jnp.tile(v, SWA_HEADS).reshape(1, W)
    vec = pl.BlockSpec((1, W), lambda g, i: (0, 0))
    out = jax.ShapeDtypeStruct((len(SWA_GROUPS), T, W), BF16)
    return pl.pallas_call(
        _qk_rope_body,
        grid=(len(SWA_GROUPS), T // tm),
        in_specs=[pl.BlockSpec((tm, W), lambda g, i: (i, COL_BQ // W + g)),
                  pl.BlockSpec((tm, W), lambda g, i: (i, COL_BK // W + g)),
                  pl.BlockSpec((tm, W), lambda g, i: (i, 0)),
                  pl.BlockSpec((tm, W), lambda g, i: (i, 0)),
                  vec, vec, pl.BlockSpec((W, W), lambda g, i: (0, 0))],
        out_specs=[pl.BlockSpec((None, tm, W), lambda g, i: (g, i, 0)),
                   pl.BlockSpec((None, tm, W), lambda g, i: (g, i, 0))],
        out_shape=[out, out],
        compiler_params=_cparams("parallel", "parallel"),
        name="qk_norm_rope",
    )(proj, proj, cosf, sins, tile(q_gain), tile(k_gain), gmean)


def _band_attn_body(q_ref, kp_ref, kc_ref, vp_ref, vc_ref, o_ref, lse_ref, *, blocks_per_residue):
    i = pl.program_id(1)
    first = (i % blocks_per_residue) == 0
    c = SWA_BACK
    W = SWA_HEADS * SWA_DH
    q = q_ref[...]
    kcat = jnp.concatenate([kp_ref[...], kc_ref[...]], axis=0)
    vcat = jnp.concatenate([vp_ref[...], vc_ref[...]], axis=0)
    a = _iota((c, 2 * c), 0)
    b = _iota((c, 2 * c), 1)
    dist = a + c - b
    valid = (dist >= 0) & (dist <= SWA_BACK) & ((b >= c) | jnp.logical_not(first))
    lane = _iota((1, W), 1)
    o_acc = jnp.zeros((c, W), F32)
    lse_acc = jnp.zeros((c, W), F32)
    for hd in range(SWA_HEADS):
        hmask = (lane // SWA_DH) == hd
        qh = jnp.where(hmask, q, jnp.zeros_like(q))
        s = jnp.where(valid, _dot_nt(qh, kcat), -jnp.inf)
        m = jnp.max(s, axis=-1, keepdims=True)
        p = jnp.exp(s - m)
        l = jnp.sum(p, axis=-1, keepdims=True)
        pv = _dot(p.astype(BF16), vcat) / l
        o_acc = jnp.where(hmask, pv, o_acc)
        lse_acc = jnp.where(hmask, m + jnp.log(l), lse_acc)
    o_ref[...] = o_acc
    lse_ref[...] = lse_acc


def band_attention(q, k, v, *, dilation):
    B, S, W = q.shape
    c = SWA_BACK
    nblk = S // c
    cur = pl.BlockSpec((None, c, W), lambda b, i: (b, i, 0))
    prev = pl.BlockSpec((None, c, W), lambda b, i: (b, jnp.maximum(i - 1, 0), 0))
    out = jax.ShapeDtypeStruct((B, S, W), F32)
    return pl.pallas_call(
        functools.partial(_band_attn_body, blocks_per_residue=nblk // dilation),
        grid=(B, nblk),
        in_specs=[cur, prev, cur, prev, cur],
        out_specs=[cur, cur],
        out_shape=[out, out],
        compiler_params=_cparams("parallel", "parallel"),
        name=f"band_attention_d{dilation}",
    )(q, k, k, v, v)


def _gla_body(q_ref, k_ref, v_ref, r_ref, sm_ref, wg_ref, bg_ref, gn_ref, o_ref,
              state_ref, qs_ref, ks_ref, la_ref, os_ref, *, rows):
    blk = pl.program_id(1)
    nchunk = rows // CHUNK
    C = CHUNK

    @pl.when(blk == 0)
    def _():
        state_ref[...] = jnp.zeros_like(state_ref)

    x = _dot(sm_ref[...].astype(BF16), wg_ref[...]) + bg_ref[...]
    la_ref[...] = (jnp.minimum(x, 0.0) - jnp.log(1.0 + jnp.exp(-jnp.abs(x)))) * (1.0 / GLA_TAU)
    qs_ref[...] = q_ref[...] * (GLA_DK ** -0.5)
    ks_ref[...] = k_ref[...]

    ri = _iota((C, C), 0)
    ci = _iota((C, C), 1)
    tril = (ri >= ci).astype(F32)
    lane = _iota((1, LANES), 1)
    lrow = _iota((LANES, C), 0)
    gsum = [((lrow // GLA_DK) == h2).astype(BF16) for h2 in range(2)]
    hmask = [(lane // GLA_DK) == h2 for h2 in range(2)]
    nsub = C // GLA_SUB

    def chunk_step(c, carry):
        r0 = pl.multiple_of(c * C, C)
        for pair in range(GLA_HEADS // 2):
            cols = pl.ds(pair * LANES, LANES)
            qc = qs_ref[pl.ds(r0, C), cols]
            kc = ks_ref[pl.ds(r0, C), cols]
            la = la_ref[pl.ds(r0, C), cols]
            bcum = _dot(tril, la, HIGHEST)
            b_last = bcum[C - 1:C, :]
            s_rows = [[jnp.zeros((GLA_SUB, C), F32)] for _ in range(2)]
            for sb in range(1, nsub):
                lo = sb * GLA_SUB
                bref = bcum[lo - 1:lo, :]
                q_sb = (qc[lo:lo + GLA_SUB] * jnp.exp(bcum[lo:lo + GLA_SUB] - bref))
                k_sb = (kc * jnp.exp(jnp.minimum(bref - bcum, 0.0))).astype(BF16)
                for h2 in range(2):
                    qm = jnp.where(hmask[h2], q_sb, 0.0).astype(BF16)
                    s_rows[h2].append(_dot_nt(qm, k_sb))
            scores = []
            for h2 in range(2):
                s_off = jnp.concatenate(s_rows[h2], axis=0)
                scores.append(jnp.where((ri // GLA_SUB) > (ci // GLA_SUB), s_off, 0.0))
            for off in range(GLA_SUB):
                if off == 0:
                    prod = qc * kc
                else:
                    k_sh = pltpu.roll(kc, off, 0)
                    b_sh = pltpu.roll(bcum, off, 0)
                    prod = qc * k_sh * jnp.exp(jnp.minimum(bcum - b_sh, 0.0))
                p16 = prod.astype(BF16)
                on_diag = ((ri - ci) == off) & ((ri // GLA_SUB) == (ci // GLA_SUB))
                for h2 in range(2):
                    d = _dot(p16, gsum[h2])
                    scores[h2] = jnp.where(on_diag, d, scores[h2])
            q_dec = qc * jnp.exp(bcum)
            k_dec = kc * jnp.exp(b_last - bcum)
            dec_last = jnp.exp(b_last)
            for h2 in range(2):
                hd = pair * 2 + h2
                vc = v_ref[pl.ds(r0, C), pl.ds(hd * GLA_DV, GLA_DV)].astype(BF16)
                st = state_ref[hd]
                qd = jnp.where(hmask[h2], q_dec, 0.0).astype(BF16)
                kd = jnp.where(hmask[h2], k_dec, 0.0).astype(BF16)
                o = _dot(qd, st.astype(BF16)) + _dot(scores[h2].astype(BF16), vc)
                dl_col = jnp.sum(jnp.where(_iota((LANES, LANES), 0) == _iota((LANES, LANES), 1),
                                           jnp.broadcast_to(dec_last, (LANES, LANES)), 0.0),
                                 axis=-1, keepdims=True)
                state_ref[hd] = st * dl_col + _dot_tn(kd, vc)
                os_ref[pl.ds(r0, C), pl.ds(hd * GLA_DV, GLA_DV)] = o
        return carry

    lax.fori_loop(0, nchunk, chunk_step, 0)

    for hd in range(GLA_HEADS):
        cols = pl.ds(hd * GLA_DV, GLA_DV)
        o = os_ref[:, cols]
        ms = jnp.mean(o * o, axis=-1, keepdims=True)
        y = o * lax.rsqrt(ms + RMS_EPS) * gn_ref[...]
        o_ref[:, cols] = (y * _silu(r_ref[:, cols])).astype(o_ref.dtype)


def gla_mixer(proj, gate_up, gate_bias, gnorm, *, batch, seq, rows):
    T = batch * seq
    nblk = seq // rows
    QW = GLA_HEADS * GLA_DK
    VW = GLA_HEADS * GLA_DV
    row = lambda b, i: b * nblk + i
    wg = jnp.zeros((LANES, QW), F32).at[8:8 + GLA_RANK].set(gate_up).astype(BF16)
    return pl.pallas_call(
        functools.partial(_gla_body, rows=rows),
        grid=(batch, nblk),
        in_specs=[pl.BlockSpec((rows, QW), lambda b, i: (row(b, i), COL_CQ // QW)),
                  pl.BlockSpec((rows, QW), lambda b, i: (row(b, i), COL_CK // QW)),
                  pl.BlockSpec((rows, VW), lambda b, i: (row(b, i), COL_CV // VW)),
                  pl.BlockSpec((rows, VW), lambda b, i: (row(b, i), COL_CR // VW)),
                  pl.BlockSpec((rows, LANES), lambda b, i: (row(b, i), COL_SMALL // LANES)),
                  pl.BlockSpec((LANES, QW), lambda b, i: (0, 0)),
                  pl.BlockSpec((1, QW), lambda b, i: (0, 0)),
                  pl.BlockSpec((1, GLA_DV), lambda b, i: (0, 0))],
        out_specs=pl.BlockSpec((rows, VW), lambda b, i: (row(b, i), 0)),
        out_shape=jax.ShapeDtypeStruct((T, VW), BF16),
        scratch_shapes=[pltpu.VMEM((GLA_HEADS, LANES, GLA_DV), F32),
                        pltpu.VMEM((rows, QW), F32), pltpu.VMEM((rows, QW), F32),
                        pltpu.VMEM((rows, QW), F32), pltpu.VMEM((rows, VW), F32)],
        compiler_params=_cparams("parallel", "arbitrary"),
        name="gla_mixer",
    )(proj, proj, proj, proj, proj, wg, gate_bias.reshape(1, QW), gnorm.reshape(1, GLA_DV))


def _mix_out_body(x_ref, g0_ref, g1_ref, g2_ref, gb_ref, ya_ref, yc_ref,
                  o0_ref, o1_ref, o2_ref, l0_ref, l1_ref, l2_ref,
                  wa_ref, wb_ref, wc_ref, wo_ref, out_ref):
    l0, l1, l2 = l0_ref[...], l1_ref[...], l2_ref[...]
    m = jnp.maximum(jnp.maximum(l0, l1), l2)
    e0, e1, e2 = jnp.exp(l0 - m), jnp.exp(l1 - m), jnp.exp(l2 - m)
    ob = (e0 * o0_ref[...] + e1 * o1_ref[...] + e2 * o2_ref[...]) / (e0 + e1 + e2)
    gb = gb_ref[...]
    y = _sigmoid(g0_ref[...] + gb[0:1]) * _dot(ya_ref[...], wa_ref[...])
    y = y + _sigmoid(g1_ref[...] + gb[1:2]) * _dot(ob.astype(BF16), wb_ref[...])
    y = y + _sigmoid(g2_ref[...] + gb[2:3]) * _dot(yc_ref[...], wc_ref[...])
    out_ref[...] = x_ref[...] + _dot(y.astype(BF16), wo_ref[...])


def mix_out(x, proj, gate_bias, ya, yc, o_grp, lse_grp, wa, wb, wc, wo, *, tm):
    T, D = x.shape
    W = SWA_HEADS * SWA_DH
    rowblk = lambda w: pl.BlockSpec((tm, w), lambda i: (i, 0))
    full = lambda a: pl.BlockSpec(a.shape, lambda i: (0, 0))
    gate = lambda n: pl.BlockSpec((tm, D), lambda i: (i, COL_GATES // D + n))
    return pl.pallas_call(
        _mix_out_body,
        grid=(T // tm,),
        in_specs=[rowblk(D), gate(0), gate(1), gate(2), full(gate_bias), rowblk(ya.shape[1]),
                  rowblk(yc.shape[1]), rowblk(W), rowblk(W), rowblk(W), rowblk(W), rowblk(W), rowblk(W),
                  full(wa), full(wb), full(wc), full(wo)],
        out_specs=rowblk(D),
        out_shape=jax.ShapeDtypeStruct((T, D), F32),
        compiler_params=_cparams("parallel"),
        name="mix_out",
    )(x, proj, proj, proj, gate_bias, ya, yc, *o_grp, *lse_grp, wa, wb, wc, wo)


def _cross_attn_body(x_ref, gx_ref, wq_ref, kv_ref, gq_ref, gk_ref, wo_ref, out_ref):
    x = x_ref[...]
    ms = jnp.mean(x * x, axis=-1, keepdims=True)
    h = (x * lax.rsqrt(ms + RMS_EPS) * gx_ref[...]).astype(BF16)
    q = _dot(h, wq_ref[...])
    kv = kv_ref[...]
    KW = XA_HEADS * XA_DH
    outs = []
    for hd in range(XA_HEADS):
        qh = q[:, hd * XA_DH:(hd + 1) * XA_DH]
        kh = kv[:, hd * XA_DH:(hd + 1) * XA_DH]
        vh = kv[:, KW + hd * XA_DH:KW + (hd + 1) * XA_DH]
        qn = qh * lax.rsqrt(jnp.mean(qh * qh, axis=-1, keepdims=True) + RMS_EPS) * gq_ref[...]
        kn = kh * lax.rsqrt(jnp.mean(kh * kh, axis=-1, keepdims=True) + RMS_EPS) * gk_ref[...]
        s = _dot_nt(qn.astype(BF16), kn.astype(BF16)) * (XA_DH ** -0.5)
        m = jnp.max(s, axis=-1, keepdims=True)
        p = jnp.exp(s - m)
        l = jnp.sum(p, axis=-1, keepdims=True)
        outs.append((_dot(p.astype(BF16), vh.astype(BF16)) / l).astype(BF16))
    o = jnp.concatenate(outs, axis=-1)
    out_ref[...] = x + _dot(o, wo_ref[...])


def cross_attention(x, kv, gx, wq, gq, gk, wo, *, batch, seq, mem_len, tm):
    T, D = x.shape
    per_batch = seq // tm
    full = lambda a: pl.BlockSpec(a.shape, lambda i: (0, 0))
    gx, gq, gk = gx.reshape(1, D), gq.reshape(1, XA_DH), gk.reshape(1, XA_DH)
    return pl.pallas_call(
        _cross_attn_body,
        grid=(T // tm,),
        in_specs=[pl.BlockSpec((tm, D), lambda i: (i, 0)), full(gx), full(wq),
                  pl.BlockSpec((mem_len, kv.shape[1]), lambda i: (i // per_batch, 0)),
                  full(gq), full(gk), full(wo)],
        out_specs=pl.BlockSpec((tm, D), lambda i: (i, 0)),
        out_shape=jax.ShapeDtypeStruct((T, D), F32),
        compiler_params=_cparams("parallel"),
        name="cross_attention",
    )(x, gx, wq, kv, gq, gk, wo)


def _router_body(x_ref, gx_ref, wr_ref, br_ref, h_ref, idx_ref, gate_ref):
    x = x_ref[...]
    ms = jnp.mean(x * x, axis=-1, keepdims=True)
    h = x * lax.rsqrt(ms + RMS_EPS) * gx_ref[...]
    h_ref[...] = h
    lane = _iota((1, LANES), 1)
    logits = _dot(h, wr_ref[...], HIGHEST) + br_ref[...]
    logits = jnp.where(lane < N_EXPERTS, logits, -jnp.inf)
    idx_out = jnp.zeros(logits.shape, jnp.int32)
    val_out = jnp.full(logits.shape, -jnp.inf, F32)
    for k in range(TOP_K):
        m = jnp.max(logits, axis=-1, keepdims=True)
        sel = jnp.min(jnp.where(logits == m, lane, LANES), axis=-1, keepdims=True)
        idx_out = jnp.where(lane == k, sel, idx_out)
        val_out = jnp.where(lane == k, m, val_out)
        logits = jnp.where(lane == sel, -jnp.inf, logits)
    top = jnp.max(val_out, axis=-1, keepdims=True)
    e = jnp.exp(val_out - top)
    idx_ref[...] = idx_out
    gate_ref[...] = e / jnp.sum(e, axis=-1, keepdims=True)


def moe_router(x, gx, wr, br, *, tm):
    T, D = x.shape
    wr_p = jnp.zeros((D, LANES), F32).at[:, :N_EXPERTS].set(wr)
    br_p = jnp.zeros((1, LANES), F32).at[0, :N_EXPERTS].set(br)
    full = lambda a: pl.BlockSpec(a.shape, lambda i: (0, 0))
    gx = gx.reshape(1, D)
    return pl.pallas_call(
        _router_body,
        grid=(T // tm,),
        in_specs=[pl.BlockSpec((tm, D), lambda i: (i, 0)), full(gx), full(wr_p), full(br_p)],
        out_specs=[pl.BlockSpec((tm, D), lambda i: (i, 0)),
                   pl.BlockSpec((tm, LANES), lambda i: (i, 0)),
                   pl.BlockSpec((tm, LANES), lambda i: (i, 0))],
        out_shape=[jax.ShapeDtypeStruct((T, D), F32),
                   jax.ShapeDtypeStruct((T, LANES), jnp.int32),
                   jax.ShapeDtypeStruct((T, LANES), F32)],
        compiler_params=_cparams("parallel"),
        name="moe_router",
    )(x, gx, wr_p, br_p)


def _expert_body(be_ref, nused_ref, tok0_ref, tokn_ref, dst_ref, h_ref, wi_ref, bi_ref, wo_ref, bo_ref,
                 out_ref, xbuf, ybuf, gsem, ssem, *, dump_base):
    i = pl.program_id(0)
    nu = nused_ref[0]
    slot = i % 2

    def gather(tok_ref, s, j):
        return pltpu.make_async_copy(h_ref.at[pl.ds(tok_ref[0, 0, j], 1)], xbuf.at[s, pl.ds(j, 1)], gsem.at[s])

    def scatter(s, j):
        d = dst_ref[0, 0, j]
        d = jnp.where(d < 0, dump_base + s * MOE_BM + j, d)
        return pltpu.make_async_copy(ybuf.at[s, pl.ds(j, 1)], out_ref.at[pl.ds(d, 1)], ssem.at[s])

    def for_rows(fn):
        def body(j, carry):
            fn(j)
            return carry
        lax.fori_loop(0, MOE_BM, body, 0)

    @pl.when(i == 0)
    def _():
        for_rows(lambda j: gather(tok0_ref, 0, j).start())
        ybuf[...] = jnp.zeros_like(ybuf)
        for s in range(2):
            fill = pltpu.make_async_copy(ybuf.at[s], out_ref.at[pl.ds(dump_base + s * MOE_BM, MOE_BM)], ssem.at[s])
            fill.start()
            fill.wait()

    @pl.when(i + 1 < nu)
    def _():
        for_rows(lambda j: gather(tokn_ref, 1 - slot, j).start())

    @pl.when(i < nu)
    def _():
        for_rows(lambda j: gather(tok0_ref, slot, j).wait())
        hh = _dot(xbuf[slot].astype(BF16), wi_ref[...]) + bi_ref[...]
        glu = jnp.minimum(hh[:, :D_EXPERT], SWIGLU_LIMIT)
        lin = jnp.clip(hh[:, D_EXPERT:], -SWIGLU_LIMIT, SWIGLU_LIMIT)
        act = glu * _sigmoid(SWIGLU_ALPHA * glu) * (lin + 1.0)
        y = _dot(act.astype(BF16), wo_ref[...]) + bo_ref[...]

        @pl.when(i >= 2)
        def _():
            for_rows(lambda j: scatter(slot, j).wait())

        ybuf[slot] = y
        for_rows(lambda j: scatter(slot, j).start())

        @pl.when(i == nu - 1)
        def _():
            for_rows(lambda j: scatter(slot, j).wait())

            @pl.when(i >= 1)
            def _():
                for_rows(lambda j: scatter(1 - slot, j).wait())


def expert_ffn(h, tok_buf, slot_dst, block_expert, n_used, w_in, b_in, w_out, b_out):
    T, D = h.shape
    P = tok_buf.shape[0]
    nb = P // MOE_BM
    E, _, F2 = w_in.shape
    idx_blk = lambda f: pl.BlockSpec((1, 1, MOE_BM), f, memory_space=pltpu.SMEM)
    grid_spec = pltpu.PrefetchScalarGridSpec(
        num_scalar_prefetch=2,
        grid=(nb,),
        in_specs=[idx_blk(lambda i, be, nu: (i, 0, 0)),
                  idx_blk(lambda i, be, nu: (jnp.minimum(i + 1, nb - 1), 0, 0)),
                  idx_blk(lambda i, be, nu: (i, 0, 0)),
                  pl.BlockSpec(memory_space=pl.ANY),
                  pl.BlockSpec((None, D, F2), lambda i, be, nu: (be[i], 0, 0)),
                  pl.BlockSpec((None, 1, F2), lambda i, be, nu: (be[i], 0, 0)),
                  pl.BlockSpec((None, F2 // 2, D), lambda i, be, nu: (be[i], 0, 0)),
                  pl.BlockSpec((None, 1, D), lambda i, be, nu: (be[i], 0, 0))],
        out_specs=pl.BlockSpec(memory_space=pl.ANY),
        scratch_shapes=[pltpu.VMEM((2, MOE_BM, D), F32), pltpu.VMEM((2, MOE_BM, D), F32),
                        pltpu.SemaphoreType.DMA((2,)), pltpu.SemaphoreType.DMA((2,))],
    )
    tok3 = tok_buf.reshape(nb, 1, MOE_BM)
    return pl.pallas_call(
        functools.partial(_expert_body, dump_base=TOP_K * T),
        grid_spec=grid_spec,
        out_shape=jax.ShapeDtypeStruct((TOP_K * T + 2 * MOE_BM, D), F32),
        compiler_params=_cparams("arbitrary"),
        name="expert_ffn",
    )(block_expert, n_used, tok3, tok3, slot_dst.reshape(nb, 1, MOE_BM), h,
      w_in, b_in.reshape(E, 1, F2), w_out, b_out.reshape(E, 1, D))


def _moe_combine_body(x_ref, y0_ref, y1_ref, y2_ref, y3_ref, gate_ref, out_ref):
    g = gate_ref[...]
    acc = x_ref[...]
    for k, y_ref in enumerate((y0_ref, y1_ref, y2_ref, y3_ref)):
        acc = acc + g[:, k:k + 1] * y_ref[...]
    out_ref[...] = acc


def moe_combine(x, y4, gates, *, tm):
    T, D = x.shape
    nblk = T // tm
    ysp = lambda k: pl.BlockSpec((tm, D), lambda i: (k * nblk + i, 0))
    return pl.pallas_call(
        _moe_combine_body,
        grid=(nblk,),
        in_specs=[pl.BlockSpec((tm, D), lambda i: (i, 0)), ysp(0), ysp(1), ysp(2), ysp(3),
                  pl.BlockSpec((tm, LANES), lambda i: (i, 0))],
        out_specs=pl.BlockSpec((tm, D), lambda i: (i, 0)),
        out_shape=jax.ShapeDtypeStruct((T, D), F32),
        compiler_params=_cparams("parallel"),
        name="moe_combine",
    )(x, y4, y4, y4, y4, gates)


def _pack_w_in(w_in):
    L, D, _ = w_in.shape
    o_alpha = 2048
    o_b = 2056
    o_c = o_b + 2304
    o_low = o_c + 1536
    o_gates = o_low + GLA_RANK
    c = w_in[:, :, o_c:o_low]
    parts = [w_in[:, :, o_gates:],
             w_in[:, :, 0:2048],
             w_in[:, :, o_b:o_c],
             w_in[:, :, o_alpha:o_b], w_in[:, :, o_low:o_gates],
             jnp.zeros((L, D, 256 - 8 - GLA_RANK), w_in.dtype),
             c[:, :, 512:1024], c[:, :, 1024:1536], c[:, :, 0:256], c[:, :, 256:512]]
    packed = jnp.concatenate(parts, axis=-1)
    assert packed.shape[-1] == N_PACKED
    return packed.astype(BF16)


def _to_residue_major(t, dilation):
    if dilation == 1:
        return t
    *lead, S, W = t.shape
    return jnp.swapaxes(t.reshape(*lead, S // dilation, dilation, W), -2, -3).reshape(*lead, S, W)


def _from_residue_major(t, dilation):
    if dilation == 1:
        return t
    *lead, S, W = t.shape
    return jnp.swapaxes(t.reshape(*lead, dilation, S // dilation, W), -2, -3).reshape(*lead, S, W)


def _routing_tables(idx, n_tokens):
    A = n_tokens * TOP_K
    P = A + N_EXPERTS * MOE_BM
    e_flat = idx.reshape(A)
    onehot = (e_flat[:, None] == jnp.arange(N_EXPERTS, dtype=jnp.int32)[None, :]).astype(jnp.int32)
    csum = jnp.cumsum(onehot, axis=0)
    rank = jnp.sum(onehot * csum, axis=1) - 1
    counts = csum[-1]
    padded = (counts + MOE_BM - 1) // MOE_BM * MOE_BM
    pad_ends = jnp.cumsum(padded)
    pad_starts = pad_ends - padded
    dest = (pad_starts[e_flat] + rank).astype(jnp.int32)
    a = jnp.arange(A, dtype=jnp.int32)
    tok_buf = jnp.zeros((P,), jnp.int32).at[dest].set(a // TOP_K)
    slot_dst = jnp.full((P,), -1, jnp.int32).at[dest].set((a % TOP_K) * n_tokens + a // TOP_K)
    block_expert = jnp.minimum(
        jnp.searchsorted(pad_ends, jnp.arange(P // MOE_BM, dtype=jnp.int32) * MOE_BM, side="right"),
        N_EXPERTS - 1).astype(jnp.int32)
    n_used = (pad_ends[-1:] // MOE_BM).astype(jnp.int32)
    return tok_buf, slot_dst, block_expert, n_used


def kernel(x, mem, positions, norm_mix, w_in, gate_bias, gdn_conv, gdn_a_log, gdn_dt_bias, gdn_norm, swa_q_norm, swa_k_norm, gla_gate_up, gla_gate_bias, gla_norm, w_branch_a, w_branch_b, w_branch_c, w_mix_out, norm_cross, norm_mem, xa_wq, xa_wkv, xa_q_norm, xa_k_norm, xa_wo, norm_ffn, router_w, router_b, moe_w_in, moe_b_in, moe_w_out, moe_b_out):
    B, S, D = x.shape
    T = B * S
    M = mem.shape[1]
    depth = w_in.shape[0]
    W = SWA_HEADS * SWA_DH

    inv_freq = ROPE_THETA ** (-jnp.arange(0, SWA_DH, 2, dtype=F32) / SWA_DH)
    ang = positions.astype(F32).reshape(T, 1) * inv_freq[None, :]
    cos, sin = jnp.cos(ang), jnp.sin(ang)
    cosf = jnp.tile(jnp.concatenate([cos, cos], axis=-1), (1, SWA_HEADS))
    sins = jnp.tile(jnp.concatenate([-sin, sin], axis=-1), (1, SWA_HEADS))

    w_in_p = _pack_w_in(w_in)
    xf = x.reshape(T, D)
    memf = mem.reshape(B * M, D)

    for l in range(depth):
        proj = norm_matmul(xf, norm_mix[l], w_in_p[l], tm=1024, tn=1536, name="in_proj")
        ya = gdn_mixer(proj, gdn_conv[l], gdn_a_log[l], gdn_dt_bias[l], gdn_norm[l], batch=B, seq=S, rows=512)
        qn, kn = qk_norm_rope(proj, cosf, sins, swa_q_norm[l], swa_k_norm[l], tm=1024)
        o_grp, lse_grp = [], []
        for gi, (window, dil) in enumerate(SWA_GROUPS):
            assert window // dil == SWA_BACK
            vg = proj[:, COL_BV + gi * W:COL_BV + (gi + 1) * W].astype(BF16)
            rm = lambda t: _to_residue_major(t.reshape(B, S, W), dil)
            o_g, lse_g = band_attention(rm(qn[gi]), rm(kn[gi]), rm(vg), dilation=dil)
            o_grp.append(_from_residue_major(o_g, dil).reshape(T, W))
            lse_grp.append(_from_residue_major(lse_g, dil).reshape(T, W))
        yc = gla_mixer(proj, gla_gate_up[l], gla_gate_bias[l], gla_norm[l], batch=B, seq=S, rows=512)
        xf = mix_out(xf, proj, gate_bias[l], ya, yc, o_grp, lse_grp,
                     w_branch_a[l].astype(BF16), w_branch_b[l].astype(BF16), w_branch_c[l].astype(BF16),
                     w_mix_out[l].astype(BF16), tm=512)

        kv = norm_matmul(memf, norm_mem[l], xa_wkv[l].astype(BF16), tm=min(1024, B * M), tn=1024, name="mem_kv")
        xf = cross_attention(xf, kv, norm_cross[l], xa_wq[l].astype(BF16), xa_q_norm[l], xa_k_norm[l],
                             xa_wo[l].astype(BF16), batch=B, seq=S, mem_len=M, tm=512)

        h, idx, gates = moe_router(xf, norm_ffn[l], router_w[l], router_b[l], tm=1024)
        tok_buf, slot_dst, block_expert, n_used = _routing_tables(idx[:, :TOP_K], T)
        y4 = expert_ffn(h, tok_buf, slot_dst, block_expert, n_used, moe_w_in[l].astype(BF16), moe_b_in[l],
                        moe_w_out[l].astype(BF16), moe_b_out[l])
        xf = moe_combine(xf, y4, gates, tm=512)

    return xf.reshape(B, S, D)
```

```python
import functools

import jax
import jax.numpy as jnp
from jax import lax
from jax.experimental import pallas as pl
from jax.experimental.pallas import tpu as pltpu

F32 = jnp.float32
BF16 = jnp.bfloat16
HIGHEST = lax.Precision.HIGHEST

RMS_EPS = 1e-6
L2_EPS = 1e-6
LANES = 128
VMEM_LIMIT = 56 * 1024 * 1024

D_MODEL = 1024
GDN_HEADS, GDN_D, GDN_CONV, CHUNK = 4, 128, 4, 64
SWA_GROUPS = ((128, 1), (512, 4), (2048, 16))
SWA_HEADS, SWA_DH, SWA_BACK = 4, 64, 128
ROPE_THETA = 10000.0
GLA_HEADS, GLA_DK, GLA_DV, GLA_RANK, GLA_TAU = 4, 64, 128, 16, 16.0
GLA_SUB = 8
XA_HEADS, XA_DH = 4, 128
N_EXPERTS, TOP_K, D_EXPERT = 32, 4, 1024
SWIGLU_ALPHA, SWIGLU_LIMIT = 1.702, 7.0
MOE_BM = 256

COL_GATES = 0
COL_AQ, COL_AK, COL_AV, COL_AZ = 3072, 3584, 4096, 4608
COL_BQ, COL_BK, COL_BV = 5120, 5888, 6656
COL_SMALL = 7424
COL_CV, COL_CR, COL_CQ, COL_CK = 7680, 8192, 8704, 8960
N_PACKED = 9216


def _cparams(*sem):
    return pltpu.CompilerParams(dimension_semantics=sem, vmem_limit_bytes=VMEM_LIMIT)


def _sigmoid(x):
    return 1.0 / (1.0 + jnp.exp(-x))


def _silu(x):
    return x * _sigmoid(x)


def _softplus(x):
    return jnp.maximum(x, 0.0) + jnp.log(1.0 + jnp.exp(-jnp.abs(x)))


def _dot(a, b, precision=None):
    return jnp.dot(a, b, preferred_element_type=F32, precision=precision)


def _dot_nt(a, b, precision=None):
    return lax.dot_general(a, b, (((1,), (1,)), ((), ())), preferred_element_type=F32, precision=precision)


def _dot_tn(a, b, precision=None):
    return lax.dot_general(a, b, (((0,), (0,)), ((), ())), preferred_element_type=F32, precision=precision)


def _dot3(a, b):
    ah = a.astype(BF16)
    bh = b.astype(BF16)
    al = (a - ah.astype(F32)).astype(BF16)
    bl = (b - bh.astype(F32)).astype(BF16)
    return _dot(ah, bh) + (_dot(ah, bl) + _dot(al, bh))


def _iota(shape, axis):
    return lax.broadcasted_iota(jnp.int32, shape, axis)


def _norm_matmul_body(x_ref, g_ref, w_ref, o_ref, h_ref):
    @pl.when(pl.program_id(1) == 0)
    def _():
        x = x_ref[...]
        ms = jnp.mean(x * x, axis=-1, keepdims=True)
        h_ref[...] = (x * lax.rsqrt(ms + RMS_EPS) * g_ref[...]).astype(h_ref.dtype)

    o_ref[...] = _dot(h_ref[...], w_ref[...]).astype(o_ref.dtype)


def norm_matmul(x, gain, w, *, tm, tn, name):
    T, D = x.shape
    N = w.shape[1]
    return pl.pallas_call(
        _norm_matmul_body,
        grid=(T // tm, N // tn),
        in_specs=[pl.BlockSpec((tm, D), lambda i, j: (i, 0)),
                  pl.BlockSpec((1, D), lambda i, j: (0, 0)),
                  pl.BlockSpec((D, tn), lambda i, j: (0, j))],
        out_specs=pl.BlockSpec((tm, tn), lambda i, j: (i, j)),
        out_shape=jax.ShapeDtypeStruct((T, N), F32),
        scratch_shapes=[pltpu.VMEM((tm, D), BF16)],
        compiler_params=_cparams("parallel", "arbitrary"),
        name=name,
    )(x, gain.reshape(1, D), w)


def _gdn_body(q_ref, k_ref, v_ref, z_ref, sm_ref, smt_ref, cq_ref, ck_ref, cv_ref, alog_ref, dtb_ref, gn_ref,
              o_ref, state_ref, tail_ref, u_ref, wq_ref, kd_ref, in_ref, dl_ref, os_ref, *, rows):
    h = pl.program_id(1)
    blk = pl.program_id(2)
    nchunk = rows // CHUNK

    @pl.when(blk == 0)
    def _():
        state_ref[...] = jnp.zeros_like(state_ref)
        tail_ref[...] = jnp.zeros_like(tail_ref)

    def conv_silu(x_ref, w_ref, slot):
        x = x_ref[...]
        xp = jnp.concatenate([tail_ref[slot], x], axis=0)
        w = w_ref[...]
        acc = x * w[GDN_CONV - 1:GDN_CONV, :]
        for s in range(1, GDN_CONV):
            acc = acc + pltpu.roll(xp, s, 0)[8:] * w[GDN_CONV - 1 - s:GDN_CONV - s, :]
        tail_ref[slot] = x[rows - 8:rows]
        return _silu(acc)

    q = conv_silu(q_ref, cq_ref, 0)
    k = conv_silu(k_ref, ck_ref, 1)
    v = conv_silu(v_ref, cv_ref, 2)
    q = q * lax.rsqrt(jnp.sum(q * q, axis=-1, keepdims=True) + L2_EPS) * (GDN_D ** -0.5)
    k = k * lax.rsqrt(jnp.sum(k * k, axis=-1, keepdims=True) + L2_EPS)

    lane = _iota((1, LANES), 1)
    sm = sm_ref[...]
    pick = lambda ref_val, idx: jnp.sum(jnp.where(lane == idx, ref_val, 0.0), axis=-1, keepdims=True)
    neg_a = -jnp.exp(pick(alog_ref[...], h))
    dt_b = pick(dtb_ref[...], h)
    g_col = jnp.broadcast_to(neg_a * _softplus(pick(sm, h) + dt_b), (rows, LANES))
    beta = jnp.broadcast_to(_sigmoid(pick(sm, h + GDN_HEADS)), (rows, LANES))
    g_row = neg_a * _softplus(smt_ref[pl.ds(h, 1), :] + dt_b)

    pos_c = _iota((rows, 1), 0) % CHUNK
    pos_r = _iota((1, rows), 1) % CHUNK
    step = 1
    while step < CHUNK:
        g_col = g_col + jnp.where(pos_c >= step, pltpu.roll(g_col, step, 0), 0.0)
        g_row = g_row + jnp.where(pos_r >= step, pltpu.roll(g_row, step, 1), 0.0)
        step *= 2
    eg = jnp.exp(g_col)

    ri = _iota((CHUNK, CHUNK), 0)
    ci = _iota((CHUNK, CHUNK), 1)
    incl = ri >= ci
    strict = ri > ci
    eye = (ri == ci).astype(F32)
    k_beta = k * beta
    v_beta = v * beta
    kbe = k_beta * eg
    q_dec = q * eg
    k16 = k.astype(BF16)

    for c in range(nchunk):
        rs = slice(c * CHUNK, (c + 1) * CHUNK)
        gc = g_col[rs]
        decay = jnp.where(incl, jnp.exp(jnp.minimum(gc[:, :CHUNK] - g_row[:, rs], 0.0)), 0.0)
        kq = jnp.concatenate([k_beta[rs], q[rs]], axis=0).astype(BF16)
        s = _dot_nt(kq, k16[rs])
        lower = jnp.where(strict, s[:CHUNK] * decay, 0.0)
        intra = jnp.where(incl, s[CHUNK:] * decay, 0.0)
        pw = -lower
        inv = eye + pw
        pw = _dot3(pw, pw)
        for _ in range(4):
            both = _dot3(jnp.concatenate([inv, pw], axis=0), pw)
            inv = inv + both[:CHUNK]
            pw = both[CHUNK:]
        inv = inv + _dot3(inv, pw)
        uw = _dot3(inv, jnp.concatenate([v_beta[rs], kbe[rs]], axis=1))
        g_last = gc[CHUNK - 1:CHUNK, :]
        u_ref[c] = uw[:, :GDN_D]
        wq_ref[c] = jnp.concatenate([uw[:, GDN_D:], q_dec[rs]], axis=0).astype(BF16)
        kd_ref[c] = (k[rs] * jnp.exp(g_last - gc)).astype(BF16)
        in_ref[c] = intra.astype(BF16)
        dl_ref[c] = jnp.exp(g_last)

    def chunk_step(c, carry):
        st = state_ref[...]
        wq_s = _dot(wq_ref[c], st.astype(BF16))
        v16 = (u_ref[c] - wq_s[:CHUNK]).astype(BF16)
        o = wq_s[CHUNK:] + _dot(in_ref[c], v16)
        state_ref[...] = st * dl_ref[c][:, :1] + _dot_tn(kd_ref[c], v16)
        os_ref[pl.ds(pl.multiple_of(c * CHUNK, CHUNK), CHUNK), :] = o
        return carry

    lax.fori_loop(0, nchunk, chunk_step, 0, unroll=True)

    o = os_ref[...]
    ms = jnp.mean(o * o, axis=-1, keepdims=True)
    y = o * lax.rsqrt(ms + RMS_EPS) * gn_ref[...]
    o_ref[...] = (y * _silu(z_ref[...])).astype(o_ref.dtype)


def gdn_mixer(proj, conv_w, a_log, dt_bias, gnorm, *, batch, seq, rows):
    T = batch * seq
    nblk = seq // rows
    nchunk = rows // CHUNK
    small_t = proj[:, COL_SMALL:COL_SMALL + 2 * GDN_HEADS].T
    cb = lambda base: (lambda b, h, i: (b * nblk + i, base // LANES + h))
    pad = lambda v: jnp.zeros((1, LANES), F32).at[0, :GDN_HEADS].set(v)
    wspec = lambda base: pl.BlockSpec((GDN_CONV, LANES), lambda b, h, i: (0, base // LANES + h))
    vec = pl.BlockSpec((1, LANES), lambda b, h, i: (0, 0))
    return pl.pallas_call(
        functools.partial(_gdn_body, rows=rows),
        grid=(batch, GDN_HEADS, nblk),
        in_specs=[pl.BlockSpec((rows, LANES), cb(COL_AQ)),
                  pl.BlockSpec((rows, LANES), cb(COL_AK)),
                  pl.BlockSpec((rows, LANES), cb(COL_AV)),
                  pl.BlockSpec((rows, LANES), cb(COL_AZ)),
                  pl.BlockSpec((rows, LANES), lambda b, h, i: (b * nblk + i, COL_SMALL // LANES)),
                  pl.BlockSpec((2 * GDN_HEADS, rows), lambda b, h, i: (0, b * nblk + i)),
                  wspec(0), wspec(512), wspec(1024), vec, vec, vec],
        out_specs=pl.BlockSpec((rows, LANES), lambda b, h, i: (b * nblk + i, h)),
        out_shape=jax.ShapeDtypeStruct((T, GDN_HEADS * GDN_D), BF16),
        scratch_shapes=[pltpu.VMEM((GDN_D, GDN_D), F32),
                        pltpu.VMEM((3, 8, LANES), F32),
                        pltpu.VMEM((nchunk, CHUNK, GDN_D), F32),
                        pltpu.VMEM((nchunk, 2 * CHUNK, GDN_D), BF16),
                        pltpu.VMEM((nchunk, CHUNK, GDN_D), BF16),
                        pltpu.VMEM((nchunk, CHUNK, CHUNK), BF16),
                        pltpu.VMEM((nchunk, 1, LANES), F32),
                        pltpu.VMEM((rows, LANES), F32)],
        compiler_params=_cparams("parallel", "parallel", "arbitrary"),
        name="gdn_mixer",
    )(proj, proj, proj, proj, proj, small_t, conv_w, conv_w, conv_w, pad(a_log), pad(dt_bias),
      gnorm.reshape(1, GDN_D))


def _qk_rope_body(q_ref, k_ref, cos_ref, sin_ref, gq_ref, gk_ref, gm_ref, qo_ref, ko_ref):
    lane = _iota((1, 2 * LANES), 1)
    first_half = (lane % SWA_DH) < (SWA_DH // 2)
    cosf = cos_ref[...]
    sins = sin_ref[...]

    def norm_rope(x, gain):
        ms = _dot(x * x, gm_ref[...], HIGHEST)
        xn = x * lax.rsqrt(ms + RMS_EPS) * gain
        other = jnp.where(first_half, pltpu.roll(xn, 2 * LANES - SWA_DH // 2, 1),
                          pltpu.roll(xn, SWA_DH // 2, 1))
        return xn * cosf + other * sins

    qo_ref[...] = (norm_rope(q_ref[...], gq_ref[...]) * (SWA_DH ** -0.5)).astype(qo_ref.dtype)
    ko_ref[...] = norm_rope(k_ref[...], gk_ref[...]).astype(ko_ref.dtype)


def qk_norm_rope(proj, cosf, sins, q_gain, k_gain, *, tm):
    T = proj.shape[0]
    W = SWA_HEADS * SWA_DH
    grp = jnp.arange(W) // SWA_DH
    gmean = (grp[:, None] == grp[None, :]).astype(F32) / SWA_DH
    tile = lambda v: jnp.tile(v, SWA_HEADS).reshape(1, W)
    vec = pl.BlockSpec((1, W), lambda g, i: (0, 0))
    out = jax.ShapeDtypeStruct((len(SWA_GROUPS), T, W), BF16)
    return pl.pallas_call(
        _qk_rope_body,
        grid=(len(SWA_GROUPS), T // tm),
        in_specs=[pl.BlockSpec((tm, W), lambda g, i: (i, COL_BQ // W + g)),
                  pl.BlockSpec((tm, W), lambda g, i: (i, COL_BK // W + g)),
                  pl.BlockSpec((tm, W), lambda g, i: (i, 0)),
                  pl.BlockSpec((tm, W), lambda g, i: (i, 0)),
                  vec, vec, pl.BlockSpec((W, W), lambda g, i: (0, 0))],
        out_specs=[pl.BlockSpec((None, tm, W), lambda g, i: (g, i, 0)),
                   pl.BlockSpec((None, tm, W), lambda g, i: (g, i, 0))],
        out_shape=[out, out],
        compiler_params=_cparams("parallel", "parallel"),
        name="qk_norm_rope",
    )(proj, proj, cosf, sins, tile(q_gain), tile(k_gain), gmean)


def _band_attn_body(q_ref, kp_ref, kc_ref, vp_ref, vc_ref, o_ref, lse_ref, *, blocks_per_residue):
    i = pl.program_id(1)
    first = (i % blocks_per_residue) == 0
    c = SWA_BACK
    W = SWA_HEADS * SWA_DH
    q = q_ref[...]
    kcat = jnp.concatenate([kp_ref[...], kc_ref[...]], axis=0)
    vcat = jnp.concatenate([vp_ref[...], vc_ref[...]], axis=0)
    a = _iota((c, 2 * c), 0)
    b = _iota((c, 2 * c), 1)
    dist = a + c - b
    valid = (dist >= 0) & (dist <= SWA_BACK) & ((b >= c) | jnp.logical_not(first))
    lane = _iota((1, W), 1)
    o_acc = jnp.zeros((c, W), F32)
    lse_acc = jnp.zeros((c, W), F32)
    for hd in range(SWA_HEADS):
        hmask = (lane // SWA_DH) == hd
        qh = jnp.where(hmask, q, jnp.zeros_like(q))
        s = jnp.where(valid, _dot_nt(qh, kcat), -jnp.inf)
        m = jnp.max(s, axis=-1, keepdims=True)
        p = jnp.exp(s - m)
        l = jnp.sum(p, axis=-1, keepdims=True)
        pv = _dot(p.astype(BF16), vcat) / l
        o_acc = jnp.where(hmask, pv, o_acc)
        lse_acc = jnp.where(hmask, m + jnp.log(l), lse_acc)
    o_ref[...] = o_acc
    lse_ref[...] = lse_acc


def band_attention(q, k, v, *, dilation):
    B, S, W = q.shape
    c = SWA_BACK
    nblk = S // c
    cur = pl.BlockSpec((None, c, W), lambda b, i: (b, i, 0))
    prev = pl.BlockSpec((None, c, W), lambda b, i: (b, jnp.maximum(i - 1, 0), 0))
    out = jax.ShapeDtypeStruct((B, S, W), F32)
    return pl.pallas_call(
        functools.partial(_band_attn_body, blocks_per_residue=nblk // dilation),
        grid=(B, nblk),
        in_specs=[cur, prev, cur, prev, cur],
        out_specs=[cur, cur],
        out_shape=[out, out],
        compiler_params=_cparams("parallel", "parallel"),
        name=f"band_attention_d{dilation}",
    )(q, k, k, v, v)


def _gla_body(q_ref, k_ref, v_ref, r_ref, sm_ref, wg_ref, bg_ref, gn_ref, o_ref,
              state_ref, qs_ref, ks_ref, la_ref, os_ref, *, rows):
    blk = pl.program_id(1)
    nchunk = rows // CHUNK
    C = CHUNK

    @pl.when(blk == 0)
    def _():
        state_ref[...] = jnp.zeros_like(state_ref)

    x = _dot(sm_ref[...].astype(BF16), wg_ref[...]) + bg_ref[...]
    la_ref[...] = (jnp.minimum(x, 0.0) - jnp.log(1.0 + jnp.exp(-jnp.abs(x)))) * (1.0 / GLA_TAU)
    qs_ref[...] = q_ref[...] * (GLA_DK ** -0.5)
    ks_ref[...] = k_ref[...]

    ri = _iota((C, C), 0)
    ci = _iota((C, C), 1)
    tril = (ri >= ci).astype(F32)
    lane = _iota((1, LANES), 1)
    lrow = _iota((LANES, C), 0)
    gsum = [((lrow // GLA_DK) == h2).astype(BF16) for h2 in range(2)]
    hmask = [(lane // GLA_DK) == h2 for h2 in range(2)]
    nsub = C // GLA_SUB

    def chunk_step(c, carry):
        r0 = pl.multiple_of(c * C, C)
        for pair in range(GLA_HEADS // 2):
            cols = pl.ds(pair * LANES, LANES)
            qc = qs_ref[pl.ds(r0, C), cols]
            kc = ks_ref[pl.ds(r0, C), cols]
            la = la_ref[pl.ds(r0, C), cols]
            bcum = _dot(tril, la, HIGHEST)
            b_last = bcum[C - 1:C, :]
            s_rows = [[jnp.zeros((GLA_SUB, C), F32)] for _ in range(2)]
            for sb in range(1, nsub):
                lo = sb * GLA_SUB
                bref = bcum[lo - 1:lo, :]
                q_sb = (qc[lo:lo + GLA_SUB] * jnp.exp(bcum[lo:lo + GLA_SUB] - bref))
                k_sb = (kc * jnp.exp(jnp.minimum(bref - bcum, 0.0))).astype(BF16)
                for h2 in range(2):
                    qm = jnp.where(hmask[h2], q_sb, 0.0).astype(BF16)
                    s_rows[h2].append(_dot_nt(qm, k_sb))
            scores = []
            for h2 in range(2):
                s_off = jnp.concatenate(s_rows[h2], axis=0)
                scores.append(jnp.where((ri // GLA_SUB) > (ci // GLA_SUB), s_off, 0.0))
            for off in range(GLA_SUB):
                if off == 0:
                    prod = qc * kc
                else:
                    k_sh = pltpu.roll(kc, off, 0)
                    b_sh = pltpu.roll(bcum, off, 0)
                    prod = qc * k_sh * jnp.exp(jnp.minimum(bcum - b_sh, 0.0))
                p16 = prod.astype(BF16)
                on_diag = ((ri - ci) == off) & ((ri // GLA_SUB) == (ci // GLA_SUB))
                for h2 in range(2):
                    d = _dot(p16, gsum[h2])
                    scores[h2] = jnp.where(on_diag, d, scores[h2])
            q_dec = qc * jnp.exp(bcum)
            k_dec = kc * jnp.exp(b_last - bcum)
            dec_last = jnp.exp(b_last)
            for h2 in range(2):
                hd = pair * 2 + h2
                vc = v_ref[pl.ds(r0, C), pl.ds(hd * GLA_DV, GLA_DV)].astype(BF16)
                st = state_ref[hd]
                qd = jnp.where(hmask[h2], q_dec, 0.0).astype(BF16)
                kd = jnp.where(hmask[h2], k_dec, 0.0).astype(BF16)
                o = _dot(qd, st.astype(BF16)) + _dot(scores[h2].astype(BF16), vc)
                dl_col = jnp.sum(jnp.where(_iota((LANES, LANES), 0) == _iota((LANES, LANES), 1),
                                           jnp.broadcast_to(dec_last, (LANES, LANES)), 0.0),
                                 axis=-1, keepdims=True)
                state_ref[hd] = st * dl_col + _dot_tn(kd, vc)
                os_ref[pl.ds(r0, C), pl.ds(hd * GLA_DV, GLA_DV)] = o
        return carry

    lax.fori_loop(0, nchunk, chunk_step, 0)

    for hd in range(GLA_HEADS):
        cols = pl.ds(hd * GLA_DV, GLA_DV)
        o = os_ref[:, cols]
        ms = jnp.mean(o * o, axis=-1, keepdims=True)
        y = o * lax.rsqrt(ms + RMS_EPS) * gn_ref[...]
        o_ref[:, cols] = (y * _silu(r_ref[:, cols])).astype(o_ref.dtype)


def gla_mixer(proj, gate_up, gate_bias, gnorm, *, batch, seq, rows):
    T = batch * seq
    nblk = seq // rows
    QW = GLA_HEADS * GLA_DK
    VW = GLA_HEADS * GLA_DV
    row = lambda b, i: b * nblk + i
    wg = jnp.zeros((LANES, QW), F32).at[8:8 + GLA_RANK].set(gate_up).astype(BF16)
    return pl.pallas_call(
        functools.partial(_gla_body, rows=rows),
        grid=(batch, nblk),
        in_specs=[pl.BlockSpec((rows, QW), lambda b, i: (row(b, i), COL_CQ // QW)),
                  pl.BlockSpec((rows, QW), lambda b, i: (row(b, i), COL_CK // QW)),
                  pl.BlockSpec((rows, VW), lambda b, i: (row(b, i), COL_CV // VW)),
                  pl.BlockSpec((rows, VW), lambda b, i: (row(b, i), COL_CR // VW)),
                  pl.BlockSpec((rows, LANES), lambda b, i: (row(b, i), COL_SMALL // LANES)),
                  pl.BlockSpec((LANES, QW), lambda b, i: (0, 0)),
                  pl.BlockSpec((1, QW), lambda b, i: (0, 0)),
                  pl.BlockSpec((1, GLA_DV), lambda b, i: (0, 0))],
        out_specs=pl.BlockSpec((rows, VW), lambda b, i: (row(b, i), 0)),
        out_shape=jax.ShapeDtypeStruct((T, VW), BF16),
        scratch_shapes=[pltpu.VMEM((GLA_HEADS, LANES, GLA_DV), F32),
                        pltpu.VMEM((rows, QW), F32), pltpu.VMEM((rows, QW), F32),
                        pltpu.VMEM((rows, QW), F32), pltpu.VMEM((rows, VW), F32)],
        compiler_params=_cparams("parallel", "arbitrary"),
        name="gla_mixer",
    )(proj, proj, proj, proj, proj, wg, gate_bias.reshape(1, QW), gnorm.reshape(1, GLA_DV))


def _mix_out_body(x_ref, g0_ref, g1_ref, g2_ref, gb_ref, ya_ref, yc_ref,
                  o0_ref, o1_ref, o2_ref, l0_ref, l1_ref, l2_ref,
                  wa_ref, wb_ref, wc_ref, wo_ref, out_ref):
    l0, l1, l2 = l0_ref[...], l1_ref[...], l2_ref[...]
    m = jnp.maximum(jnp.maximum(l0, l1), l2)
    e0, e1, e2 = jnp.exp(l0 - m), jnp.exp(l1 - m), jnp.exp(l2 - m)
    ob = (e0 * o0_ref[...] + e1 * o1_ref[...] + e2 * o2_ref[...]) / (e0 + e1 + e2)
    gb = gb_ref[...]
    y = _sigmoid(g0_ref[...] + gb[0:1]) * _dot(ya_ref[...], wa_ref[...])
    y = y + _sigmoid(g1_ref[...] + gb[1:2]) * _dot(ob.astype(BF16), wb_ref[...])
    y = y + _sigmoid(g2_ref[...] + gb[2:3]) * _dot(yc_ref[...], wc_ref[...])
    out_ref[...] = x_ref[...] + _dot(y.astype(BF16), wo_ref[...])


def mix_out(x, proj, gate_bias, ya, yc, o_grp, lse_grp, wa, wb, wc, wo, *, tm):
    T, D = x.shape
    W = SWA_HEADS * SWA_DH
    rowblk = lambda w: pl.BlockSpec((tm, w), lambda i: (i, 0))
    full = lambda a: pl.BlockSpec(a.shape, lambda i: (0, 0))
    gate = lambda n: pl.BlockSpec((tm, D), lambda i: (i, COL_GATES // D + n))
    return pl.pallas_call(
        _mix_out_body,
        grid=(T // tm,),
        in_specs=[rowblk(D), gate(0), gate(1), gate(2), full(gate_bias), rowblk(ya.shape[1]),
                  rowblk(yc.shape[1]), rowblk(W), rowblk(W), rowblk(W), rowblk(W), rowblk(W), rowblk(W),
                  full(wa), full(wb), full(wc), full(wo)],
        out_specs=rowblk(D),
        out_shape=jax.ShapeDtypeStruct((T, D), F32),
        compiler_params=_cparams("parallel"),
        name="mix_out",
    )(x, proj, proj, proj, gate_bias, ya, yc, *o_grp, *lse_grp, wa, wb, wc, wo)


def _cross_attn_body(x_ref, gx_ref, wq_ref, kv_ref, gq_ref, gk_ref, wo_ref, out_ref):
    x = x_ref[...]
    ms = jnp.mean(x * x, axis=-1, keepdims=True)
    h = (x * lax.rsqrt(ms + RMS_EPS) * gx_ref[...]).astype(BF16)
    q = _dot(h, wq_ref[...])
    kv = kv_ref[...]
    KW = XA_HEADS * XA_DH
    outs = []
    for hd in range(XA_HEADS):
        qh = q[:, hd * XA_DH:(hd + 1) * XA_DH]
        kh = kv[:, hd * XA_DH:(hd + 1) * XA_DH]
        vh = kv[:, KW + hd * XA_DH:KW + (hd + 1) * XA_DH]
        qn = qh * lax.rsqrt(jnp.mean(qh * qh, axis=-1, keepdims=True) + RMS_EPS) * gq_ref[...]
        kn = kh * lax.rsqrt(jnp.mean(kh * kh, axis=-1, keepdims=True) + RMS_EPS) * gk_ref[...]
        s = _dot_nt(qn.astype(BF16), kn.astype(BF16)) * (XA_DH ** -0.5)
        m = jnp.max(s, axis=-1, keepdims=True)
        p = jnp.exp(s - m)
        l = jnp.sum(p, axis=-1, keepdims=True)
        outs.append((_dot(p.astype(BF16), vh.astype(BF16)) / l).astype(BF16))
    o = jnp.concatenate(outs, axis=-1)
    out_ref[...] = x + _dot(o, wo_ref[...])


def cross_attention(x, kv, gx, wq, gq, gk, wo, *, batch, seq, mem_len, tm):
    T, D = x.shape
    per_batch = seq // tm
    full = lambda a: pl.BlockSpec(a.shape, lambda i: (0, 0))
    gx, gq, gk = gx.reshape(1, D), gq.reshape(1, XA_DH), gk.reshape(1, XA_DH)
    return pl.pallas_call(
        _cross_attn_body,
        grid=(T // tm,),
        in_specs=[pl.BlockSpec((tm, D), lambda i: (i, 0)), full(gx), full(wq),
                  pl.BlockSpec((mem_len, kv.shape[1]), lambda i: (i // per_batch, 0)),
                  full(gq), full(gk), full(wo)],
        out_specs=pl.BlockSpec((tm, D), lambda i: (i, 0)),
        out_shape=jax.ShapeDtypeStruct((T, D), F32),
        compiler_params=_cparams("parallel"),
        name="cross_attention",
    )(x, gx, wq, kv, gq, gk, wo)


def _router_body(x_ref, gx_ref, wr_ref, br_ref, h_ref, idx_ref, gate_ref):
    x = x_ref[...]
    ms = jnp.mean(x * x, axis=-1, keepdims=True)
    h = x * lax.rsqrt(ms + RMS_EPS) * gx_ref[...]
    h_ref[...] = h
    lane = _iota((1, LANES), 1)
    logits = _dot(h, wr_ref[...], HIGHEST) + br_ref[...]
    logits = jnp.where(lane < N_EXPERTS, logits, -jnp.inf)
    idx_out = jnp.zeros(logits.shape, jnp.int32)
    val_out = jnp.full(logits.shape, -jnp.inf, F32)
    for k in range(TOP_K):
        m = jnp.max(logits, axis=-1, keepdims=True)
        sel = jnp.min(jnp.where(logits == m, lane, LANES), axis=-1, keepdims=True)
        idx_out = jnp.where(lane == k, sel, idx_out)
        val_out = jnp.where(lane == k, m, val_out)
        logits = jnp.where(lane == sel, -jnp.inf, logits)
    top = jnp.max(val_out, axis=-1, keepdims=True)
    e = jnp.exp(val_out - top)
    idx_ref[...] = idx_out
    gate_ref[...] = e / jnp.sum(e, axis=-1, keepdims=True)


def moe_router(x, gx, wr, br, *, tm):
    T, D = x.shape
    wr_p = jnp.zeros((D, LANES), F32).at[:, :N_EXPERTS].set(wr)
    br_p = jnp.zeros((1, LANES), F32).at[0, :N_EXPERTS].set(br)
    full = lambda a: pl.BlockSpec(a.shape, lambda i: (0, 0))
    gx = gx.reshape(1, D)
    return pl.pallas_call(
        _router_body,
        grid=(T // tm,),
        in_specs=[pl.BlockSpec((tm, D), lambda i: (i, 0)), full(gx), full(wr_p), full(br_p)],
        out_specs=[pl.BlockSpec((tm, D), lambda i: (i, 0)),
                   pl.BlockSpec((tm, LANES), lambda i: (i, 0)),
                   pl.BlockSpec((tm, LANES), lambda i: (i, 0))],
        out_shape=[jax.ShapeDtypeStruct((T, D), F32),
                   jax.ShapeDtypeStruct((T, LANES), jnp.int32),
                   jax.ShapeDtypeStruct((T, LANES), F32)],
        compiler_params=_cparams("parallel"),
        name="moe_router",
    )(x, gx, wr_p, br_p)


def _expert_body(be_ref, nused_ref, tok0_ref, tokn_ref, dst_ref, h_ref, wi_ref, bi_ref, wo_ref, bo_ref,
                 out_ref, xbuf, ybuf, gsem, ssem, *, dump_base):
    i = pl.program_id(0)
    nu = nused_ref[0]
    slot = i % 2

    def gather(tok_ref, s, j):
        return pltpu.make_async_copy(h_ref.at[pl.ds(tok_ref[0, 0, j], 1)], xbuf.at[s, pl.ds(j, 1)], gsem.at[s])

    def scatter(s, j):
        return pltpu.make_async_copy(ybuf.at[s, pl.ds(j, 1)], out_ref.at[pl.ds(dst_ref[0, 0, j], 1)], ssem.at[s])

    def for_rows(fn):
        def body(j, carry):
            fn(j)
            return carry
        lax.fori_loop(0, MOE_BM, body, 0, unroll=8)

    def wait_gathered(s):
        pltpu.make_async_copy(xbuf.at[s], xbuf.at[s], gsem.at[s]).wait()

    def wait_scattered(s):
        pltpu.make_async_copy(ybuf.at[s], ybuf.at[s], ssem.at[s]).wait()

    @pl.when(i == 0)
    def _():
        for_rows(lambda j: gather(tok0_ref, 0, j).start())
        ybuf[...] = jnp.zeros_like(ybuf)
        for s in range(2):
            fill = pltpu.make_async_copy(ybuf.at[s], out_ref.at[pl.ds(dump_base + s * MOE_BM, MOE_BM)], ssem.at[s])
            fill.start()
            fill.wait()

    @pl.when(i + 1 < nu)
    def _():
        for_rows(lambda j: gather(tokn_ref, 1 - slot, j).start())

    @pl.when(i < nu)
    def _():
        wait_gathered(slot)
        hh = _dot(xbuf[slot].astype(BF16), wi_ref[...]) + bi_ref[...]
        glu = jnp.minimum(hh[:, :D_EXPERT], SWIGLU_LIMIT)
        lin = jnp.clip(hh[:, D_EXPERT:], -SWIGLU_LIMIT, SWIGLU_LIMIT)
        act = glu * _sigmoid(SWIGLU_ALPHA * glu) * (lin + 1.0)
        y = _dot(act.astype(BF16), wo_ref[...]) + bo_ref[...]

        @pl.when(i >= 2)
        def _():
            wait_scattered(slot)

        ybuf[slot] = y
        for_rows(lambda j: scatter(slot, j).start())

        @pl.when(i == nu - 1)
        def _():
            wait_scattered(slot)

            @pl.when(i >= 1)
            def _():
                wait_scattered(1 - slot)


def expert_ffn(h, tok_buf, slot_dst, block_expert, n_used, w_in, b_in, w_out, b_out):
    T, D = h.shape
    P = tok_buf.shape[0]
    nb = P // MOE_BM
    E, _, F2 = w_in.shape
    idx_blk = lambda f: pl.BlockSpec((1, 1, MOE_BM), f, memory_space=pltpu.SMEM)
    grid_spec = pltpu.PrefetchScalarGridSpec(
        num_scalar_prefetch=2,
        grid=(nb,),
        in_specs=[idx_blk(lambda i, be, nu: (i, 0, 0)),
                  idx_blk(lambda i, be, nu: (jnp.minimum(i + 1, nb - 1), 0, 0)),
                  idx_blk(lambda i, be, nu: (i, 0, 0)),
                  pl.BlockSpec(memory_space=pl.ANY),
                  pl.BlockSpec((None, D, F2), lambda i, be, nu: (be[i], 0, 0)),
                  pl.BlockSpec((None, 1, F2), lambda i, be, nu: (be[i], 0, 0)),
                  pl.BlockSpec((None, F2 // 2, D), lambda i, be, nu: (be[i], 0, 0)),
                  pl.BlockSpec((None, 1, D), lambda i, be, nu: (be[i], 0, 0))],
        out_specs=pl.BlockSpec(memory_space=pl.ANY),
        scratch_shapes=[pltpu.VMEM((2, MOE_BM, D), F32), pltpu.VMEM((2, MOE_BM, D), F32),
                        pltpu.SemaphoreType.DMA((2,)), pltpu.SemaphoreType.DMA((2,))],
    )
    tok3 = tok_buf.reshape(nb, 1, MOE_BM)
    return pl.pallas_call(
        functools.partial(_expert_body, dump_base=TOP_K * T),
        grid_spec=grid_spec,
        out_shape=jax.ShapeDtypeStruct((TOP_K * T + 2 * MOE_BM, D), F32),
        compiler_params=_cparams("arbitrary"),
        name="expert_ffn",
    )(block_expert, n_used, tok3, tok3, slot_dst.reshape(nb, 1, MOE_BM), h,
      w_in, b_in.reshape(E, 1, F2), w_out, b_out.reshape(E, 1, D))


def _moe_combine_body(x_ref, y0_ref, y1_ref, y2_ref, y3_ref, gate_ref, out_ref):
    g = gate_ref[...]
    acc = x_ref[...]
    for k, y_ref in enumerate((y0_ref, y1_ref, y2_ref, y3_ref)):
        acc = acc + g[:, k:k + 1] * y_ref[...]
    out_ref[...] = acc


def moe_combine(x, y4, gates, *, tm):
    T, D = x.shape
    nblk = T // tm
    ysp = lambda k: pl.BlockSpec((tm, D), lambda i: (k * nblk + i, 0))
    return pl.pallas_call(
        _moe_combine_body,
        grid=(nblk,),
        in_specs=[pl.BlockSpec((tm, D), lambda i: (i, 0)), ysp(0), ysp(1), ysp(2), ysp(3),
                  pl.BlockSpec((tm, LANES), lambda i: (i, 0))],
        out_specs=pl.BlockSpec((tm, D), lambda i: (i, 0)),
        out_shape=jax.ShapeDtypeStruct((T, D), F32),
        compiler_params=_cparams("parallel"),
        name="moe_combine",
    )(x, y4, y4, y4, y4, gates)


def _pack_w_in(w_in):
    L, D, _ = w_in.shape
    o_alpha = 2048
    o_b = 2056
    o_c = o_b + 2304
    o_low = o_c + 1536
    o_gates = o_low + GLA_RANK
    c = w_in[:, :, o_c:o_low]
    parts = [w_in[:, :, o_gates:],
             w_in[:, :, 0:2048],
             w_in[:, :, o_b:o_c],
             w_in[:, :, o_alpha:o_b], w_in[:, :, o_low:o_gates],
             jnp.zeros((L, D, 256 - 8 - GLA_RANK), w_in.dtype),
             c[:, :, 512:1024], c[:, :, 1024:1536], c[:, :, 0:256], c[:, :, 256:512]]
    packed = jnp.concatenate(parts, axis=-1)
    assert packed.shape[-1] == N_PACKED
    return packed.astype(BF16)


def _to_residue_major(t, dilation):
    if dilation == 1:
        return t
    *lead, S, W = t.shape
    return jnp.swapaxes(t.reshape(*lead, S // dilation, dilation, W), -2, -3).reshape(*lead, S, W)


def _from_residue_major(t, dilation):
    if dilation == 1:
        return t
    *lead, S, W = t.shape
    return jnp.swapaxes(t.reshape(*lead, dilation, S // dilation, W), -2, -3).reshape(*lead, S, W)


def _routing_tables(idx, n_tokens):
    A = n_tokens * TOP_K
    P = A + N_EXPERTS * MOE_BM
    e_flat = idx.reshape(A)
    onehot = (e_flat[:, None] == jnp.arange(N_EXPERTS, dtype=jnp.int32)[None, :]).astype(jnp.int32)
    csum = jnp.cumsum(onehot, axis=0)
    rank = jnp.sum(onehot * csum, axis=1) - 1
    counts = csum[-1]
    padded = (counts + MOE_BM - 1) // MOE_BM * MOE_BM
    pad_ends = jnp.cumsum(padded)
    pad_starts = pad_ends - padded
    dest = (pad_starts[e_flat] + rank).astype(jnp.int32)
    a = jnp.arange(A, dtype=jnp.int32)
    tok_buf = jnp.zeros((P,), jnp.int32).at[dest].set(a // TOP_K)
    p = jnp.arange(P, dtype=jnp.int32)
    dump = TOP_K * n_tokens + ((p // MOE_BM) % 2) * MOE_BM + p % MOE_BM
    slot_dst = dump.at[dest].set((a % TOP_K) * n_tokens + a // TOP_K)
    block_expert = jnp.minimum(
        jnp.searchsorted(pad_ends, jnp.arange(P // MOE_BM, dtype=jnp.int32) * MOE_BM, side="right"),
        N_EXPERTS - 1).astype(jnp.int32)
    n_used = (pad_ends[-1:] // MOE_BM).astype(jnp.int32)
    return tok_buf, slot_dst, block_expert, n_used


def kernel(x, mem, positions, norm_mix, w_in, gate_bias, gdn_conv, gdn_a_log, gdn_dt_bias, gdn_norm, swa_q_norm, swa_k_norm, gla_gate_up, gla_gate_bias, gla_norm, w_branch_a, w_branch_b, w_branch_c, w_mix_out, norm_cross, norm_mem, xa_wq, xa_wkv, xa_q_norm, xa_k_norm, xa_wo, norm_ffn, router_w, router_b, moe_w_in, moe_b_in, moe_w_out, moe_b_out):
    B, S, D = x.shape
    T = B * S
    M = mem.shape[1]
    depth = w_in.shape[0]
    W = SWA_HEADS * SWA_DH

    inv_freq = ROPE_THETA ** (-jnp.arange(0, SWA_DH, 2, dtype=F32) / SWA_DH)
    ang = positions.astype(F32).reshape(T, 1) * inv_freq[None, :]
    cos, sin = jnp.cos(ang), jnp.sin(ang)
    cosf = jnp.tile(jnp.concatenate([cos, cos], axis=-1), (1, SWA_HEADS))
    sins = jnp.tile(jnp.concatenate([-sin, sin], axis=-1), (1, SWA_HEADS))

    w_in_p = _pack_w_in(w_in)
    xf = x.reshape(T, D)
    memf = mem.reshape(B * M, D)

    for l in range(depth):
        proj = norm_matmul(xf, norm_mix[l], w_in_p[l], tm=1024, tn=1536, name="in_proj")
        ya = gdn_mixer(proj, gdn_conv[l], gdn_a_log[l], gdn_dt_bias[l], gdn_norm[l], batch=B, seq=S, rows=512)
        qn, kn = qk_norm_rope(proj, cosf, sins, swa_q_norm[l], swa_k_norm[l], tm=1024)
        o_grp, lse_grp = [], []
        for gi, (window, dil) in enumerate(SWA_GROUPS):
            assert window // dil == SWA_BACK
            vg = proj[:, COL_BV + gi * W:COL_BV + (gi + 1) * W].astype(BF16)
            rm = lambda t: _to_residue_major(t.reshape(B, S, W), dil)
            o_g, lse_g = band_attention(rm(qn[gi]), rm(kn[gi]), rm(vg), dilation=dil)
            o_grp.append(_from_residue_major(o_g, dil).reshape(T, W))
            lse_grp.append(_from_residue_major(lse_g, dil).reshape(T, W))
        yc = gla_mixer(proj, gla_gate_up[l], gla_gate_bias[l], gla_norm[l], batch=B, seq=S, rows=512)
        xf = mix_out(xf, proj, gate_bias[l], ya, yc, o_grp, lse_grp,
                     w_branch_a[l].astype(BF16), w_branch_b[l].astype(BF16), w_branch_c[l].astype(BF16),
                     w_mix_out[l].astype(BF16), tm=512)

        kv = norm_matmul(memf, norm_mem[l], xa_wkv[l].astype(BF16), tm=min(1024, B * M), tn=1024, name="mem_kv")
        xf = cross_attention(xf, kv, norm_cross[l], xa_wq[l].astype(BF16), xa_q_norm[l], xa_k_norm[l],
                             xa_wo[l].astype(BF16), batch=B, seq=S, mem_len=M, tm=512)

        h, idx, gates = moe_router(xf, norm_ffn[l], router_w[l], router_b[l], tm=1024)
        tok_buf, slot_dst, block_expert, n_used = _routing_tables(idx[:, :TOP_K], T)
        y4 = expert_ffn(h, tok_buf, slot_dst, block_expert, n_used, moe_w_in[l].astype(BF16), moe_b_in[l],
                        moe_w_out[l].astype(BF16), moe_b_out[l])
        xf = moe_combine(xf, y4, gates, tm=512)

    return xf.reshape(B, S, D)
```

```python
import functools

import jax
import jax.numpy as jnp
from jax import lax
from jax.experimental import pallas as pl
from jax.experimental.pallas import tpu as pltpu

F32 = jnp.float32
BF16 = jnp.bfloat16
HIGHEST = lax.Precision.HIGHEST

RMS_EPS = 1e-6
L2_EPS = 1e-6
LANES = 128
VMEM_LIMIT = 56 * 1024 * 1024

D_MODEL = 1024
GDN_HEADS, GDN_D, GDN_CONV, CHUNK = 4, 128, 4, 64
SWA_GROUPS = ((128, 1), (512, 4), (2048, 16))
SWA_HEADS, SWA_DH, SWA_BACK = 4, 64, 128
ROPE_THETA = 10000.0
GLA_HEADS, GLA_DK, GLA_DV, GLA_RANK, GLA_TAU = 4, 64, 128, 16, 16.0
GLA_SUB = 8
XA_HEADS, XA_DH = 4, 128
N_EXPERTS, TOP_K, D_EXPERT = 32, 4, 1024
SWIGLU_ALPHA, SWIGLU_LIMIT = 1.702, 7.0
MOE_BM = 256

COL_GATES = 0
COL_AQ, COL_AK, COL_AV, COL_AZ = 3072, 3584, 4096, 4608
COL_BQ, COL_BK, COL_BV = 5120, 5888, 6656
COL_SMALL = 7424
COL_CV, COL_CR, COL_CQ, COL_CK = 7680, 8192, 8704, 8960
N_PACKED = 9216


def _cparams(*sem):
    return pltpu.CompilerParams(dimension_semantics=sem, vmem_limit_bytes=VMEM_LIMIT)


def _sigmoid(x):
    return 1.0 / (1.0 + jnp.exp(-x))


def _silu(x):
    return x * _sigmoid(x)


def _softplus(x):
    return jnp.maximum(x, 0.0) + jnp.log(1.0 + jnp.exp(-jnp.abs(x)))


def _dot(a, b, precision=None):
    return jnp.dot(a, b, preferred_element_type=F32, precision=precision)


def _dot_nt(a, b, precision=None):
    return lax.dot_general(a, b, (((1,), (1,)), ((), ())), preferred_element_type=F32, precision=precision)


def _dot_tn(a, b, precision=None):
    return lax.dot_general(a, b, (((0,), (0,)), ((), ())), preferred_element_type=F32, precision=precision)


def _bdot(a, b):
    return lax.dot_general(a, b, (((2,), (1,)), ((0,), (0,))), preferred_element_type=F32)


def _bdot_nt(a, b):
    return lax.dot_general(a, b, (((2,), (2,)), ((0,), (0,))), preferred_element_type=F32)


def _bdot3(a, b):
    ah = a.astype(BF16)
    bh = b.astype(BF16)
    al = (a - ah.astype(F32)).astype(BF16)
    bl = (b - bh.astype(F32)).astype(BF16)
    return _bdot(ah, bh) + (_bdot(ah, bl) + _bdot(al, bh))


def _iota(shape, axis):
    return lax.broadcasted_iota(jnp.int32, shape, axis)


def _norm_matmul_body(x_ref, g_ref, w_ref, o_ref, h_ref):
    @pl.when(pl.program_id(1) == 0)
    def _():
        x = x_ref[...]
        ms = jnp.mean(x * x, axis=-1, keepdims=True)
        h_ref[...] = (x * lax.rsqrt(ms + RMS_EPS) * g_ref[...]).astype(h_ref.dtype)

    o_ref[...] = _dot(h_ref[...], w_ref[...]).astype(o_ref.dtype)


def norm_matmul(x, gain, w, *, tm, tn, name):
    T, D = x.shape
    N = w.shape[1]
    return pl.pallas_call(
        _norm_matmul_body,
        grid=(T // tm, N // tn),
        in_specs=[pl.BlockSpec((tm, D), lambda i, j: (i, 0)),
                  pl.BlockSpec((1, D), lambda i, j: (0, 0)),
                  pl.BlockSpec((D, tn), lambda i, j: (0, j))],
        out_specs=pl.BlockSpec((tm, tn), lambda i, j: (i, j)),
        out_shape=jax.ShapeDtypeStruct((T, N), F32),
        scratch_shapes=[pltpu.VMEM((tm, D), BF16)],
        compiler_params=_cparams("parallel", "arbitrary"),
        name=name,
    )(x, gain.reshape(1, D), w)


def _gdn_body(q_ref, k_ref, v_ref, z_ref, sm_ref, smt_ref, cq_ref, ck_ref, cv_ref, alog_ref, dtb_ref, gn_ref,
              o_ref, state_ref, tail_ref, u_ref, wq_ref, kd_ref, in_ref, dl_ref, os_ref, *, rows):
    h = pl.program_id(1)
    blk = pl.program_id(2)
    nchunk = rows // CHUNK

    @pl.when(blk == 0)
    def _():
        state_ref[...] = jnp.zeros_like(state_ref)
        tail_ref[...] = jnp.zeros_like(tail_ref)

    def conv_silu(x_ref, w_ref, slot):
        x = x_ref[...]
        xp = jnp.concatenate([tail_ref[slot], x], axis=0)
        w = w_ref[...]
        acc = x * w[GDN_CONV - 1:GDN_CONV, :]
        for s in range(1, GDN_CONV):
            acc = acc + pltpu.roll(xp, s, 0)[8:] * w[GDN_CONV - 1 - s:GDN_CONV - s, :]
        tail_ref[slot] = x[rows - 8:rows]
        return _silu(acc)

    q = conv_silu(q_ref, cq_ref, 0)
    k = conv_silu(k_ref, ck_ref, 1)
    v = conv_silu(v_ref, cv_ref, 2)
    q = q * lax.rsqrt(jnp.sum(q * q, axis=-1, keepdims=True) + L2_EPS) * (GDN_D ** -0.5)
    k = k * lax.rsqrt(jnp.sum(k * k, axis=-1, keepdims=True) + L2_EPS)

    lane = _iota((1, LANES), 1)
    sm = sm_ref[...]
    pick = lambda ref_val, idx: jnp.sum(jnp.where(lane == idx, ref_val, 0.0), axis=-1, keepdims=True)
    neg_a = -jnp.exp(pick(alog_ref[...], h))
    dt_b = pick(dtb_ref[...], h)
    g_col = jnp.broadcast_to(neg_a * _softplus(pick(sm, h) + dt_b), (rows, LANES))
    beta = jnp.broadcast_to(_sigmoid(pick(sm, h + GDN_HEADS)), (rows, LANES))
    g_row = neg_a * _softplus(smt_ref[pl.ds(h, 1), :] + dt_b)

    pos_c = _iota((rows, 1), 0) % CHUNK
    pos_r = _iota((1, rows), 1) % CHUNK
    step = 1
    while step < CHUNK:
        g_col = g_col + jnp.where(pos_c >= step, pltpu.roll(g_col, step, 0), 0.0)
        g_row = g_row + jnp.where(pos_r >= step, pltpu.roll(g_row, step, 1), 0.0)
        step *= 2
    eg = jnp.exp(g_col)

    ri = _iota((1, CHUNK, CHUNK), 1)
    ci = _iota((1, CHUNK, CHUNK), 2)
    incl = ri >= ci
    strict = ri > ci
    eye = (ri == ci).astype(F32)

    chunked = lambda t: t.reshape(nchunk, CHUNK, t.shape[-1])
    q3, k3, g3 = chunked(q), chunked(k), chunked(g_col)
    kb3 = chunked(k * beta)
    eg3 = chunked(eg)
    g_row3 = jnp.stack([g_row[:, c * CHUNK:(c + 1) * CHUNK] for c in range(nchunk)], axis=0)
    decay = jnp.where(incl, jnp.exp(jnp.minimum(g3[:, :, :CHUNK] - g_row3, 0.0)), 0.0)
    kq = jnp.concatenate([kb3, q3], axis=1).astype(BF16)
    s = _bdot_nt(kq, k3.astype(BF16))
    lower = jnp.where(strict, s[:, :CHUNK] * decay, 0.0)
    intra = jnp.where(incl, s[:, CHUNK:] * decay, 0.0)
    pw = -lower
    inv = eye + pw
    pw = _bdot3(pw, pw)
    for _ in range(4):
        both = _bdot3(jnp.concatenate([inv, pw], axis=1), pw)
        inv = inv + both[:, :CHUNK]
        pw = both[:, CHUNK:]
    inv = inv + _bdot3(inv, pw)
    uw = _bdot3(inv, jnp.concatenate([chunked(v * beta), kb3 * eg3], axis=2))
    g_last = g3[:, CHUNK - 1:CHUNK, :]
    u_ref[...] = uw[:, :, :GDN_D]
    wq_ref[...] = jnp.concatenate([uw[:, :, GDN_D:], q3 * eg3], axis=1).astype(BF16)
    kd_ref[...] = (k3 * jnp.exp(g_last - g3)).astype(BF16)
    in_ref[...] = intra.astype(BF16)
    dl_ref[...] = jnp.exp(g_last)

    def chunk_step(c, carry):
        st = state_ref[...]
        wq_s = _dot(wq_ref[c], st.astype(BF16))
        v16 = (u_ref[c] - wq_s[:CHUNK]).astype(BF16)
        o = wq_s[CHUNK:] + _dot(in_ref[c], v16)
        state_ref[...] = st * dl_ref[c][:, :1] + _dot_tn(kd_ref[c], v16)
        os_ref[pl.ds(pl.multiple_of(c * CHUNK, CHUNK), CHUNK), :] = o
        return carry

    lax.fori_loop(0, nchunk, chunk_step, 0, unroll=True)

    o = os_ref[...]
    ms = jnp.mean(o * o, axis=-1, keepdims=True)
    y = o * lax.rsqrt(ms + RMS_EPS) * gn_ref[...]
    o_ref[...] = (y * _silu(z_ref[...])).astype(o_ref.dtype)


def gdn_mixer(proj, conv_w, a_log, dt_bias, gnorm, *, batch, seq, rows):
    T = batch * seq
    nblk = seq // rows
    nchunk = rows // CHUNK
    small_t = proj[:, COL_SMALL:COL_SMALL + 2 * GDN_HEADS].T
    cb = lambda base: (lambda b, h, i: (b * nblk + i, base // LANES + h))
    pad = lambda v: jnp.zeros((1, LANES), F32).at[0, :GDN_HEADS].set(v)
    wspec = lambda base: pl.BlockSpec((GDN_CONV, LANES), lambda b, h, i: (0, base // LANES + h))
    vec = pl.BlockSpec((1, LANES), lambda b, h, i: (0, 0))
    return pl.pallas_call(
        functools.partial(_gdn_body, rows=rows),
        grid=(batch, GDN_HEADS, nblk),
        in_specs=[pl.BlockSpec((rows, LANES), cb(COL_AQ)),
                  pl.BlockSpec((rows, LANES), cb(COL_AK)),
                  pl.BlockSpec((rows, LANES), cb(COL_AV)),
                  pl.BlockSpec((rows, LANES), cb(COL_AZ)),
                  pl.BlockSpec((rows, LANES), lambda b, h, i: (b * nblk + i, COL_SMALL // LANES)),
                  pl.BlockSpec((2 * GDN_HEADS, rows), lambda b, h, i: (0, b * nblk + i)),
                  wspec(0), wspec(512), wspec(1024), vec, vec, vec],
        out_specs=pl.BlockSpec((rows, LANES), lambda b, h, i: (b * nblk + i, h)),
        out_shape=jax.ShapeDtypeStruct((T, GDN_HEADS * GDN_D), BF16),
        scratch_shapes=[pltpu.VMEM((GDN_D, GDN_D), F32),
                        pltpu.VMEM((3, 8, LANES), F32),
                        pltpu.VMEM((nchunk, CHUNK, GDN_D), F32),
                        pltpu.VMEM((nchunk, 2 * CHUNK, GDN_D), BF16),
                        pltpu.VMEM((nchunk, CHUNK, GDN_D), BF16),
                        pltpu.VMEM((nchunk, CHUNK, CHUNK), BF16),
                        pltpu.VMEM((nchunk, 1, LANES), F32),
                        pltpu.VMEM((rows, LANES), F32)],
        compiler_params=_cparams("parallel", "parallel", "arbitrary"),
        name="gdn_mixer",
    )(proj, proj, proj, proj, proj, small_t, conv_w, conv_w, conv_w, pad(a_log), pad(dt_bias),
      gnorm.reshape(1, GDN_D))


def _qk_rope_body(q_ref, k_ref, cos_ref, sin_ref, gq_ref, gk_ref, gm_ref, qo_ref, ko_ref):
    lane = _iota((1, 2 * LANES), 1)
    first_half = (lane % SWA_DH) < (SWA_DH // 2)
    cosf = cos_ref[...]
    sins = sin_ref[...]

    def norm_rope(x, gain):
        ms = _dot(x * x, gm_ref[...], HIGHEST)
        xn = x * lax.rsqrt(ms + RMS_EPS) * gain
        other = jnp.where(first_half, pltpu.roll(xn, 2 * LANES - SWA_DH // 2, 1),
                          pltpu.roll(xn, SWA_DH // 2, 1))
        return xn * cosf + other * sins

    qo_ref[...] = (norm_rope(q_ref[...], gq_ref[...]) * (SWA_DH ** -0.5)).astype(qo_ref.dtype)
    ko_ref[...] = norm_rope(k_ref[...], gk_ref[...]).astype(ko_ref.dtype)


def qk_norm_rope(proj, cosf, sins, q_gain, k_gain, *, tm):
    T = proj.shape[0]
    W = SWA_HEADS * SWA_DH
    grp = jnp.arange(W) // SWA_DH
    gmean = (grp[:, None] == grp[None, :]).astype(F32) / SWA_DH
    tile = lambda v: jnp.tile(v, SWA_HEADS).reshape(1, W)
    vec = pl.BlockSpec((1, W), lambda g, i: (0, 0))
    out = jax.ShapeDtypeStruct((len(SWA_GROUPS), T, W), BF16)
    return pl.pallas_call(
        _qk_rope_body,
        grid=(len(SWA_GROUPS), T // tm),
        in_specs=[pl.BlockSpec((tm, W), lambda g, i: (i, COL_BQ // W + g)),
                  pl.BlockSpec((tm, W), lambda g, i: (i, COL_BK // W + g)),
                  pl.BlockSpec((tm, W), lambda g, i: (i, 0)),
                  pl.BlockSpec((tm, W), lambda g, i: (i, 0)),
                  vec, vec, pl.BlockSpec((W, W), lambda g, i: (0, 0))],
        out_specs=[pl.BlockSpec((None, tm, W), lambda g, i: (g, i, 0)),
                   pl.BlockSpec((None, tm, W), lambda g, i: (g, i, 0))],
        out_shape=[out, out],
        compiler_params=_cparams("parallel", "parallel"),
        name="qk_norm_rope",
    )(proj, proj, cosf, sins, tile(q_gain), tile(k_gain), gmean)


def _band_attn_body(q_ref, kp_ref, kc_ref, vp_ref, vc_ref, o_ref, lse_ref, *, blocks_per_residue):
    i = pl.program_id(1)
    first = (i % blocks_per_residue) == 0
    c = SWA_BACK
    W = SWA_HEADS * SWA_DH
    q = q_ref[...]
    kcat = jnp.concatenate([kp_ref[...], kc_ref[...]], axis=0)
    vcat = jnp.concatenate([vp_ref[...], vc_ref[...]], axis=0)
    a = _iota((c, 2 * c), 0)
    b = _iota((c, 2 * c), 1)
    dist = a + c - b
    valid = (dist >= 0) & (dist <= SWA_BACK) & ((b >= c) | jnp.logical_not(first))
    lane = _iota((1, W), 1)
    o_acc = jnp.zeros((c, W), F32)
    lse_acc = jnp.zeros((c, W), F32)
    for hd in range(SWA_HEADS):
        hmask = (lane // SWA_DH) == hd
        qh = jnp.where(hmask, q, jnp.zeros_like(q))
        s = jnp.where(valid, _dot_nt(qh, kcat), -jnp.inf)
        m = jnp.max(s, axis=-1, keepdims=True)
        p = jnp.exp(s - m)
        l = jnp.sum(p, axis=-1, keepdims=True)
        pv = _dot(p.astype(BF16), vcat) / l
        o_acc = jnp.where(hmask, pv, o_acc)
        lse_acc = jnp.where(hmask, m + jnp.log(l), lse_acc)
    o_ref[...] = o_acc
    lse_ref[...] = lse_acc


def band_attention(q, k, v, *, dilation):
    B, S, W = q.shape
    c = SWA_BACK
    nblk = S // c
    cur = pl.BlockSpec((None, c, W), lambda b, i: (b, i, 0))
    prev = pl.BlockSpec((None, c, W), lambda b, i: (b, jnp.maximum(i - 1, 0), 0))
    out = jax.ShapeDtypeStruct((B, S, W), F32)
    return pl.pallas_call(
        functools.partial(_band_attn_body, blocks_per_residue=nblk // dilation),
        grid=(B, nblk),
        in_specs=[cur, prev, cur, prev, cur],
        out_specs=[cur, cur],
        out_shape=[out, out],
        compiler_params=_cparams("parallel", "parallel"),
        name=f"band_attention_d{dilation}",
    )(q, k, k, v, v)


def _gla_body(q_ref, k_ref, v_ref, r_ref, sm_ref, wg_ref, bg_ref, gn_ref, o_ref,
              state_ref, qs_ref, ks_ref, la_ref, os_ref, *, rows):
    blk = pl.program_id(1)
    nchunk = rows // CHUNK
    C = CHUNK

    @pl.when(blk == 0)
    def _():
        state_ref[...] = jnp.zeros_like(state_ref)

    x = _dot(sm_ref[...].astype(BF16), wg_ref[...]) + bg_ref[...]
    la_ref[...] = (jnp.minimum(x, 0.0) - jnp.log(1.0 + jnp.exp(-jnp.abs(x)))) * (1.0 / GLA_TAU)
    qs_ref[...] = q_ref[...] * (GLA_DK ** -0.5)
    ks_ref[...] = k_ref[...]

    ri = _iota((C, C), 0)
    ci = _iota((C, C), 1)
    tril = (ri >= ci).astype(F32)
    lane = _iota((1, LANES), 1)
    lrow = _iota((LANES, C), 0)
    gsum = [((lrow // GLA_DK) == h2).astype(BF16) for h2 in range(2)]
    hmask = [(lane // GLA_DK) == h2 for h2 in range(2)]
    nsub = C // GLA_SUB

    def chunk_step(c, carry):
        r0 = pl.multiple_of(c * C, C)
        for pair in range(GLA_HEADS // 2):
            cols = pl.ds(pair * LANES, LANES)
            qc = qs_ref[pl.ds(r0, C), cols]
            kc = ks_ref[pl.ds(r0, C), cols]
            la = la_ref[pl.ds(r0, C), cols]
            bcum = _dot(tril, la, HIGHEST)
            b_last = bcum[C - 1:C, :]
            s_rows = [[jnp.zeros((GLA_SUB, C), F32)] for _ in range(2)]
            for sb in range(1, nsub):
                lo = sb * GLA_SUB
                bref = bcum[lo - 1:lo, :]
                q_sb = (qc[lo:lo + GLA_SUB] * jnp.exp(bcum[lo:lo + GLA_SUB] - bref))
                k_sb = (kc * jnp.exp(jnp.minimum(bref - bcum, 0.0))).astype(BF16)
                for h2 in range(2):
                    qm = jnp.where(hmask[h2], q_sb, 0.0).astype(BF16)
                    s_rows[h2].append(_dot_nt(qm, k_sb))
            scores = []
            for h2 in range(2):
                s_off = jnp.concatenate(s_rows[h2], axis=0)
                scores.append(jnp.where((ri // GLA_SUB) > (ci // GLA_SUB), s_off, 0.0))
            for off in range(GLA_SUB):
                if off == 0:
                    prod = qc * kc
                else:
                    k_sh = pltpu.roll(kc, off, 0)
                    b_sh = pltpu.roll(bcum, off, 0)
                    prod = qc * k_sh * jnp.exp(jnp.minimum(bcum - b_sh, 0.0))
                p16 = prod.astype(BF16)
                on_diag = ((ri - ci) == off) & ((ri // GLA_SUB) == (ci // GLA_SUB))
                for h2 in range(2):
                    d = _dot(p16, gsum[h2])
                    scores[h2] = jnp.where(on_diag, d, scores[h2])
            q_dec = qc * jnp.exp(bcum)
            k_dec = kc * jnp.exp(b_last - bcum)
            dec_last = jnp.exp(b_last)
            for h2 in range(2):
                hd = pair * 2 + h2
                vc = v_ref[pl.ds(r0, C), pl.ds(hd * GLA_DV, GLA_DV)].astype(BF16)
                st = state_ref[hd]
                qd = jnp.where(hmask[h2], q_dec, 0.0).astype(BF16)
                kd = jnp.where(hmask[h2], k_dec, 0.0).astype(BF16)
                o = _dot(qd, st.astype(BF16)) + _dot(scores[h2].astype(BF16), vc)
                dl_col = jnp.sum(jnp.where(_iota((LANES, LANES), 0) == _iota((LANES, LANES), 1),
                                           jnp.broadcast_to(dec_last, (LANES, LANES)), 0.0),
                                 axis=-1, keepdims=True)
                state_ref[hd] = st * dl_col + _dot_tn(kd, vc)
                os_ref[pl.ds(r0, C), pl.ds(hd * GLA_DV, GLA_DV)] = o
        return carry

    lax.fori_loop(0, nchunk, chunk_step, 0)

    for hd in range(GLA_HEADS):
        cols = pl.ds(hd * GLA_DV, GLA_DV)
        o = os_ref[:, cols]
        ms = jnp.mean(o * o, axis=-1, keepdims=True)
        y = o * lax.rsqrt(ms + RMS_EPS) * gn_ref[...]
        o_ref[:, cols] = (y * _silu(r_ref[:, cols])).astype(o_ref.dtype)


def gla_mixer(proj, gate_up, gate_bias, gnorm, *, batch, seq, rows):
    T = batch * seq
    nblk = seq // rows
    QW = GLA_HEADS * GLA_DK
    VW = GLA_HEADS * GLA_DV
    row = lambda b, i: b * nblk + i
    wg = jnp.zeros((LANES, QW), F32).at[8:8 + GLA_RANK].set(gate_up).astype(BF16)
    return pl.pallas_call(
        functools.partial(_gla_body, rows=rows),
        grid=(batch, nblk),
        in_specs=[pl.BlockSpec((rows, QW), lambda b, i: (row(b, i), COL_CQ // QW)),
                  pl.BlockSpec((rows, QW), lambda b, i: (row(b, i), COL_CK // QW)),
                  pl.BlockSpec((rows, VW), lambda b, i: (row(b, i), COL_CV // VW)),
                  pl.BlockSpec((rows, VW), lambda b, i: (row(b, i), COL_CR // VW)),
                  pl.BlockSpec((rows, LANES), lambda b, i: (row(b, i), COL_SMALL // LANES)),
                  pl.BlockSpec((LANES, QW), lambda b, i: (0, 0)),
                  pl.BlockSpec((1, QW), lambda b, i: (0, 0)),
                  pl.BlockSpec((1, GLA_DV), lambda b, i: (0, 0))],
        out_specs=pl.BlockSpec((rows, VW), lambda b, i: (row(b, i), 0)),
        out_shape=jax.ShapeDtypeStruct((T, VW), BF16),
        scratch_shapes=[pltpu.VMEM((GLA_HEADS, LANES, GLA_DV), F32),
                        pltpu.VMEM((rows, QW), F32), pltpu.VMEM((rows, QW), F32),
                        pltpu.VMEM((rows, QW), F32), pltpu.VMEM((rows, VW), F32)],
        compiler_params=_cparams("parallel", "arbitrary"),
        name="gla_mixer",
    )(proj, proj, proj, proj, proj, wg, gate_bias.reshape(1, QW), gnorm.reshape(1, GLA_DV))


def _mix_out_body(x_ref, g0_ref, g1_ref, g2_ref, gb_ref, ya_ref, yc_ref,
                  o0_ref, o1_ref, o2_ref, l0_ref, l1_ref, l2_ref,
                  wa_ref, wb_ref, wc_ref, wo_ref, out_ref):
    l0, l1, l2 = l0_ref[...], l1_ref[...], l2_ref[...]
    m = jnp.maximum(jnp.maximum(l0, l1), l2)
    e0, e1, e2 = jnp.exp(l0 - m), jnp.exp(l1 - m), jnp.exp(l2 - m)
    ob = (e0 * o0_ref[...] + e1 * o1_ref[...] + e2 * o2_ref[...]) / (e0 + e1 + e2)
    gb = gb_ref[...]
    y = _sigmoid(g0_ref[...] + gb[0:1]) * _dot(ya_ref[...], wa_ref[...])
    y = y + _sigmoid(g1_ref[...] + gb[1:2]) * _dot(ob.astype(BF16), wb_ref[...])
    y = y + _sigmoid(g2_ref[...] + gb[2:3]) * _dot(yc_ref[...], wc_ref[...])
    out_ref[...] = x_ref[...] + _dot(y.astype(BF16), wo_ref[...])


def mix_out(x, proj, gate_bias, ya, yc, o_grp, lse_grp, wa, wb, wc, wo, *, tm):
    T, D = x.shape
    W = SWA_HEADS * SWA_DH
    rowblk = lambda w: pl.BlockSpec((tm, w), lambda i: (i, 0))
    full = lambda a: pl.BlockSpec(a.shape, lambda i: (0, 0))
    gate = lambda n: pl.BlockSpec((tm, D), lambda i: (i, COL_GATES // D + n))
    return pl.pallas_call(
        _mix_out_body,
        grid=(T // tm,),
        in_specs=[rowblk(D), gate(0), gate(1), gate(2), full(gate_bias), rowblk(ya.shape[1]),
                  rowblk(yc.shape[1]), rowblk(W), rowblk(W), rowblk(W), rowblk(W), rowblk(W), rowblk(W),
                  full(wa), full(wb), full(wc), full(wo)],
        out_specs=rowblk(D),
        out_shape=jax.ShapeDtypeStruct((T, D), F32),
        compiler_params=_cparams("parallel"),
        name="mix_out",
    )(x, proj, proj, proj, gate_bias, ya, yc, *o_grp, *lse_grp, wa, wb, wc, wo)


def _cross_attn_body(x_ref, gx_ref, wq_ref, kv_ref, gq_ref, gk_ref, wo_ref, out_ref):
    x = x_ref[...]
    ms = jnp.mean(x * x, axis=-1, keepdims=True)
    h = (x * lax.rsqrt(ms + RMS_EPS) * gx_ref[...]).astype(BF16)
    q = _dot(h, wq_ref[...])
    kv = kv_ref[...]
    KW = XA_HEADS * XA_DH
    outs = []
    for hd in range(XA_HEADS):
        qh = q[:, hd * XA_DH:(hd + 1) * XA_DH]
        kh = kv[:, hd * XA_DH:(hd + 1) * XA_DH]
        vh = kv[:, KW + hd * XA_DH:KW + (hd + 1) * XA_DH]
        qn = qh * lax.rsqrt(jnp.mean(qh * qh, axis=-1, keepdims=True) + RMS_EPS) * gq_ref[...]
        kn = kh * lax.rsqrt(jnp.mean(kh * kh, axis=-1, keepdims=True) + RMS_EPS) * gk_ref[...]
        s = _dot_nt(qn.astype(BF16), kn.astype(BF16)) * (XA_DH ** -0.5)
        m = jnp.max(s, axis=-1, keepdims=True)
        p = jnp.exp(s - m)
        l = jnp.sum(p, axis=-1, keepdims=True)
        outs.append((_dot(p.astype(BF16), vh.astype(BF16)) / l).astype(BF16))
    o = jnp.concatenate(outs, axis=-1)
    out_ref[...] = x + _dot(o, wo_ref[...])


def cross_attention(x, kv, gx, wq, gq, gk, wo, *, batch, seq, mem_len, tm):
    T, D = x.shape
    per_batch = seq // tm
    full = lambda a: pl.BlockSpec(a.shape, lambda i: (0, 0))
    gx, gq, gk = gx.reshape(1, D), gq.reshape(1, XA_DH), gk.reshape(1, XA_DH)
    return pl.pallas_call(
        _cross_attn_body,
        grid=(T // tm,),
        in_specs=[pl.BlockSpec((tm, D), lambda i: (i, 0)), full(gx), full(wq),
                  pl.BlockSpec((mem_len, kv.shape[1]), lambda i: (i // per_batch, 0)),
                  full(gq), full(gk), full(wo)],
        out_specs=pl.BlockSpec((tm, D), lambda i: (i, 0)),
        out_shape=jax.ShapeDtypeStruct((T, D), F32),
        compiler_params=_cparams("parallel"),
        name="cross_attention",
    )(x, gx, wq, kv, gq, gk, wo)


def _router_body(x_ref, gx_ref, wr_ref, br_ref, h_ref, idx_ref, gate_ref):
    x = x_ref[...]
    ms = jnp.mean(x * x, axis=-1, keepdims=True)
    h = x * lax.rsqrt(ms + RMS_EPS) * gx_ref[...]
    h_ref[...] = h
    lane = _iota((1, LANES), 1)
    logits = _dot(h, wr_ref[...], HIGHEST) + br_ref[...]
    logits = jnp.where(lane < N_EXPERTS, logits, -jnp.inf)
    idx_out = jnp.zeros(logits.shape, jnp.int32)
    val_out = jnp.full(logits.shape, -jnp.inf, F32)
    for k in range(TOP_K):
        m = jnp.max(logits, axis=-1, keepdims=True)
        sel = jnp.min(jnp.where(logits == m, lane, LANES), axis=-1, keepdims=True)
        idx_out = jnp.where(lane == k, sel, idx_out)
        val_out = jnp.where(lane == k, m, val_out)
        logits = jnp.where(lane == sel, -jnp.inf, logits)
    top = jnp.max(val_out, axis=-1, keepdims=True)
    e = jnp.exp(val_out - top)
    idx_ref[...] = idx_out
    gate_ref[...] = e / jnp.sum(e, axis=-1, keepdims=True)


def moe_router(x, gx, wr, br, *, tm):
    T, D = x.shape
    wr_p = jnp.zeros((D, LANES), F32).at[:, :N_EXPERTS].set(wr)
    br_p = jnp.zeros((1, LANES), F32).at[0, :N_EXPERTS].set(br)
    full = lambda a: pl.BlockSpec(a.shape, lambda i: (0, 0))
    gx = gx.reshape(1, D)
    return pl.pallas_call(
        _router_body,
        grid=(T // tm,),
        in_specs=[pl.BlockSpec((tm, D), lambda i: (i, 0)), full(gx), full(wr_p), full(br_p)],
        out_specs=[pl.BlockSpec((tm, D), lambda i: (i, 0)),
                   pl.BlockSpec((tm, LANES), lambda i: (i, 0)),
                   pl.BlockSpec((tm, LANES), lambda i: (i, 0))],
        out_shape=[jax.ShapeDtypeStruct((T, D), F32),
                   jax.ShapeDtypeStruct((T, LANES), jnp.int32),
                   jax.ShapeDtypeStruct((T, LANES), F32)],
        compiler_params=_cparams("parallel"),
        name="moe_router",
    )(x, gx, wr_p, br_p)


def _expert_body(be_ref, nused_ref, tok0_ref, tokn_ref, dstp_ref, dstc_ref, h_ref, wi_ref, bi_ref, wo_ref, bo_ref,
                 out_ref, xbuf, ybuf, wi16, wo16, gsem, ssem, *, dump_base):
    i = pl.program_id(0)
    nu = nused_ref[0]
    slot = i % 2

    @pl.when((i == 0) | (be_ref[i] != be_ref[jnp.maximum(i - 1, 0)]))
    def _():
        wi16[...] = wi_ref[...].astype(BF16)
        wo16[...] = wo_ref[...].astype(BF16)

    def gather(tok_ref, s, j):
        return pltpu.make_async_copy(h_ref.at[pl.ds(tok_ref[0, 0, j], 1)], xbuf.at[s, pl.ds(j, 1)], gsem.at[s])

    def scatter(dst_ref, s, j):
        return pltpu.make_async_copy(ybuf.at[s, pl.ds(j, 1)], out_ref.at[pl.ds(dst_ref[0, 0, j], 1)], ssem.at[s])

    def for_rows(fn):
        def body(j, carry):
            fn(j)
            return carry
        lax.fori_loop(0, MOE_BM, body, 0, unroll=8)

    def wait_gathered(s):
        pltpu.make_async_copy(xbuf.at[s], xbuf.at[s], gsem.at[s]).wait()

    def wait_scattered(s):
        pltpu.make_async_copy(ybuf.at[s], ybuf.at[s], ssem.at[s]).wait()

    @pl.when(i == 0)
    def _():
        for_rows(lambda j: gather(tok0_ref, 0, j).start())
        ybuf[...] = jnp.zeros_like(ybuf)
        pltpu.make_async_copy(ybuf.at[0], out_ref.at[pl.ds(dump_base, MOE_BM)], ssem.at[0]).start()

    def step(s):
        wait_gathered(s)
        for j in range(MOE_BM):
            gather(tokn_ref, 1 - s, j).start()
            scatter(dstp_ref, 1 - s, j).start()
        hh = _dot(xbuf[s].astype(BF16), wi16[...]) + bi_ref[...]
        glu = jnp.minimum(hh[:, :D_EXPERT], SWIGLU_LIMIT)
        lin = jnp.clip(hh[:, D_EXPERT:], -SWIGLU_LIMIT, SWIGLU_LIMIT)
        act = glu * _sigmoid(SWIGLU_ALPHA * glu) * (lin + 1.0)
        y = _dot(act.astype(BF16), wo16[...]) + bo_ref[...]
        wait_scattered(s)
        ybuf[s] = y

        @pl.when(i == nu - 1)
        def _():
            for_rows(lambda j: scatter(dstc_ref, s, j).start())
            wait_scattered(s)
            wait_scattered(1 - s)
            wait_gathered(1 - s)

    for s in range(2):
        pl.when((i < nu) & (slot == s))(functools.partial(step, s))


def expert_ffn(h, tok_buf, slot_dst, block_expert, n_used, w_in, b_in, w_out, b_out):
    T, D = h.shape
    P = tok_buf.shape[0]
    nb = P // MOE_BM
    E, _, F2 = w_in.shape
    dump_base = TOP_K * T
    dst_ext = jnp.concatenate([dump_base + MOE_BM + jnp.arange(MOE_BM, dtype=jnp.int32), slot_dst])
    dst_ext = dst_ext.reshape(nb + 1, 1, MOE_BM)
    idx_blk = lambda f: pl.BlockSpec((1, 1, MOE_BM), f, memory_space=pltpu.SMEM)
    grid_spec = pltpu.PrefetchScalarGridSpec(
        num_scalar_prefetch=2,
        grid=(nb,),
        in_specs=[idx_blk(lambda i, be, nu: (i, 0, 0)),
                  idx_blk(lambda i, be, nu: (jnp.minimum(i + 1, nb - 1), 0, 0)),
                  idx_blk(lambda i, be, nu: (i, 0, 0)),
                  idx_blk(lambda i, be, nu: (i + 1, 0, 0)),
                  pl.BlockSpec(memory_space=pl.ANY),
                  pl.BlockSpec((None, D, F2), lambda i, be, nu: (be[i], 0, 0)),
                  pl.BlockSpec((None, 1, F2), lambda i, be, nu: (be[i], 0, 0)),
                  pl.BlockSpec((None, F2 // 2, D), lambda i, be, nu: (be[i], 0, 0)),
                  pl.BlockSpec((None, 1, D), lambda i, be, nu: (be[i], 0, 0))],
        out_specs=pl.BlockSpec(memory_space=pl.ANY),
        scratch_shapes=[pltpu.VMEM((2, MOE_BM, D), F32), pltpu.VMEM((2, MOE_BM, D), F32),
                        pltpu.VMEM((D, F2), BF16), pltpu.VMEM((F2 // 2, D), BF16),
                        pltpu.SemaphoreType.DMA((2,)), pltpu.SemaphoreType.DMA((2,))],
    )
    tok3 = tok_buf.reshape(nb, 1, MOE_BM)
    return pl.pallas_call(
        functools.partial(_expert_body, dump_base=dump_base),
        grid_spec=grid_spec,
        out_shape=jax.ShapeDtypeStruct((dump_base + 2 * MOE_BM, D), F32),
        compiler_params=_cparams("arbitrary"),
        name="expert_ffn",
    )(block_expert, n_used, tok3, tok3, dst_ext, dst_ext, h,
      w_in, b_in.reshape(E, 1, F2), w_out, b_out.reshape(E, 1, D))


def _moe_combine_body(x_ref, y0_ref, y1_ref, y2_ref, y3_ref, gate_ref, out_ref):
    g = gate_ref[...]
    acc = x_ref[...]
    for k, y_ref in enumerate((y0_ref, y1_ref, y2_ref, y3_ref)):
        acc = acc + g[:, k:k + 1] * y_ref[...]
    out_ref[...] = acc


def moe_combine(x, y4, gates, *, tm):
    T, D = x.shape
    nblk = T // tm
    ysp = lambda k: pl.BlockSpec((tm, D), lambda i: (k * nblk + i, 0))
    return pl.pallas_call(
        _moe_combine_body,
        grid=(nblk,),
        in_specs=[pl.BlockSpec((tm, D), lambda i: (i, 0)), ysp(0), ysp(1), ysp(2), ysp(3),
                  pl.BlockSpec((tm, LANES), lambda i: (i, 0))],
        out_specs=pl.BlockSpec((tm, D), lambda i: (i, 0)),
        out_shape=jax.ShapeDtypeStruct((T, D), F32),
        compiler_params=_cparams("parallel"),
        name="moe_combine",
    )(x, y4, y4, y4, y4, gates)


def _pack_w_in(w_in):
    L, D, _ = w_in.shape
    o_alpha = 2048
    o_b = 2056
    o_c = o_b + 2304
    o_low = o_c + 1536
    o_gates = o_low + GLA_RANK
    c = w_in[:, :, o_c:o_low]
    parts = [w_in[:, :, o_gates:],
             w_in[:, :, 0:2048],
             w_in[:, :, o_b:o_c],
             w_in[:, :, o_alpha:o_b], w_in[:, :, o_low:o_gates],
             jnp.zeros((L, D, 256 - 8 - GLA_RANK), w_in.dtype),
             c[:, :, 512:1024], c[:, :, 1024:1536], c[:, :, 0:256], c[:, :, 256:512]]
    packed = jnp.concatenate(parts, axis=-1)
    assert packed.shape[-1] == N_PACKED
    return packed.astype(BF16)


def _to_residue_major(t, dilation):
    if dilation == 1:
        return t
    *lead, S, W = t.shape
    return jnp.swapaxes(t.reshape(*lead, S // dilation, dilation, W), -2, -3).reshape(*lead, S, W)


def _from_residue_major(t, dilation):
    if dilation == 1:
        return t
    *lead, S, W = t.shape
    return jnp.swapaxes(t.reshape(*lead, dilation, S // dilation, W), -2, -3).reshape(*lead, S, W)


def _routing_tables(idx, n_tokens):
    A = n_tokens * TOP_K
    P = A + N_EXPERTS * MOE_BM
    e_flat = idx.reshape(A)
    onehot = (e_flat[:, None] == jnp.arange(N_EXPERTS, dtype=jnp.int32)[None, :]).astype(jnp.int32)
    csum = jnp.cumsum(onehot, axis=0)
    rank = jnp.sum(onehot * csum, axis=1) - 1
    counts = csum[-1]
    padded = (counts + MOE_BM - 1) // MOE_BM * MOE_BM
    pad_ends = jnp.cumsum(padded)
    pad_starts = pad_ends - padded
    dest = (pad_starts[e_flat] + rank).astype(jnp.int32)
    slot_a = jnp.full((P,), -1, jnp.int32).at[dest].set(jnp.arange(A, dtype=jnp.int32))
    tok_buf = jnp.maximum(slot_a, 0) // TOP_K
    p = jnp.arange(P, dtype=jnp.int32)
    dump = TOP_K * n_tokens + ((p // MOE_BM) % 2) * MOE_BM + p % MOE_BM
    slot_dst = jnp.where(slot_a < 0, dump, (slot_a % TOP_K) * n_tokens + slot_a // TOP_K)
    block_expert = jnp.minimum(
        jnp.searchsorted(pad_ends, jnp.arange(P // MOE_BM, dtype=jnp.int32) * MOE_BM, side="right"),
        N_EXPERTS - 1).astype(jnp.int32)
    n_used = (pad_ends[-1:] // MOE_BM).astype(jnp.int32)
    return tok_buf, slot_dst, block_expert, n_used


def kernel(x, mem, positions, norm_mix, w_in, gate_bias, gdn_conv, gdn_a_log, gdn_dt_bias, gdn_norm, swa_q_norm, swa_k_norm, gla_gate_up, gla_gate_bias, gla_norm, w_branch_a, w_branch_b, w_branch_c, w_mix_out, norm_cross, norm_mem, xa_wq, xa_wkv, xa_q_norm, xa_k_norm, xa_wo, norm_ffn, router_w, router_b, moe_w_in, moe_b_in, moe_w_out, moe_b_out):
    B, S, D = x.shape
    T = B * S
    M = mem.shape[1]
    depth = w_in.shape[0]
    W = SWA_HEADS * SWA_DH

    inv_freq = ROPE_THETA ** (-jnp.arange(0, SWA_DH, 2, dtype=F32) / SWA_DH)
    ang = positions.astype(F32).reshape(T, 1) * inv_freq[None, :]
    cos, sin = jnp.cos(ang), jnp.sin(ang)
    cosf = jnp.tile(jnp.concatenate([cos, cos], axis=-1), (1, SWA_HEADS))
    sins = jnp.tile(jnp.concatenate([-sin, sin], axis=-1), (1, SWA_HEADS))

    w_in_p = _pack_w_in(w_in)
    xf = x.reshape(T, D)
    memf = mem.reshape(B * M, D)

    for l in range(depth):
        proj = norm_matmul(xf, norm_mix[l], w_in_p[l], tm=1024, tn=1536, name="in_proj")
        ya = gdn_mixer(proj, gdn_conv[l], gdn_a_log[l], gdn_dt_bias[l], gdn_norm[l], batch=B, seq=S, rows=512)
        qn, kn = qk_norm_rope(proj, cosf, sins, swa_q_norm[l], swa_k_norm[l], tm=1024)
        o_grp, lse_grp = [], []
        for gi, (window, dil) in enumerate(SWA_GROUPS):
            assert window // dil == SWA_BACK
            vg = proj[:, COL_BV + gi * W:COL_BV + (gi + 1) * W].astype(BF16)
            rm = lambda t: _to_residue_major(t.reshape(B, S, W), dil)
            o_g, lse_g = band_attention(rm(qn[gi]), rm(kn[gi]), rm(vg), dilation=dil)
            o_grp.append(_from_residue_major(o_g, dil).reshape(T, W))
            lse_grp.append(_from_residue_major(lse_g, dil).reshape(T, W))
        yc = gla_mixer(proj, gla_gate_up[l], gla_gate_bias[l], gla_norm[l], batch=B, seq=S, rows=512)
        xf = mix_out(xf, proj, gate_bias[l], ya, yc, o_grp, lse_grp,
                     w_branch_a[l].astype(BF16), w_branch_b[l].astype(BF16), w_branch_c[l].astype(BF16),
                     w_mix_out[l].astype(BF16), tm=512)

        kv = norm_matmul(memf, norm_mem[l], xa_wkv[l].astype(BF16), tm=min(1024, B * M), tn=1024, name="mem_kv")
        xf = cross_attention(xf, kv, norm_cross[l], xa_wq[l].astype(BF16), xa_q_norm[l], xa_k_norm[l],
                             xa_wo[l].astype(BF16), batch=B, seq=S, mem_len=M, tm=512)

        h, idx, gates = moe_router(xf, norm_ffn[l], router_w[l], router_b[l], tm=1024)
        tok_buf, slot_dst, block_expert, n_used = _routing_tables(idx[:, :TOP_K], T)
        y4 = expert_ffn(h, tok_buf, slot_dst, block_expert, n_used, moe_w_in[l], moe_b_in[l],
                        moe_w_out[l], moe_b_out[l])
        xf = moe_combine(xf, y4, gates, tm=512)

    return xf.reshape(B, S, D)
```

```python
import functools

import jax
import jax.numpy as jnp
from jax import lax
from jax.experimental import pallas as pl
from jax.experimental.pallas import tpu as pltpu

F32 = jnp.float32
BF16 = jnp.bfloat16
HIGHEST = lax.Precision.HIGHEST

RMS_EPS = 1e-6
L2_EPS = 1e-6
LANES = 128
VMEM_LIMIT = 56 * 1024 * 1024

D_MODEL = 1024
GDN_HEADS, GDN_D, GDN_CONV, CHUNK = 4, 128, 4, 64
SWA_GROUPS = ((128, 1), (512, 4), (2048, 16))
SWA_HEADS, SWA_DH, SWA_BACK = 4, 64, 128
ROPE_THETA = 10000.0
GLA_HEADS, GLA_DK, GLA_DV, GLA_RANK, GLA_TAU = 4, 64, 128, 16, 16.0
GLA_SUB = 8
XA_HEADS, XA_DH = 4, 128
N_EXPERTS, TOP_K, D_EXPERT = 32, 4, 1024
SWIGLU_ALPHA, SWIGLU_LIMIT = 1.702, 7.0
MOE_BM = 256

COL_GATES = 0
COL_AQ, COL_AK, COL_AV, COL_AZ = 3072, 3584, 4096, 4608
COL_BQ, COL_BK, COL_BV = 5120, 5888, 6656
COL_SMALL = 7424
COL_CV, COL_CR, COL_CQ, COL_CK = 7680, 8192, 8704, 8960
N_PACKED = 9216


def _cparams(*sem):
    return pltpu.CompilerParams(dimension_semantics=sem, vmem_limit_bytes=VMEM_LIMIT)


def _sigmoid(x):
    return 1.0 / (1.0 + jnp.exp(-x))


def _silu(x):
    return x * _sigmoid(x)


def _softplus(x):
    return jnp.maximum(x, 0.0) + jnp.log(1.0 + jnp.exp(-jnp.abs(x)))


def _dot(a, b, precision=None):
    return jnp.dot(a, b, preferred_element_type=F32, precision=precision)


def _dot_nt(a, b, precision=None):
    return lax.dot_general(a, b, (((1,), (1,)), ((), ())), preferred_element_type=F32, precision=precision)


def _dot_tn(a, b, precision=None):
    return lax.dot_general(a, b, (((0,), (0,)), ((), ())), preferred_element_type=F32, precision=precision)


def _bdot(a, b):
    return lax.dot_general(a, b, (((2,), (1,)), ((0,), (0,))), preferred_element_type=F32)


def _bdot_nt(a, b):
    return lax.dot_general(a, b, (((2,), (2,)), ((0,), (0,))), preferred_element_type=F32)


def _bdot3(a, b):
    ah = a.astype(BF16)
    bh = b.astype(BF16)
    al = (a - ah.astype(F32)).astype(BF16)
    bl = (b - bh.astype(F32)).astype(BF16)
    return _bdot(ah, bh) + (_bdot(ah, bl) + _bdot(al, bh))


def _iota(shape, axis):
    return lax.broadcasted_iota(jnp.int32, shape, axis)


def _norm_matmul_body(x_ref, g_ref, w_ref, o_ref, h_ref):
    @pl.when(pl.program_id(1) == 0)
    def _():
        x = x_ref[...]
        ms = jnp.mean(x * x, axis=-1, keepdims=True)
        h_ref[...] = (x * lax.rsqrt(ms + RMS_EPS) * g_ref[...]).astype(h_ref.dtype)

    o_ref[...] = _dot(h_ref[...], w_ref[...]).astype(o_ref.dtype)


def norm_matmul(x, gain, w, *, tm, tn, name):
    T, D = x.shape
    N = w.shape[1]
    return pl.pallas_call(
        _norm_matmul_body,
        grid=(T // tm, N // tn),
        in_specs=[pl.BlockSpec((tm, D), lambda i, j: (i, 0)),
                  pl.BlockSpec((1, D), lambda i, j: (0, 0)),
                  pl.BlockSpec((D, tn), lambda i, j: (0, j))],
        out_specs=pl.BlockSpec((tm, tn), lambda i, j: (i, j)),
        out_shape=jax.ShapeDtypeStruct((T, N), F32),
        scratch_shapes=[pltpu.VMEM((tm, D), BF16)],
        compiler_params=_cparams("parallel", "arbitrary"),
        name=name,
    )(x, gain.reshape(1, D), w)


def _gdn_body(q_ref, k_ref, v_ref, z_ref, sm_ref, smt_ref, cq_ref, ck_ref, cv_ref, alog_ref, dtb_ref, gn_ref,
              o_ref, state_ref, tail_ref, u_ref, wq_ref, kd_ref, in_ref, dl_ref, os_ref, *, rows):
    blk = pl.program_id(1)
    nchunk = rows // CHUNK

    @pl.when(blk == 0)
    def _():
        state_ref[...] = jnp.zeros_like(state_ref)
        tail_ref[...] = jnp.zeros_like(tail_ref)

    def conv_silu(x_ref, w_ref, slot):
        x = x_ref[...]
        xp = jnp.concatenate([tail_ref[slot], x], axis=0)
        w = w_ref[...]
        acc = x * w[GDN_CONV - 1:GDN_CONV, :]
        for s in range(1, GDN_CONV):
            acc = acc + pltpu.roll(xp, s, 0)[8:] * w[GDN_CONV - 1 - s:GDN_CONV - s, :]
        tail_ref[slot] = x[rows - 8:rows]
        return _silu(acc)

    q_all = conv_silu(q_ref, cq_ref, 0)
    k_all = conv_silu(k_ref, ck_ref, 1)
    v_all = conv_silu(v_ref, cv_ref, 2)
    sm = sm_ref[...]
    pos_c = _iota((rows, 1), 0) % CHUNK
    pos_r = _iota((1, rows), 1) % CHUNK
    ri = _iota((1, CHUNK, CHUNK), 1)
    ci = _iota((1, CHUNK, CHUNK), 2)
    incl = ri >= ci
    strict = ri > ci
    eye = (ri == ci).astype(F32)
    chunked = lambda t: t.reshape(nchunk, CHUNK, t.shape[-1])

    for hd in range(GDN_HEADS):
        _gdn_chunk_local(hd, q_all, k_all, v_all, sm, smt_ref, alog_ref, dtb_ref, pos_c, pos_r, incl, strict, eye,
                         chunked, u_ref, wq_ref, kd_ref, in_ref, dl_ref, rows=rows)

    def chunk_step(c, carry):
        for hd in range(GDN_HEADS):
            st = state_ref[hd]
            wq_s = _dot(wq_ref[hd, c], st.astype(BF16))
            v16 = (u_ref[hd, c] - wq_s[:CHUNK]).astype(BF16)
            o = wq_s[CHUNK:] + _dot(in_ref[hd, c], v16)
            state_ref[hd] = st * dl_ref[hd, c][:, :1] + _dot_tn(kd_ref[hd, c], v16)
            os_ref[pl.ds(pl.multiple_of(c * CHUNK, CHUNK), CHUNK), pl.ds(hd * GDN_D, GDN_D)] = o
        return carry

    lax.fori_loop(0, nchunk, chunk_step, 0, unroll=True)

    for hd in range(GDN_HEADS):
        cols = pl.ds(hd * GDN_D, GDN_D)
        o = os_ref[:, cols]
        ms = jnp.mean(o * o, axis=-1, keepdims=True)
        y = o * lax.rsqrt(ms + RMS_EPS) * gn_ref[...]
        o_ref[:, cols] = (y * _silu(z_ref[:, cols])).astype(o_ref.dtype)


def _gdn_chunk_local(hd, q_all, k_all, v_all, sm, smt_ref, alog_ref, dtb_ref, pos_c, pos_r, incl, strict, eye,
                     chunked, u_ref, wq_ref, kd_ref, in_ref, dl_ref, *, rows):
    nchunk = rows // CHUNK
    cols = slice(hd * GDN_D, (hd + 1) * GDN_D)
    q, k, v = q_all[:, cols], k_all[:, cols], v_all[:, cols]
    q = q * lax.rsqrt(jnp.sum(q * q, axis=-1, keepdims=True) + L2_EPS) * (GDN_D ** -0.5)
    k = k * lax.rsqrt(jnp.sum(k * k, axis=-1, keepdims=True) + L2_EPS)

    neg_a = -jnp.exp(alog_ref[:, hd:hd + 1])
    dt_b = dtb_ref[:, hd:hd + 1]
    g_col = jnp.broadcast_to(neg_a * _softplus(sm[:, hd:hd + 1] + dt_b), (rows, LANES))
    beta = jnp.broadcast_to(_sigmoid(sm[:, GDN_HEADS + hd:GDN_HEADS + hd + 1]), (rows, LANES))
    g_row = neg_a * _softplus(smt_ref[hd:hd + 1, :] + dt_b)

    step = 1
    while step < CHUNK:
        g_col = g_col + jnp.where(pos_c >= step, pltpu.roll(g_col, step, 0), 0.0)
        g_row = g_row + jnp.where(pos_r >= step, pltpu.roll(g_row, step, 1), 0.0)
        step *= 2
    eg = jnp.exp(g_col)

    q3, k3, g3 = chunked(q), chunked(k), chunked(g_col)
    kb3 = chunked(k * beta)
    eg3 = chunked(eg)
    g_row3 = jnp.stack([g_row[:, c * CHUNK:(c + 1) * CHUNK] for c in range(nchunk)], axis=0)
    decay = jnp.where(incl, jnp.exp(jnp.minimum(g3[:, :, :CHUNK] - g_row3, 0.0)), 0.0)
    kq = jnp.concatenate([kb3, q3], axis=1).astype(BF16)
    s = _bdot_nt(kq, k3.astype(BF16))
    lower = jnp.where(strict, s[:, :CHUNK] * decay, 0.0)
    intra = jnp.where(incl, s[:, CHUNK:] * decay, 0.0)
    pw = -lower
    inv = eye + pw
    pw = _bdot3(pw, pw)
    for _ in range(4):
        both = _bdot3(jnp.concatenate([inv, pw], axis=1), pw)
        inv = inv + both[:, :CHUNK]
        pw = both[:, CHUNK:]
    inv = inv + _bdot3(inv, pw)
    uw = _bdot3(inv, jnp.concatenate([chunked(v * beta), kb3 * eg3], axis=2))
    g_last = g3[:, CHUNK - 1:CHUNK, :]
    u_ref[hd] = uw[:, :, :GDN_D]
    wq_ref[hd] = jnp.concatenate([uw[:, :, GDN_D:], q3 * eg3], axis=1).astype(BF16)
    kd_ref[hd] = (k3 * jnp.exp(g_last - g3)).astype(BF16)
    in_ref[hd] = intra.astype(BF16)
    dl_ref[hd] = jnp.exp(g_last)


def gdn_mixer(proj, conv_w, a_log, dt_bias, gnorm, *, batch, seq, rows):
    T = batch * seq
    nblk = seq // rows
    nchunk = rows // CHUNK
    H = GDN_HEADS
    HW = H * GDN_D
    small_t = proj[:, COL_SMALL:COL_SMALL + 2 * H].T
    cb = lambda base: pl.BlockSpec((rows, HW), lambda b, i: (b * nblk + i, base // HW))
    pad = lambda v: jnp.zeros((1, LANES), F32).at[0, :H].set(v)
    wspec = lambda n: pl.BlockSpec((GDN_CONV, HW), lambda b, i: (0, n))
    vec = pl.BlockSpec((1, LANES), lambda b, i: (0, 0))
    return pl.pallas_call(
        functools.partial(_gdn_body, rows=rows),
        grid=(batch, nblk),
        in_specs=[cb(COL_AQ), cb(COL_AK), cb(COL_AV), cb(COL_AZ),
                  pl.BlockSpec((rows, LANES), lambda b, i: (b * nblk + i, COL_SMALL // LANES)),
                  pl.BlockSpec((2 * H, rows), lambda b, i: (0, b * nblk + i)),
                  wspec(0), wspec(1), wspec(2), vec, vec, vec],
        out_specs=pl.BlockSpec((rows, HW), lambda b, i: (b * nblk + i, 0)),
        out_shape=jax.ShapeDtypeStruct((T, HW), BF16),
        scratch_shapes=[pltpu.VMEM((H, GDN_D, GDN_D), F32),
                        pltpu.VMEM((3, 8, HW), F32),
                        pltpu.VMEM((H, nchunk, CHUNK, GDN_D), F32),
                        pltpu.VMEM((H, nchunk, 2 * CHUNK, GDN_D), BF16),
                        pltpu.VMEM((H, nchunk, CHUNK, GDN_D), BF16),
                        pltpu.VMEM((H, nchunk, CHUNK, CHUNK), BF16),
                        pltpu.VMEM((H, nchunk, 1, LANES), F32),
                        pltpu.VMEM((rows, HW), F32)],
        compiler_params=_cparams("parallel", "arbitrary"),
        name="gdn_mixer",
    )(proj, proj, proj, proj, proj, small_t, conv_w, conv_w, conv_w, pad(a_log), pad(dt_bias),
      gnorm.reshape(1, GDN_D))


def _qk_rope_body(q_ref, k_ref, v_ref, cos_ref, sin_ref, gq_ref, gk_ref, gm_ref, qo_ref, ko_ref, vo_ref):
    lane = _iota((1, 2 * LANES), 1)
    first_half = (lane % SWA_DH) < (SWA_DH // 2)
    cosf = cos_ref[...]
    sins = sin_ref[...]
    vo_ref[...] = v_ref[...].astype(vo_ref.dtype)

    def norm_rope(x, gain):
        sq = x * x
        hi = sq.astype(BF16)
        lo = (sq - hi.astype(F32)).astype(BF16)
        ms = _dot(hi, gm_ref[...]) + _dot(lo, gm_ref[...])
        xn = x * lax.rsqrt(ms + RMS_EPS) * gain
        other = jnp.where(first_half, pltpu.roll(xn, 2 * LANES - SWA_DH // 2, 1),
                          pltpu.roll(xn, SWA_DH // 2, 1))
        return xn * cosf + other * sins

    qo_ref[...] = (norm_rope(q_ref[...], gq_ref[...]) * (SWA_DH ** -0.5)).astype(qo_ref.dtype)
    ko_ref[...] = norm_rope(k_ref[...], gk_ref[...]).astype(ko_ref.dtype)


def qk_norm_rope(proj, cosf, sins, q_gain, k_gain, *, tm):
    T = proj.shape[0]
    W = SWA_HEADS * SWA_DH
    grp = jnp.arange(W) // SWA_DH
    gmean = ((grp[:, None] == grp[None, :]).astype(F32) / SWA_DH).astype(BF16)
    tile = lambda v: jnp.tile(v, SWA_HEADS).reshape(1, W)
    vec = pl.BlockSpec((1, W), lambda g, i: (0, 0))
    out = jax.ShapeDtypeStruct((len(SWA_GROUPS), T, W), BF16)
    ospec = pl.BlockSpec((None, tm, W), lambda g, i: (g, i, 0))
    return pl.pallas_call(
        _qk_rope_body,
        grid=(len(SWA_GROUPS), T // tm),
        in_specs=[pl.BlockSpec((tm, W), lambda g, i: (i, COL_BQ // W + g)),
                  pl.BlockSpec((tm, W), lambda g, i: (i, COL_BK // W + g)),
                  pl.BlockSpec((tm, W), lambda g, i: (i, COL_BV // W + g)),
                  pl.BlockSpec((tm, W), lambda g, i: (i, 0)),
                  pl.BlockSpec((tm, W), lambda g, i: (i, 0)),
                  vec, vec, pl.BlockSpec((W, W), lambda g, i: (0, 0))],
        out_specs=[ospec, ospec, ospec],
        out_shape=[out, out, out],
        compiler_params=_cparams("parallel", "parallel"),
        name="qk_norm_rope",
    )(proj, proj, proj, cosf, sins, tile(q_gain), tile(k_gain), gmean)


def _band_attn_body(q_ref, kp_ref, kc_ref, vp_ref, vc_ref, o_ref, lse_ref, *, tiles_per_residue):
    i = pl.program_id(1)
    first = (i % tiles_per_residue) == 0
    c = SWA_BACK
    rows = q_ref.shape[0]
    W = SWA_HEADS * SWA_DH
    q = q_ref[...]
    kcat = jnp.concatenate([kp_ref[...], kc_ref[...]], axis=0)
    vcat = jnp.concatenate([vp_ref[...], vc_ref[...]], axis=0)
    a = _iota((rows, c + rows), 0)
    b = _iota((rows, c + rows), 1)
    dist = a + c - b
    valid = (dist >= 0) & (dist <= SWA_BACK) & ((b >= c) | jnp.logical_not(first))
    lane = _iota((1, W), 1)
    o_acc = jnp.zeros((rows, W), F32)
    lse_acc = jnp.zeros((rows, W), F32)
    for hd in range(SWA_HEADS):
        hmask = (lane // SWA_DH) == hd
        qh = jnp.where(hmask, q, jnp.zeros_like(q))
        s = jnp.where(valid, _dot_nt(qh, kcat), -jnp.inf)
        m = jnp.max(s, axis=-1, keepdims=True)
        p = jnp.exp(s - m)
        l = jnp.sum(p, axis=-1, keepdims=True)
        pv = _dot(p.astype(BF16), vcat) / l
        o_acc = jnp.where(hmask, pv, o_acc)
        lse_acc = jnp.where(hmask, m + jnp.log(l), lse_acc)
    o_ref[...] = o_acc
    lse_ref[...] = lse_acc


def band_attention(q, k, v, *, dilation):
    B, S, W = q.shape
    c = SWA_BACK
    rows = 2 * c
    assert (S // dilation) % rows == 0
    ntile = S // rows
    cur = pl.BlockSpec((None, rows, W), lambda b, i: (b, i, 0))
    prev = pl.BlockSpec((None, c, W), lambda b, i: (b, jnp.maximum(i * (rows // c) - 1, 0), 0))
    out = jax.ShapeDtypeStruct((B, S, W), F32)
    return pl.pallas_call(
        functools.partial(_band_attn_body, tiles_per_residue=ntile // dilation),
        grid=(B, ntile),
        in_specs=[cur, prev, cur, prev, cur],
        out_specs=[cur, cur],
        out_shape=[out, out],
        compiler_params=_cparams("parallel", "parallel"),
        name=f"band_attention_d{dilation}",
    )(q, k, k, v, v)


def _gla_body(q_ref, k_ref, v_ref, r_ref, sm_ref, wg_ref, bg_ref, gn_ref, o_ref,
              state_ref, qs_ref, ks_ref, la_ref, os_ref, *, rows):
    blk = pl.program_id(1)
    nchunk = rows // CHUNK
    C = CHUNK

    @pl.when(blk == 0)
    def _():
        state_ref[...] = jnp.zeros_like(state_ref)

    x = _dot(sm_ref[...].astype(BF16), wg_ref[...]) + bg_ref[...]
    la_ref[...] = (jnp.minimum(x, 0.0) - jnp.log(1.0 + jnp.exp(-jnp.abs(x)))) * (1.0 / GLA_TAU)
    qs_ref[...] = q_ref[...] * (GLA_DK ** -0.5)
    ks_ref[...] = k_ref[...]

    ri = _iota((C, C), 0)
    ci = _iota((C, C), 1)
    tril = (ri >= ci).astype(F32)
    lane = _iota((1, LANES), 1)
    lrow = _iota((LANES, C), 0)
    gsum = [((lrow // GLA_DK) == h2).astype(BF16) for h2 in range(2)]
    hmask = [(lane // GLA_DK) == h2 for h2 in range(2)]
    nsub = C // GLA_SUB

    def chunk_step(c, carry):
        r0 = pl.multiple_of(c * C, C)
        for pair in range(GLA_HEADS // 2):
            cols = pl.ds(pair * LANES, LANES)
            qc = qs_ref[pl.ds(r0, C), cols]
            kc = ks_ref[pl.ds(r0, C), cols]
            la = la_ref[pl.ds(r0, C), cols]
            bcum = _dot(tril, la, HIGHEST)
            b_last = bcum[C - 1:C, :]
            s_rows = [[jnp.zeros((GLA_SUB, C), F32)] for _ in range(2)]
            for sb in range(1, nsub):
                lo = sb * GLA_SUB
                bref = bcum[lo - 1:lo, :]
                q_sb = (qc[lo:lo + GLA_SUB] * jnp.exp(bcum[lo:lo + GLA_SUB] - bref))
                k_sb = (kc * jnp.exp(jnp.minimum(bref - bcum, 0.0))).astype(BF16)
                for h2 in range(2):
                    qm = jnp.where(hmask[h2], q_sb, 0.0).astype(BF16)
                    s_rows[h2].append(_dot_nt(qm, k_sb))
            scores = []
            for h2 in range(2):
                s_off = jnp.concatenate(s_rows[h2], axis=0)
                scores.append(jnp.where((ri // GLA_SUB) > (ci // GLA_SUB), s_off, 0.0))
            for off in range(GLA_SUB):
                if off == 0:
                    prod = qc * kc
                else:
                    k_sh = pltpu.roll(kc, off, 0)
                    b_sh = pltpu.roll(bcum, off, 0)
                    prod = qc * k_sh * jnp.exp(jnp.minimum(bcum - b_sh, 0.0))
                p16 = prod.astype(BF16)
                on_diag = ((ri - ci) == off) & ((ri // GLA_SUB) == (ci // GLA_SUB))
                for h2 in range(2):
                    d = _dot(p16, gsum[h2])
                    scores[h2] = jnp.where(on_diag, d, scores[h2])
            q_dec = qc * jnp.exp(bcum)
            k_dec = kc * jnp.exp(b_last - bcum)
            dec_last = jnp.exp(b_last)
            for h2 in range(2):
                hd = pair * 2 + h2
                vc = v_ref[pl.ds(r0, C), pl.ds(hd * GLA_DV, GLA_DV)].astype(BF16)
                st = state_ref[hd]
                qd = jnp.where(hmask[h2], q_dec, 0.0).astype(BF16)
                kd = jnp.where(hmask[h2], k_dec, 0.0).astype(BF16)
                o = _dot(qd, st.astype(BF16)) + _dot(scores[h2].astype(BF16), vc)
                dl_col = jnp.sum(jnp.where(_iota((LANES, LANES), 0) == _iota((LANES, LANES), 1),
                                           jnp.broadcast_to(dec_last, (LANES, LANES)), 0.0),
                                 axis=-1, keepdims=True)
                state_ref[hd] = st * dl_col + _dot_tn(kd, vc)
                os_ref[pl.ds(r0, C), pl.ds(hd * GLA_DV, GLA_DV)] = o
        return carry

    lax.fori_loop(0, nchunk, chunk_step, 0)

    for hd in range(GLA_HEADS):
        cols = pl.ds(hd * GLA_DV, GLA_DV)
        o = os_ref[:, cols]
        ms = jnp.mean(o * o, axis=-1, keepdims=True)
        y = o * lax.rsqrt(ms + RMS_EPS) * gn_ref[...]
        o_ref[:, cols] = (y * _silu(r_ref[:, cols])).astype(o_ref.dtype)


def gla_mixer(proj, gate_up, gate_bias, gnorm, *, batch, seq, rows):
    T = batch * seq
    nblk = seq // rows
    QW = GLA_HEADS * GLA_DK
    VW = GLA_HEADS * GLA_DV
    row = lambda b, i: b * nblk + i
    wg = jnp.zeros((LANES, QW), F32).at[8:8 + GLA_RANK].set(gate_up).astype(BF16)
    return pl.pallas_call(
        functools.partial(_gla_body, rows=rows),
        grid=(batch, nblk),
        in_specs=[pl.BlockSpec((rows, QW), lambda b, i: (row(b, i), COL_CQ // QW)),
                  pl.BlockSpec((rows, QW), lambda b, i: (row(b, i), COL_CK // QW)),
                  pl.BlockSpec((rows, VW), lambda b, i: (row(b, i), COL_CV // VW)),
                  pl.BlockSpec((rows, VW), lambda b, i: (row(b, i), COL_CR // VW)),
                  pl.BlockSpec((rows, LANES), lambda b, i: (row(b, i), COL_SMALL // LANES)),
                  pl.BlockSpec((LANES, QW), lambda b, i: (0, 0)),
                  pl.BlockSpec((1, QW), lambda b, i: (0, 0)),
                  pl.BlockSpec((1, GLA_DV), lambda b, i: (0, 0))],
        out_specs=pl.BlockSpec((rows, VW), lambda b, i: (row(b, i), 0)),
        out_shape=jax.ShapeDtypeStruct((T, VW), BF16),
        scratch_shapes=[pltpu.VMEM((GLA_HEADS, LANES, GLA_DV), F32),
                        pltpu.VMEM((rows, QW), F32), pltpu.VMEM((rows, QW), F32),
                        pltpu.VMEM((rows, QW), F32), pltpu.VMEM((rows, VW), F32)],
        compiler_params=_cparams("parallel", "arbitrary"),
        name="gla_mixer",
    )(proj, proj, proj, proj, proj, wg, gate_bias.reshape(1, QW), gnorm.reshape(1, GLA_DV))


def _mix_out_body(x_ref, g0_ref, g1_ref, g2_ref, gb_ref, ya_ref, yc_ref,
                  o0_ref, o1_ref, o2_ref, l0_ref, l1_ref, l2_ref,
                  wa_ref, wb_ref, wc_ref, wo_ref, out_ref):
    l0, l1, l2 = l0_ref[...], l1_ref[...], l2_ref[...]
    m = jnp.maximum(jnp.maximum(l0, l1), l2)
    e0, e1, e2 = jnp.exp(l0 - m), jnp.exp(l1 - m), jnp.exp(l2 - m)
    ob = (e0 * o0_ref[...] + e1 * o1_ref[...] + e2 * o2_ref[...]) / (e0 + e1 + e2)
    gb = gb_ref[...]
    y = _sigmoid(g0_ref[...] + gb[0:1]) * _dot(ya_ref[...], wa_ref[...])
    y = y + _sigmoid(g1_ref[...] + gb[1:2]) * _dot(ob.astype(BF16), wb_ref[...])
    y = y + _sigmoid(g2_ref[...] + gb[2:3]) * _dot(yc_ref[...], wc_ref[...])
    out_ref[...] = x_ref[...] + _dot(y.astype(BF16), wo_ref[...])


def mix_out(x, proj, gate_bias, ya, yc, o_grp, lse_grp, wa, wb, wc, wo, *, tm):
    T, D = x.shape
    W = SWA_HEADS * SWA_DH
    rowblk = lambda w: pl.BlockSpec((tm, w), lambda i: (i, 0))
    full = lambda a: pl.BlockSpec(a.shape, lambda i: (0, 0))
    gate = lambda n: pl.BlockSpec((tm, D), lambda i: (i, COL_GATES // D + n))
    return pl.pallas_call(
        _mix_out_body,
        grid=(T // tm,),
        in_specs=[rowblk(D), gate(0), gate(1), gate(2), full(gate_bias), rowblk(ya.shape[1]),
                  rowblk(yc.shape[1]), rowblk(W), rowblk(W), rowblk(W), rowblk(W), rowblk(W), rowblk(W),
                  full(wa), full(wb), full(wc), full(wo)],
        out_specs=rowblk(D),
        out_shape=jax.ShapeDtypeStruct((T, D), F32),
        compiler_params=_cparams("parallel"),
        name="mix_out",
    )(x, proj, proj, proj, gate_bias, ya, yc, *o_grp, *lse_grp, wa, wb, wc, wo)


def _cross_attn_body(x_ref, gx_ref, wq_ref, kv_ref, gq_ref, gk_ref, wo_ref, out_ref):
    x = x_ref[...]
    ms = jnp.mean(x * x, axis=-1, keepdims=True)
    h = (x * lax.rsqrt(ms + RMS_EPS) * gx_ref[...]).astype(BF16)
    q = _dot(h, wq_ref[...])
    kv = kv_ref[...]
    KW = XA_HEADS * XA_DH
    outs = []
    for hd in range(XA_HEADS):
        qh = q[:, hd * XA_DH:(hd + 1) * XA_DH]
        kh = kv[:, hd * XA_DH:(hd + 1) * XA_DH]
        vh = kv[:, KW + hd * XA_DH:KW + (hd + 1) * XA_DH]
        qn = qh * lax.rsqrt(jnp.mean(qh * qh, axis=-1, keepdims=True) + RMS_EPS) * gq_ref[...]
        kn = kh * lax.rsqrt(jnp.mean(kh * kh, axis=-1, keepdims=True) + RMS_EPS) * gk_ref[...]
        s = _dot_nt(qn.astype(BF16), kn.astype(BF16)) * (XA_DH ** -0.5)
        m = jnp.max(s, axis=-1, keepdims=True)
        p = jnp.exp(s - m)
        l = jnp.sum(p, axis=-1, keepdims=True)
        outs.append((_dot(p.astype(BF16), vh.astype(BF16)) / l).astype(BF16))
    o = jnp.concatenate(outs, axis=-1)
    out_ref[...] = x + _dot(o, wo_ref[...])


def cross_attention(x, kv, gx, wq, gq, gk, wo, *, batch, seq, mem_len, tm):
    T, D = x.shape
    per_batch = seq // tm
    full = lambda a: pl.BlockSpec(a.shape, lambda i: (0, 0))
    gx, gq, gk = gx.reshape(1, D), gq.reshape(1, XA_DH), gk.reshape(1, XA_DH)
    return pl.pallas_call(
        _cross_attn_body,
        grid=(T // tm,),
        in_specs=[pl.BlockSpec((tm, D), lambda i: (i, 0)), full(gx), full(wq),
                  pl.BlockSpec((mem_len, kv.shape[1]), lambda i: (i // per_batch, 0)),
                  full(gq), full(gk), full(wo)],
        out_specs=pl.BlockSpec((tm, D), lambda i: (i, 0)),
        out_shape=jax.ShapeDtypeStruct((T, D), F32),
        compiler_params=_cparams("parallel"),
        name="cross_attention",
    )(x, gx, wq, kv, gq, gk, wo)


def _router_body(x_ref, gx_ref, wr_ref, br_ref, h_ref, idx_ref, gate_ref):
    x = x_ref[...]
    ms = jnp.mean(x * x, axis=-1, keepdims=True)
    h = x * lax.rsqrt(ms + RMS_EPS) * gx_ref[...]
    h_ref[...] = h
    lane = _iota((1, LANES), 1)
    logits = _dot(h, wr_ref[...], HIGHEST) + br_ref[...]
    logits = jnp.where(lane < N_EXPERTS, logits, -jnp.inf)
    idx_out = jnp.zeros(logits.shape, jnp.int32)
    val_out = jnp.full(logits.shape, -jnp.inf, F32)
    for k in range(TOP_K):
        m = jnp.max(logits, axis=-1, keepdims=True)
        sel = jnp.min(jnp.where(logits == m, lane, LANES), axis=-1, keepdims=True)
        idx_out = jnp.where(lane == k, sel, idx_out)
        val_out = jnp.where(lane == k, m, val_out)
        logits = jnp.where(lane == sel, -jnp.inf, logits)
    top = jnp.max(val_out, axis=-1, keepdims=True)
    e = jnp.exp(val_out - top)
    idx_ref[...] = idx_out
    gate_ref[...] = e / jnp.sum(e, axis=-1, keepdims=True)


def moe_router(x, gx, wr, br, *, tm):
    T, D = x.shape
    wr_p = jnp.zeros((D, LANES), F32).at[:, :N_EXPERTS].set(wr)
    br_p = jnp.zeros((1, LANES), F32).at[0, :N_EXPERTS].set(br)
    full = lambda a: pl.BlockSpec(a.shape, lambda i: (0, 0))
    gx = gx.reshape(1, D)
    return pl.pallas_call(
        _router_body,
        grid=(T // tm,),
        in_specs=[pl.BlockSpec((tm, D), lambda i: (i, 0)), full(gx), full(wr_p), full(br_p)],
        out_specs=[pl.BlockSpec((tm, D), lambda i: (i, 0)),
                   pl.BlockSpec((tm, LANES), lambda i: (i, 0)),
                   pl.BlockSpec((tm, LANES), lambda i: (i, 0))],
        out_shape=[jax.ShapeDtypeStruct((T, D), F32),
                   jax.ShapeDtypeStruct((T, LANES), jnp.int32),
                   jax.ShapeDtypeStruct((T, LANES), F32)],
        compiler_params=_cparams("parallel"),
        name="moe_router",
    )(x, gx, wr_p, br_p)


def _expert_body(be_ref, nused_ref, tok0_ref, tokn_ref, dstp_ref, dstc_ref, h_ref, wi_ref, bi_ref, wo_ref, bo_ref,
                 out_ref, xbuf, ybuf, wi16, wo16, gsem, ssem, *, dump_base):
    i = pl.program_id(0)
    nu = nused_ref[0]
    slot = i % 2

    @pl.when((i == 0) | (be_ref[i] != be_ref[jnp.maximum(i - 1, 0)]))
    def _():
        wi16[...] = wi_ref[...].astype(BF16)
        wo16[...] = wo_ref[...].astype(BF16)

    def gather(tok_ref, s, j):
        return pltpu.make_async_copy(h_ref.at[pl.ds(tok_ref[0, 0, j], 1)], xbuf.at[s, pl.ds(j, 1)], gsem.at[s])

    def scatter(dst_ref, s, j):
        return pltpu.make_async_copy(ybuf.at[s, pl.ds(j, 1)], out_ref.at[pl.ds(dst_ref[0, 0, j], 1)], ssem.at[s])

    def for_rows(fn):
        def body(j, carry):
            fn(j)
            return carry
        lax.fori_loop(0, MOE_BM, body, 0, unroll=8)

    def wait_gathered(s):
        pltpu.make_async_copy(xbuf.at[s], xbuf.at[s], gsem.at[s]).wait()

    def wait_scattered(s):
        pltpu.make_async_copy(ybuf.at[s], ybuf.at[s], ssem.at[s]).wait()

    @pl.when(i == 0)
    def _():
        for_rows(lambda j: gather(tok0_ref, 0, j).start())
        ybuf[...] = jnp.zeros_like(ybuf)
        pltpu.make_async_copy(ybuf.at[0], out_ref.at[pl.ds(dump_base, MOE_BM)], ssem.at[0]).start()

    def step(s):
        wait_gathered(s)
        for j in range(MOE_BM):
            gather(tokn_ref, 1 - s, j).start()
            scatter(dstp_ref, 1 - s, j).start()
        hh = _dot(xbuf[s].astype(BF16), wi16[...]) + bi_ref[...]
        glu = jnp.minimum(hh[:, :D_EXPERT], SWIGLU_LIMIT)
        lin = jnp.clip(hh[:, D_EXPERT:], -SWIGLU_LIMIT, SWIGLU_LIMIT)
        act = glu * _sigmoid(SWIGLU_ALPHA * glu) * (lin + 1.0)
        y = _dot(act.astype(BF16), wo16[...]) + bo_ref[...]
        wait_scattered(s)
        ybuf[s] = y

        @pl.when(i == nu - 1)
        def _():
            for_rows(lambda j: scatter(dstc_ref, s, j).start())
            wait_scattered(s)
            wait_scattered(1 - s)
            wait_gathered(1 - s)

    for s in range(2):
        pl.when((i < nu) & (slot == s))(functools.partial(step, s))


def expert_ffn(h, tok_buf, slot_dst, block_expert, n_used, w_in, b_in, w_out, b_out, *, layer):
    T, D = h.shape
    P = tok_buf.shape[0]
    nb = P // MOE_BM
    L, E, _, F2 = w_in.shape
    dump_base = TOP_K * T
    dst_ext = jnp.concatenate([dump_base + MOE_BM + jnp.arange(MOE_BM, dtype=jnp.int32), slot_dst])
    dst_ext = dst_ext.reshape(nb + 1, 1, MOE_BM)
    idx_blk = lambda f: pl.BlockSpec((1, 1, MOE_BM), f, memory_space=pltpu.SMEM)
    grid_spec = pltpu.PrefetchScalarGridSpec(
        num_scalar_prefetch=2,
        grid=(nb,),
        in_specs=[idx_blk(lambda i, be, nu: (i, 0, 0)),
                  idx_blk(lambda i, be, nu: (jnp.minimum(i + 1, nb - 1), 0, 0)),
                  idx_blk(lambda i, be, nu: (i, 0, 0)),
                  idx_blk(lambda i, be, nu: (i + 1, 0, 0)),
                  pl.BlockSpec(memory_space=pl.ANY),
                  pl.BlockSpec((None, None, D, F2), lambda i, be, nu: (layer, be[i], 0, 0)),
                  pl.BlockSpec((None, None, 1, F2), lambda i, be, nu: (layer, be[i], 0, 0)),
                  pl.BlockSpec((None, None, F2 // 2, D), lambda i, be, nu: (layer, be[i], 0, 0)),
                  pl.BlockSpec((None, None, 1, D), lambda i, be, nu: (layer, be[i], 0, 0))],
        out_specs=pl.BlockSpec(memory_space=pl.ANY),
        scratch_shapes=[pltpu.VMEM((2, MOE_BM, D), F32), pltpu.VMEM((2, MOE_BM, D), F32),
                        pltpu.VMEM((D, F2), BF16), pltpu.VMEM((F2 // 2, D), BF16),
                        pltpu.SemaphoreType.DMA((2,)), pltpu.SemaphoreType.DMA((2,))],
    )
    tok3 = tok_buf.reshape(nb, 1, MOE_BM)
    return pl.pallas_call(
        functools.partial(_expert_body, dump_base=dump_base),
        grid_spec=grid_spec,
        out_shape=jax.ShapeDtypeStruct((dump_base + 2 * MOE_BM, D), F32),
        compiler_params=_cparams("arbitrary"),
        name="expert_ffn",
    )(block_expert, n_used, tok3, tok3, dst_ext, dst_ext, h,
      w_in, b_in.reshape(L, E, 1, F2), w_out, b_out.reshape(L, E, 1, D))


def _moe_combine_body(x_ref, y0_ref, y1_ref, y2_ref, y3_ref, gate_ref, out_ref):
    g = gate_ref[...]
    acc = x_ref[...]
    for k, y_ref in enumerate((y0_ref, y1_ref, y2_ref, y3_ref)):
        acc = acc + g[:, k:k + 1] * y_ref[...]
    out_ref[...] = acc


def moe_combine(x, y4, gates, *, tm):
    T, D = x.shape
    nblk = T // tm
    ysp = lambda k: pl.BlockSpec((tm, D), lambda i: (k * nblk + i, 0))
    return pl.pallas_call(
        _moe_combine_body,
        grid=(nblk,),
        in_specs=[pl.BlockSpec((tm, D), lambda i: (i, 0)), ysp(0), ysp(1), ysp(2), ysp(3),
                  pl.BlockSpec((tm, LANES), lambda i: (i, 0))],
        out_specs=pl.BlockSpec((tm, D), lambda i: (i, 0)),
        out_shape=jax.ShapeDtypeStruct((T, D), F32),
        compiler_params=_cparams("parallel"),
        name="moe_combine",
    )(x, y4, y4, y4, y4, gates)


def _pack_w_in(w_in):
    L, D, _ = w_in.shape
    o_alpha = 2048
    o_b = 2056
    o_c = o_b + 2304
    o_low = o_c + 1536
    o_gates = o_low + GLA_RANK
    c = w_in[:, :, o_c:o_low]
    parts = [w_in[:, :, o_gates:],
             w_in[:, :, 0:2048],
             w_in[:, :, o_b:o_c],
             w_in[:, :, o_alpha:o_b], w_in[:, :, o_low:o_gates],
             jnp.zeros((L, D, 256 - 8 - GLA_RANK), w_in.dtype),
             c[:, :, 512:1024], c[:, :, 1024:1536], c[:, :, 0:256], c[:, :, 256:512]]
    packed = jnp.concatenate(parts, axis=-1)
    assert packed.shape[-1] == N_PACKED
    return packed.astype(BF16)


def _to_residue_major(t, dilation):
    if dilation == 1:
        return t
    *lead, S, W = t.shape
    return jnp.swapaxes(t.reshape(*lead, S // dilation, dilation, W), -2, -3).reshape(*lead, S, W)


def _from_residue_major(t, dilation):
    if dilation == 1:
        return t
    *lead, S, W = t.shape
    return jnp.swapaxes(t.reshape(*lead, dilation, S // dilation, W), -2, -3).reshape(*lead, S, W)


def _routing_tables(idx, n_tokens):
    A = n_tokens * TOP_K
    P = A + N_EXPERTS * MOE_BM
    e_flat = idx.reshape(A)
    onehot = (e_flat[:, None] == jnp.arange(N_EXPERTS, dtype=jnp.int32)[None, :]).astype(jnp.int32)
    csum = jnp.cumsum(onehot, axis=0)
    rank = jnp.sum(onehot * csum, axis=1) - 1
    counts = csum[-1]
    padded = (counts + MOE_BM - 1) // MOE_BM * MOE_BM
    pad_ends = jnp.cumsum(padded)
    pad_starts = pad_ends - padded
    dest = (pad_starts[e_flat] + rank).astype(jnp.int32)
    slot_a = jnp.full((P,), -1, jnp.int32).at[dest].set(jnp.arange(A, dtype=jnp.int32))
    tok_buf = jnp.maximum(slot_a, 0) // TOP_K
    p = jnp.arange(P, dtype=jnp.int32)
    dump = TOP_K * n_tokens + ((p // MOE_BM) % 2) * MOE_BM + p % MOE_BM
    slot_dst = jnp.where(slot_a < 0, dump, (slot_a % TOP_K) * n_tokens + slot_a // TOP_K)
    block_start = jnp.arange(P // MOE_BM, dtype=jnp.int32) * MOE_BM
    block_expert = jnp.minimum(jnp.sum((pad_ends[None, :] <= block_start[:, None]).astype(jnp.int32), axis=1),
                               N_EXPERTS - 1)
    n_used = (pad_ends[-1:] // MOE_BM).astype(jnp.int32)
    return tok_buf, slot_dst, block_expert, n_used


def kernel(x, mem, positions, norm_mix, w_in, gate_bias, gdn_conv, gdn_a_log, gdn_dt_bias, gdn_norm, swa_q_norm, swa_k_norm, gla_gate_up, gla_gate_bias, gla_norm, w_branch_a, w_branch_b, w_branch_c, w_mix_out, norm_cross, norm_mem, xa_wq, xa_wkv, xa_q_norm, xa_k_norm, xa_wo, norm_ffn, router_w, router_b, moe_w_in, moe_b_in, moe_w_out, moe_b_out):
    B, S, D = x.shape
    T = B * S
    M = mem.shape[1]
    depth = w_in.shape[0]
    W = SWA_HEADS * SWA_DH

    inv_freq = ROPE_THETA ** (-jnp.arange(0, SWA_DH, 2, dtype=F32) / SWA_DH)
    ang = positions.astype(F32).reshape(T, 1) * inv_freq[None, :]
    cos, sin = jnp.cos(ang), jnp.sin(ang)
    cosf = jnp.tile(jnp.concatenate([cos, cos], axis=-1), (1, SWA_HEADS))
    sins = jnp.tile(jnp.concatenate([-sin, sin], axis=-1), (1, SWA_HEADS))

    w_in_p = _pack_w_in(w_in)
    xf = x.reshape(T, D)
    memf = mem.reshape(B * M, D)

    for l in range(depth):
        proj = norm_matmul(xf, norm_mix[l], w_in_p[l], tm=1024, tn=1536, name="in_proj")
        ya = gdn_mixer(proj, gdn_conv[l], gdn_a_log[l], gdn_dt_bias[l], gdn_norm[l], batch=B, seq=S, rows=512)
        qn, kn, vn = qk_norm_rope(proj, cosf, sins, swa_q_norm[l], swa_k_norm[l], tm=1024)
        o_grp, lse_grp = [], []
        for gi, (window, dil) in enumerate(SWA_GROUPS):
            assert window // dil == SWA_BACK
            rm = lambda t: _to_residue_major(t.reshape(B, S, W), dil)
            o_g, lse_g = band_attention(rm(qn[gi]), rm(kn[gi]), rm(vn[gi]), dilation=dil)
            o_grp.append(_from_residue_major(o_g, dil).reshape(T, W))
            lse_grp.append(_from_residue_major(lse_g, dil).reshape(T, W))
        yc = gla_mixer(proj, gla_gate_up[l], gla_gate_bias[l], gla_norm[l], batch=B, seq=S, rows=512)
        xf = mix_out(xf, proj, gate_bias[l], ya, yc, o_grp, lse_grp,
                     w_branch_a[l].astype(BF16), w_branch_b[l].astype(BF16), w_branch_c[l].astype(BF16),
                     w_mix_out[l].astype(BF16), tm=512)

        kv = norm_matmul(memf, norm_mem[l], xa_wkv[l].astype(BF16), tm=min(1024, B * M), tn=1024, name="mem_kv")
        xf = cross_attention(xf, kv, norm_cross[l], xa_wq[l].astype(BF16), xa_q_norm[l], xa_k_norm[l],
                             xa_wo[l].astype(BF16), batch=B, seq=S, mem_len=M, tm=512)

        h, idx, gates = moe_router(xf, norm_ffn[l], router_w[l], router_b[l], tm=1024)
        tok_buf, slot_dst, block_expert, n_used = _routing_tables(idx[:, :TOP_K], T)
        y4 = expert_ffn(h, tok_buf, slot_dst, block_expert, n_used, moe_w_in, moe_b_in, moe_w_out, moe_b_out,
                        layer=l)
        xf = moe_combine(xf, y4, gates, tm=512)

    return xf.reshape(B, S, D)
```

```python
import functools

import jax
import jax.numpy as jnp
from jax import lax
from jax.experimental import pallas as pl
from jax.experimental.pallas import tpu as pltpu

F32 = jnp.float32
BF16 = jnp.bfloat16
HIGHEST = lax.Precision.HIGHEST

RMS_EPS = 1e-6
L2_EPS = 1e-6
LANES = 128
VMEM_LIMIT = 56 * 1024 * 1024

D_MODEL = 1024
GDN_HEADS, GDN_D, GDN_CONV, CHUNK = 4, 128, 4, 64
SWA_GROUPS = ((128, 1), (512, 4), (2048, 16))
SWA_HEADS, SWA_DH, SWA_BACK = 4, 64, 128
ROPE_THETA = 10000.0
GLA_HEADS, GLA_DK, GLA_DV, GLA_RANK, GLA_TAU = 4, 64, 128, 16, 16.0
GLA_SUB = 8
XA_HEADS, XA_DH = 4, 128
N_EXPERTS, TOP_K, D_EXPERT = 32, 4, 1024
SWIGLU_ALPHA, SWIGLU_LIMIT = 1.702, 7.0
MOE_BM = 256

COL_GATES = 0
COL_AQ, COL_AK, COL_AV, COL_AZ = 3072, 3584, 4096, 4608
COL_BQ, COL_BK, COL_BV = 5120, 5888, 6656
COL_SMALL = 7424
COL_CV, COL_CR, COL_CQ, COL_CK = 7680, 8192, 8704, 8960
N_PACKED = 9216


def _cparams(*sem):
    return pltpu.CompilerParams(dimension_semantics=sem, vmem_limit_bytes=VMEM_LIMIT)


def _sigmoid(x):
    return 1.0 / (1.0 + jnp.exp(-x))


def _silu(x):
    return x * _sigmoid(x)


def _softplus(x):
    return jnp.maximum(x, 0.0) + jnp.log(1.0 + jnp.exp(-jnp.abs(x)))


def _dot(a, b, precision=None):
    return jnp.dot(a, b, preferred_element_type=F32, precision=precision)


def _dot_nt(a, b, precision=None):
    return lax.dot_general(a, b, (((1,), (1,)), ((), ())), preferred_element_type=F32, precision=precision)


def _dot_tn(a, b, precision=None):
    return lax.dot_general(a, b, (((0,), (0,)), ((), ())), preferred_element_type=F32, precision=precision)


def _bdot(a, b):
    return lax.dot_general(a, b, (((2,), (1,)), ((0,), (0,))), preferred_element_type=F32)


def _bdot_nt(a, b):
    return lax.dot_general(a, b, (((2,), (2,)), ((0,), (0,))), preferred_element_type=F32)


def _bdot3(a, b):
    ah = a.astype(BF16)
    bh = b.astype(BF16)
    al = (a - ah.astype(F32)).astype(BF16)
    bl = (b - bh.astype(F32)).astype(BF16)
    return _bdot(ah, bh) + (_bdot(ah, bl) + _bdot(al, bh))


def _iota(shape, axis):
    return lax.broadcasted_iota(jnp.int32, shape, axis)


def _norm_matmul_body(x_ref, g_ref, w_ref, o_ref, h_ref):
    @pl.when(pl.program_id(1) == 0)
    def _():
        x = x_ref[...]
        ms = jnp.mean(x * x, axis=-1, keepdims=True)
        h_ref[...] = (x * lax.rsqrt(ms + RMS_EPS) * g_ref[...]).astype(h_ref.dtype)

    o_ref[...] = _dot(h_ref[...], w_ref[...]).astype(o_ref.dtype)


def norm_matmul(x, gain, w, *, tm, tn, name):
    T, D = x.shape
    N = w.shape[1]
    return pl.pallas_call(
        _norm_matmul_body,
        grid=(T // tm, N // tn),
        in_specs=[pl.BlockSpec((tm, D), lambda i, j: (i, 0)),
                  pl.BlockSpec((1, D), lambda i, j: (0, 0)),
                  pl.BlockSpec((D, tn), lambda i, j: (0, j))],
        out_specs=pl.BlockSpec((tm, tn), lambda i, j: (i, j)),
        out_shape=jax.ShapeDtypeStruct((T, N), F32),
        scratch_shapes=[pltpu.VMEM((tm, D), BF16)],
        compiler_params=_cparams("parallel", "arbitrary"),
        name=name,
    )(x, gain.reshape(1, D), w)


def _gdn_body(q_ref, k_ref, v_ref, z_ref, sm_ref, smt_ref, cq_ref, ck_ref, cv_ref, alog_ref, dtb_ref, gn_ref,
              o_ref, state_ref, tail_ref, u_ref, wq_ref, kd_ref, in_ref, dl_ref, os_ref, *, rows):
    blk = pl.program_id(1)
    nchunk = rows // CHUNK

    @pl.when(blk == 0)
    def _():
        state_ref[...] = jnp.zeros_like(state_ref)
        tail_ref[...] = jnp.zeros_like(tail_ref)

    def conv_silu(x_ref, w_ref, slot):
        x = x_ref[...]
        xp = jnp.concatenate([tail_ref[slot], x], axis=0)
        w = w_ref[...]
        acc = x * w[GDN_CONV - 1:GDN_CONV, :]
        for s in range(1, GDN_CONV):
            acc = acc + pltpu.roll(xp, s, 0)[8:] * w[GDN_CONV - 1 - s:GDN_CONV - s, :]
        tail_ref[slot] = x[rows - 8:rows]
        return _silu(acc)

    q_all = conv_silu(q_ref, cq_ref, 0)
    k_all = conv_silu(k_ref, ck_ref, 1)
    v_all = conv_silu(v_ref, cv_ref, 2)
    sm = sm_ref[...]
    pos_c = _iota((rows, 1), 0) % CHUNK
    pos_r = _iota((1, rows), 1) % CHUNK
    ri = _iota((1, CHUNK, CHUNK), 1)
    ci = _iota((1, CHUNK, CHUNK), 2)
    incl = ri >= ci
    strict = ri > ci
    eye = (ri == ci).astype(F32)
    chunked = lambda t: t.reshape(nchunk, CHUNK, t.shape[-1])

    for hd in range(GDN_HEADS):
        _gdn_chunk_local(hd, q_all, k_all, v_all, sm, smt_ref, alog_ref, dtb_ref, pos_c, pos_r, incl, strict, eye,
                         chunked, u_ref, wq_ref, kd_ref, in_ref, dl_ref, rows=rows)

    def chunk_step(c, carry):
        for hd in range(GDN_HEADS):
            st = state_ref[hd]
            wq_s = _dot(wq_ref[hd, c], st.astype(BF16))
            v16 = (u_ref[hd, c] - wq_s[:CHUNK]).astype(BF16)
            o = wq_s[CHUNK:] + _dot(in_ref[hd, c], v16)
            state_ref[hd] = st * dl_ref[hd, c][:, :1] + _dot_tn(kd_ref[hd, c], v16)
            os_ref[pl.ds(pl.multiple_of(c * CHUNK, CHUNK), CHUNK), pl.ds(hd * GDN_D, GDN_D)] = o
        return carry

    lax.fori_loop(0, nchunk, chunk_step, 0, unroll=True)

    for hd in range(GDN_HEADS):
        cols = pl.ds(hd * GDN_D, GDN_D)
        o = os_ref[:, cols]
        ms = jnp.mean(o * o, axis=-1, keepdims=True)
        y = o * lax.rsqrt(ms + RMS_EPS) * gn_ref[...]
        o_ref[:, cols] = (y * _silu(z_ref[:, cols])).astype(o_ref.dtype)


def _gdn_chunk_local(hd, q_all, k_all, v_all, sm, smt_ref, alog_ref, dtb_ref, pos_c, pos_r, incl, strict, eye,
                     chunked, u_ref, wq_ref, kd_ref, in_ref, dl_ref, *, rows):
    nchunk = rows // CHUNK
    cols = slice(hd * GDN_D, (hd + 1) * GDN_D)
    q, k, v = q_all[:, cols], k_all[:, cols], v_all[:, cols]
    q = q * lax.rsqrt(jnp.sum(q * q, axis=-1, keepdims=True) + L2_EPS) * (GDN_D ** -0.5)
    k = k * lax.rsqrt(jnp.sum(k * k, axis=-1, keepdims=True) + L2_EPS)

    neg_a = -jnp.exp(alog_ref[:, hd:hd + 1])
    dt_b = dtb_ref[:, hd:hd + 1]
    g_col = jnp.broadcast_to(neg_a * _softplus(sm[:, hd:hd + 1] + dt_b), (rows, LANES))
    beta = jnp.broadcast_to(_sigmoid(sm[:, GDN_HEADS + hd:GDN_HEADS + hd + 1]), (rows, LANES))
    g_row = neg_a * _softplus(smt_ref[hd:hd + 1, :] + dt_b)

    step = 1
    while step < CHUNK:
        g_col = g_col + jnp.where(pos_c >= step, pltpu.roll(g_col, step, 0), 0.0)
        g_row = g_row + jnp.where(pos_r >= step, pltpu.roll(g_row, step, 1), 0.0)
        step *= 2
    eg = jnp.exp(g_col)

    q3, k3, g3 = chunked(q), chunked(k), chunked(g_col)
    kb3 = chunked(k * beta)
    eg3 = chunked(eg)
    g_row3 = jnp.stack([g_row[:, c * CHUNK:(c + 1) * CHUNK] for c in range(nchunk)], axis=0)
    decay = jnp.where(incl, jnp.exp(jnp.minimum(g3[:, :, :CHUNK] - g_row3, 0.0)), 0.0)
    kq = jnp.concatenate([kb3, q3], axis=1).astype(BF16)
    s = _bdot_nt(kq, k3.astype(BF16))
    lower = jnp.where(strict, s[:, :CHUNK] * decay, 0.0)
    intra = jnp.where(incl, s[:, CHUNK:] * decay, 0.0)
    pw = -lower
    inv = eye + pw
    pw = _bdot3(pw, pw)
    for _ in range(4):
        both = _bdot3(jnp.concatenate([inv, pw], axis=1), pw)
        inv = inv + both[:, :CHUNK]
        pw = both[:, CHUNK:]
    inv = inv + _bdot3(inv, pw)
    uw = _bdot3(inv, jnp.concatenate([chunked(v * beta), kb3 * eg3], axis=2))
    g_last = g3[:, CHUNK - 1:CHUNK, :]
    u_ref[hd] = uw[:, :, :GDN_D]
    wq_ref[hd] = jnp.concatenate([uw[:, :, GDN_D:], q3 * eg3], axis=1).astype(BF16)
    kd_ref[hd] = (k3 * jnp.exp(g_last - g3)).astype(BF16)
    in_ref[hd] = intra.astype(BF16)
    dl_ref[hd] = jnp.exp(g_last)


def gdn_mixer(proj, conv_w, a_log, dt_bias, gnorm, *, batch, seq, rows):
    T = batch * seq
    nblk = seq // rows
    nchunk = rows // CHUNK
    H = GDN_HEADS
    HW = H * GDN_D
    small_t = proj[:, COL_SMALL:COL_SMALL + 2 * H].T
    cb = lambda base: pl.BlockSpec((rows, HW), lambda b, i: (b * nblk + i, base // HW))
    pad = lambda v: jnp.zeros((1, LANES), F32).at[0, :H].set(v)
    wspec = lambda n: pl.BlockSpec((GDN_CONV, HW), lambda b, i: (0, n))
    vec = pl.BlockSpec((1, LANES), lambda b, i: (0, 0))
    return pl.pallas_call(
        functools.partial(_gdn_body, rows=rows),
        grid=(batch, nblk),
        in_specs=[cb(COL_AQ), cb(COL_AK), cb(COL_AV), cb(COL_AZ),
                  pl.BlockSpec((rows, LANES), lambda b, i: (b * nblk + i, COL_SMALL // LANES)),
                  pl.BlockSpec((2 * H, rows), lambda b, i: (0, b * nblk + i)),
                  wspec(0), wspec(1), wspec(2), vec, vec, vec],
        out_specs=pl.BlockSpec((rows, HW), lambda b, i: (b * nblk + i, 0)),
        out_shape=jax.ShapeDtypeStruct((T, HW), BF16),
        scratch_shapes=[pltpu.VMEM((H, GDN_D, GDN_D), F32),
                        pltpu.VMEM((3, 8, HW), F32),
                        pltpu.VMEM((H, nchunk, CHUNK, GDN_D), F32),
                        pltpu.VMEM((H, nchunk, 2 * CHUNK, GDN_D), BF16),
                        pltpu.VMEM((H, nchunk, CHUNK, GDN_D), BF16),
                        pltpu.VMEM((H, nchunk, CHUNK, CHUNK), BF16),
                        pltpu.VMEM((H, nchunk, 1, LANES), F32),
                        pltpu.VMEM((rows, HW), F32)],
        compiler_params=_cparams("parallel", "arbitrary"),
        name="gdn_mixer",
    )(proj, proj, proj, proj, proj, small_t, conv_w, conv_w, conv_w, pad(a_log), pad(dt_bias),
      gnorm.reshape(1, GDN_D))


def _qk_rope_body(q_ref, k_ref, v_ref, cos_ref, sin_ref, gq_ref, gk_ref, gm_ref, qo_ref, ko_ref, vo_ref):
    lane = _iota((1, 2 * LANES), 1)
    first_half = (lane % SWA_DH) < (SWA_DH // 2)
    cosf = cos_ref[...]
    sins = sin_ref[...]
    vo_ref[...] = v_ref[...].astype(vo_ref.dtype)

    def norm_rope(x, gain):
        sq = x * x
        hi = sq.astype(BF16)
        lo = (sq - hi.astype(F32)).astype(BF16)
        ms = _dot(hi, gm_ref[...]) + _dot(lo, gm_ref[...])
        xn = x * lax.rsqrt(ms + RMS_EPS) * gain
        other = jnp.where(first_half, pltpu.roll(xn, 2 * LANES - SWA_DH // 2, 1),
                          pltpu.roll(xn, SWA_DH // 2, 1))
        return xn * cosf + other * sins

    qo_ref[...] = (norm_rope(q_ref[...], gq_ref[...]) * (SWA_DH ** -0.5)).astype(qo_ref.dtype)
    ko_ref[...] = norm_rope(k_ref[...], gk_ref[...]).astype(ko_ref.dtype)


def qk_norm_rope(proj, cosf, sins, q_gain, k_gain, *, tm):
    T = proj.shape[0]
    W = SWA_HEADS * SWA_DH
    grp = jnp.arange(W) // SWA_DH
    gmean = ((grp[:, None] == grp[None, :]).astype(F32) / SWA_DH).astype(BF16)
    tile = lambda v: jnp.tile(v, SWA_HEADS).reshape(1, W)
    vec = pl.BlockSpec((1, W), lambda g, i: (0, 0))
    out = jax.ShapeDtypeStruct((len(SWA_GROUPS), T, W), BF16)
    ospec = pl.BlockSpec((None, tm, W), lambda g, i: (g, i, 0))
    return pl.pallas_call(
        _qk_rope_body,
        grid=(len(SWA_GROUPS), T // tm),
        in_specs=[pl.BlockSpec((tm, W), lambda g, i: (i, COL_BQ // W + g)),
                  pl.BlockSpec((tm, W), lambda g, i: (i, COL_BK // W + g)),
                  pl.BlockSpec((tm, W), lambda g, i: (i, COL_BV // W + g)),
                  pl.BlockSpec((tm, W), lambda g, i: (i, 0)),
                  pl.BlockSpec((tm, W), lambda g, i: (i, 0)),
                  vec, vec, pl.BlockSpec((W, W), lambda g, i: (0, 0))],
        out_specs=[ospec, ospec, ospec],
        out_shape=[out, out, out],
        compiler_params=_cparams("parallel", "parallel"),
        name="qk_norm_rope",
    )(proj, proj, proj, cosf, sins, tile(q_gain), tile(k_gain), gmean)


def _band_attn_body(q_ref, kp_ref, kc_ref, vp_ref, vc_ref, o_ref, lse_ref, *, tiles_per_residue):
    i = pl.program_id(1)
    first = (i % tiles_per_residue) == 0
    c = SWA_BACK
    rows = q_ref.shape[0]
    W = SWA_HEADS * SWA_DH
    q = q_ref[...]
    kcat = jnp.concatenate([kp_ref[...], kc_ref[...]], axis=0)
    vcat = jnp.concatenate([vp_ref[...], vc_ref[...]], axis=0)
    a = _iota((rows, c + rows), 0)
    b = _iota((rows, c + rows), 1)
    dist = a + c - b
    valid = (dist >= 0) & (dist <= SWA_BACK) & ((b >= c) | jnp.logical_not(first))
    lane = _iota((1, W), 1)
    o_acc = jnp.zeros((rows, W), F32)
    lse_acc = jnp.zeros((rows, W), F32)
    for hd in range(SWA_HEADS):
        hmask = (lane // SWA_DH) == hd
        qh = jnp.where(hmask, q, jnp.zeros_like(q))
        s = jnp.where(valid, _dot_nt(qh, kcat), -jnp.inf)
        m = jnp.max(s, axis=-1, keepdims=True)
        p = jnp.exp(s - m)
        l = jnp.sum(p, axis=-1, keepdims=True)
        pv = _dot(p.astype(BF16), vcat) / l
        o_acc = jnp.where(hmask, pv, o_acc)
        lse_acc = jnp.where(hmask, m + jnp.log(l), lse_acc)
    o_ref[...] = o_acc
    lse_ref[...] = lse_acc


def band_attention(q, k, v, *, dilation):
    B, S, W = q.shape
    c = SWA_BACK
    rows = 2 * c
    assert (S // dilation) % rows == 0
    ntile = S // rows
    cur = pl.BlockSpec((None, rows, W), lambda b, i: (b, i, 0))
    prev = pl.BlockSpec((None, c, W), lambda b, i: (b, jnp.maximum(i * (rows // c) - 1, 0), 0))
    out = jax.ShapeDtypeStruct((B, S, W), F32)
    return pl.pallas_call(
        functools.partial(_band_attn_body, tiles_per_residue=ntile // dilation),
        grid=(B, ntile),
        in_specs=[cur, prev, cur, prev, cur],
        out_specs=[cur, cur],
        out_shape=[out, out],
        compiler_params=_cparams("parallel", "parallel"),
        name=f"band_attention_d{dilation}",
    )(q, k, k, v, v)


def _gla_body(q_ref, k_ref, v_ref, r_ref, sm_ref, wg_ref, bg_ref, gn_ref, o_ref,
              state_ref, qs_ref, ks_ref, la_ref, os_ref, *, rows):
    blk = pl.program_id(1)
    nchunk = rows // CHUNK
    C = CHUNK

    @pl.when(blk == 0)
    def _():
        state_ref[...] = jnp.zeros_like(state_ref)

    x = _dot(sm_ref[...].astype(BF16), wg_ref[...]) + bg_ref[...]
    la_ref[...] = (jnp.minimum(x, 0.0) - jnp.log(1.0 + jnp.exp(-jnp.abs(x)))) * (1.0 / GLA_TAU)
    qs_ref[...] = q_ref[...] * (GLA_DK ** -0.5)
    ks_ref[...] = k_ref[...]

    ri = _iota((C, C), 0)
    ci = _iota((C, C), 1)
    tril = (ri >= ci).astype(F32)
    lane = _iota((1, LANES), 1)
    lrow = _iota((LANES, C), 0)
    gsum = [((lrow // GLA_DK) == h2).astype(BF16) for h2 in range(2)]
    hmask = [(lane // GLA_DK) == h2 for h2 in range(2)]
    nsub = C // GLA_SUB

    def chunk_step(c, carry):
        r0 = pl.multiple_of(c * C, C)
        for pair in range(GLA_HEADS // 2):
            cols = pl.ds(pair * LANES, LANES)
            qc = qs_ref[pl.ds(r0, C), cols]
            kc = ks_ref[pl.ds(r0, C), cols]
            la = la_ref[pl.ds(r0, C), cols]
            bcum = _dot(tril, la, HIGHEST)
            b_last = bcum[C - 1:C, :]
            s_rows = [[jnp.zeros((GLA_SUB, C), F32)] for _ in range(2)]
            for sb in range(1, nsub):
                lo = sb * GLA_SUB
                bref = bcum[lo - 1:lo, :]
                q_sb = (qc[lo:lo + GLA_SUB] * jnp.exp(bcum[lo:lo + GLA_SUB] - bref))
                k_sb = (kc * jnp.exp(jnp.minimum(bref - bcum, 0.0))).astype(BF16)
                for h2 in range(2):
                    qm = jnp.where(hmask[h2], q_sb, 0.0).astype(BF16)
                    s_rows[h2].append(_dot_nt(qm, k_sb))
            scores = []
            for h2 in range(2):
                s_off = jnp.concatenate(s_rows[h2], axis=0)
                scores.append(jnp.where((ri // GLA_SUB) > (ci // GLA_SUB), s_off, 0.0))
            for off in range(GLA_SUB):
                if off == 0:
                    prod = qc * kc
                else:
                    k_sh = pltpu.roll(kc, off, 0)
                    b_sh = pltpu.roll(bcum, off, 0)
                    prod = qc * k_sh * jnp.exp(jnp.minimum(bcum - b_sh, 0.0))
                p16 = prod.astype(BF16)
                on_diag = ((ri - ci) == off) & ((ri // GLA_SUB) == (ci // GLA_SUB))
                for h2 in range(2):
                    d = _dot(p16, gsum[h2])
                    scores[h2] = jnp.where(on_diag, d, scores[h2])
            q_dec = qc * jnp.exp(bcum)
            k_dec = kc * jnp.exp(b_last - bcum)
            dec_last = jnp.exp(b_last)
            for h2 in range(2):
                hd = pair * 2 + h2
                vc = v_ref[pl.ds(r0, C), pl.ds(hd * GLA_DV, GLA_DV)].astype(BF16)
                st = state_ref[hd]
                qd = jnp.where(hmask[h2], q_dec, 0.0).astype(BF16)
                kd = jnp.where(hmask[h2], k_dec, 0.0).astype(BF16)
                o = _dot(qd, st.astype(BF16)) + _dot(scores[h2].astype(BF16), vc)
                dl_col = jnp.sum(jnp.where(_iota((LANES, LANES), 0) == _iota((LANES, LANES), 1),
                                           jnp.broadcast_to(dec_last, (LANES, LANES)), 0.0),
                                 axis=-1, keepdims=True)
                state_ref[hd] = st * dl_col + _dot_tn(kd, vc)
                os_ref[pl.ds(r0, C), pl.ds(hd * GLA_DV, GLA_DV)] = o
        return carry

    lax.fori_loop(0, nchunk, chunk_step, 0, unroll=2)

    for hd in range(GLA_HEADS):
        cols = pl.ds(hd * GLA_DV, GLA_DV)
        o = os_ref[:, cols]
        ms = jnp.mean(o * o, axis=-1, keepdims=True)
        y = o * lax.rsqrt(ms + RMS_EPS) * gn_ref[...]
        o_ref[:, cols] = (y * _silu(r_ref[:, cols])).astype(o_ref.dtype)


def gla_mixer(proj, gate_up, gate_bias, gnorm, *, batch, seq, rows):
    T = batch * seq
    nblk = seq // rows
    QW = GLA_HEADS * GLA_DK
    VW = GLA_HEADS * GLA_DV
    row = lambda b, i: b * nblk + i
    wg = jnp.zeros((LANES, QW), F32).at[8:8 + GLA_RANK].set(gate_up).astype(BF16)
    return pl.pallas_call(
        functools.partial(_gla_body, rows=rows),
        grid=(batch, nblk),
        in_specs=[pl.BlockSpec((rows, QW), lambda b, i: (row(b, i), COL_CQ // QW)),
                  pl.BlockSpec((rows, QW), lambda b, i: (row(b, i), COL_CK // QW)),
                  pl.BlockSpec((rows, VW), lambda b, i: (row(b, i), COL_CV // VW)),
                  pl.BlockSpec((rows, VW), lambda b, i: (row(b, i), COL_CR // VW)),
                  pl.BlockSpec((rows, LANES), lambda b, i: (row(b, i), COL_SMALL // LANES)),
                  pl.BlockSpec((LANES, QW), lambda b, i: (0, 0)),
                  pl.BlockSpec((1, QW), lambda b, i: (0, 0)),
                  pl.BlockSpec((1, GLA_DV), lambda b, i: (0, 0))],
        out_specs=pl.BlockSpec((rows, VW), lambda b, i: (row(b, i), 0)),
        out_shape=jax.ShapeDtypeStruct((T, VW), BF16),
        scratch_shapes=[pltpu.VMEM((GLA_HEADS, LANES, GLA_DV), F32),
                        pltpu.VMEM((rows, QW), F32), pltpu.VMEM((rows, QW), F32),
                        pltpu.VMEM((rows, QW), F32), pltpu.VMEM((rows, VW), F32)],
        compiler_params=_cparams("parallel", "arbitrary"),
        name="gla_mixer",
    )(proj, proj, proj, proj, proj, wg, gate_bias.reshape(1, QW), gnorm.reshape(1, GLA_DV))


def _mix_out_body(x_ref, g0_ref, g1_ref, g2_ref, gb_ref, ya_ref, yc_ref,
                  o0_ref, o1_ref, o2_ref, l0_ref, l1_ref, l2_ref,
                  wa_ref, wb_ref, wc_ref, wo_ref, out_ref):
    l0, l1, l2 = l0_ref[...], l1_ref[...], l2_ref[...]
    m = jnp.maximum(jnp.maximum(l0, l1), l2)
    e0, e1, e2 = jnp.exp(l0 - m), jnp.exp(l1 - m), jnp.exp(l2 - m)
    ob = (e0 * o0_ref[...] + e1 * o1_ref[...] + e2 * o2_ref[...]) / (e0 + e1 + e2)
    gb = gb_ref[...]
    y = _sigmoid(g0_ref[...] + gb[0:1]) * _dot(ya_ref[...], wa_ref[...])
    y = y + _sigmoid(g1_ref[...] + gb[1:2]) * _dot(ob.astype(BF16), wb_ref[...])
    y = y + _sigmoid(g2_ref[...] + gb[2:3]) * _dot(yc_ref[...], wc_ref[...])
    out_ref[...] = x_ref[...] + _dot(y.astype(BF16), wo_ref[...])


def mix_out(x, proj, gate_bias, ya, yc, o_grp, lse_grp, wa, wb, wc, wo, *, tm):
    T, D = x.shape
    W = SWA_HEADS * SWA_DH
    rowblk = lambda w: pl.BlockSpec((tm, w), lambda i: (i, 0))
    full = lambda a: pl.BlockSpec(a.shape, lambda i: (0, 0))
    gate = lambda n: pl.BlockSpec((tm, D), lambda i: (i, COL_GATES // D + n))
    return pl.pallas_call(
        _mix_out_body,
        grid=(T // tm,),
        in_specs=[rowblk(D), gate(0), gate(1), gate(2), full(gate_bias), rowblk(ya.shape[1]),
                  rowblk(yc.shape[1]), rowblk(W), rowblk(W), rowblk(W), rowblk(W), rowblk(W), rowblk(W),
                  full(wa), full(wb), full(wc), full(wo)],
        out_specs=rowblk(D),
        out_shape=jax.ShapeDtypeStruct((T, D), F32),
        compiler_params=_cparams("parallel"),
        name="mix_out",
    )(x, proj, proj, proj, gate_bias, ya, yc, *o_grp, *lse_grp, wa, wb, wc, wo)


def _cross_attn_body(x_ref, gx_ref, wq_ref, kv_ref, gq_ref, gk_ref, wo_ref, out_ref):
    x = x_ref[...]
    ms = jnp.mean(x * x, axis=-1, keepdims=True)
    h = (x * lax.rsqrt(ms + RMS_EPS) * gx_ref[...]).astype(BF16)
    q = _dot(h, wq_ref[...])
    kv = kv_ref[...]
    KW = XA_HEADS * XA_DH
    outs = []
    for hd in range(XA_HEADS):
        qh = q[:, hd * XA_DH:(hd + 1) * XA_DH]
        kh = kv[:, hd * XA_DH:(hd + 1) * XA_DH]
        vh = kv[:, KW + hd * XA_DH:KW + (hd + 1) * XA_DH]
        qn = qh * lax.rsqrt(jnp.mean(qh * qh, axis=-1, keepdims=True) + RMS_EPS) * gq_ref[...]
        kn = kh * lax.rsqrt(jnp.mean(kh * kh, axis=-1, keepdims=True) + RMS_EPS) * gk_ref[...]
        s = _dot_nt(qn.astype(BF16), kn.astype(BF16)) * (XA_DH ** -0.5)
        m = jnp.max(s, axis=-1, keepdims=True)
        p = jnp.exp(s - m)
        l = jnp.sum(p, axis=-1, keepdims=True)
        outs.append((_dot(p.astype(BF16), vh.astype(BF16)) / l).astype(BF16))
    o = jnp.concatenate(outs, axis=-1)
    out_ref[...] = x + _dot(o, wo_ref[...])


def cross_attention(x, kv, gx, wq, gq, gk, wo, *, batch, seq, mem_len, tm):
    T, D = x.shape
    per_batch = seq // tm
    full = lambda a: pl.BlockSpec(a.shape, lambda i: (0, 0))
    gx, gq, gk = gx.reshape(1, D), gq.reshape(1, XA_DH), gk.reshape(1, XA_DH)
    return pl.pallas_call(
        _cross_attn_body,
        grid=(T // tm,),
        in_specs=[pl.BlockSpec((tm, D), lambda i: (i, 0)), full(gx), full(wq),
                  pl.BlockSpec((mem_len, kv.shape[1]), lambda i: (i // per_batch, 0)),
                  full(gq), full(gk), full(wo)],
        out_specs=pl.BlockSpec((tm, D), lambda i: (i, 0)),
        out_shape=jax.ShapeDtypeStruct((T, D), F32),
        compiler_params=_cparams("parallel"),
        name="cross_attention",
    )(x, gx, wq, kv, gq, gk, wo)


def _router_body(x_ref, gx_ref, wr_ref, br_ref, h_ref, idx_ref, gate_ref):
    x = x_ref[...]
    ms = jnp.mean(x * x, axis=-1, keepdims=True)
    h = x * lax.rsqrt(ms + RMS_EPS) * gx_ref[...]
    h_ref[...] = h
    lane = _iota((1, LANES), 1)
    logits = _dot(h, wr_ref[...], HIGHEST) + br_ref[...]
    logits = jnp.where(lane < N_EXPERTS, logits, -jnp.inf)
    idx_out = jnp.zeros(logits.shape, jnp.int32)
    val_out = jnp.full(logits.shape, -jnp.inf, F32)
    for k in range(TOP_K):
        m = jnp.max(logits, axis=-1, keepdims=True)
        sel = jnp.min(jnp.where(logits == m, lane, LANES), axis=-1, keepdims=True)
        idx_out = jnp.where(lane == k, sel, idx_out)
        val_out = jnp.where(lane == k, m, val_out)
        logits = jnp.where(lane == sel, -jnp.inf, logits)
    top = jnp.max(val_out, axis=-1, keepdims=True)
    e = jnp.exp(val_out - top)
    idx_ref[...] = idx_out
    gate_ref[...] = e / jnp.sum(e, axis=-1, keepdims=True)


def moe_router(x, gx, wr, br, *, tm):
    T, D = x.shape
    wr_p = jnp.zeros((D, LANES), F32).at[:, :N_EXPERTS].set(wr)
    br_p = jnp.zeros((1, LANES), F32).at[0, :N_EXPERTS].set(br)
    full = lambda a: pl.BlockSpec(a.shape, lambda i: (0, 0))
    gx = gx.reshape(1, D)
    return pl.pallas_call(
        _router_body,
        grid=(T // tm,),
        in_specs=[pl.BlockSpec((tm, D), lambda i: (i, 0)), full(gx), full(wr_p), full(br_p)],
        out_specs=[pl.BlockSpec((tm, D), lambda i: (i, 0)),
                   pl.BlockSpec((tm, LANES), lambda i: (i, 0)),
                   pl.BlockSpec((tm, LANES), lambda i: (i, 0))],
        out_shape=[jax.ShapeDtypeStruct((T, D), F32),
                   jax.ShapeDtypeStruct((T, LANES), jnp.int32),
                   jax.ShapeDtypeStruct((T, LANES), F32)],
        compiler_params=_cparams("parallel"),
        name="moe_router",
    )(x, gx, wr_p, br_p)


def _expert_body(be_ref, nused_ref, tok0_ref, tokn_ref, dstp_ref, dstc_ref, h_ref, wi_ref, bi_ref, wo_ref, bo_ref,
                 out_ref, xbuf, ybuf, wi16, wo16, gsem, ssem, *, dump_base):
    i = pl.program_id(0)
    nu = nused_ref[0]
    slot = i % 2

    @pl.when((i == 0) | (be_ref[i] != be_ref[jnp.maximum(i - 1, 0)]))
    def _():
        wi16[...] = wi_ref[...].astype(BF16)
        wo16[...] = wo_ref[...].astype(BF16)

    def gather(tok_ref, s, j):
        return pltpu.make_async_copy(h_ref.at[pl.ds(tok_ref[0, 0, j], 1)], xbuf.at[s, pl.ds(j, 1)], gsem.at[s])

    def scatter(dst_ref, s, j):
        return pltpu.make_async_copy(ybuf.at[s, pl.ds(j, 1)], out_ref.at[pl.ds(dst_ref[0, 0, j], 1)], ssem.at[s])

    def for_rows(fn):
        def body(j, carry):
            fn(j)
            return carry
        lax.fori_loop(0, MOE_BM, body, 0, unroll=8)

    def wait_gathered(s):
        pltpu.make_async_copy(xbuf.at[s], xbuf.at[s], gsem.at[s]).wait()

    def wait_scattered(s):
        pltpu.make_async_copy(ybuf.at[s], ybuf.at[s], ssem.at[s]).wait()

    @pl.when(i == 0)
    def _():
        for_rows(lambda j: gather(tok0_ref, 0, j).start())
        ybuf[...] = jnp.zeros_like(ybuf)
        pltpu.make_async_copy(ybuf.at[0], out_ref.at[pl.ds(dump_base, MOE_BM)], ssem.at[0]).start()

    def step(s):
        wait_gathered(s)
        for j in range(MOE_BM):
            gather(tokn_ref, 1 - s, j).start(priority=j % 2)
            scatter(dstp_ref, 1 - s, j).start(priority=j % 2)
        hh = _dot(xbuf[s].astype(BF16), wi16[...]) + bi_ref[...]
        glu = jnp.minimum(hh[:, :D_EXPERT], SWIGLU_LIMIT)
        lin = jnp.clip(hh[:, D_EXPERT:], -SWIGLU_LIMIT, SWIGLU_LIMIT)
        act = glu * _sigmoid(SWIGLU_ALPHA * glu) * (lin + 1.0)
        y = _dot(act.astype(BF16), wo16[...]) + bo_ref[...]
        wait_scattered(s)
        ybuf[s] = y

        @pl.when(i == nu - 1)
        def _():
            for_rows(lambda j: scatter(dstc_ref, s, j).start())
            wait_scattered(s)
            wait_scattered(1 - s)
            wait_gathered(1 - s)

    for s in range(2):
        pl.when((i < nu) & (slot == s))(functools.partial(step, s))


def expert_ffn(h, tok_buf, slot_dst, block_expert, n_used, w_in, b_in, w_out, b_out, *, layer):
    T, D = h.shape
    P = tok_buf.shape[0]
    nb = P // MOE_BM
    L, E, _, F2 = w_in.shape
    dump_base = TOP_K * T
    dst_ext = jnp.concatenate([dump_base + MOE_BM + jnp.arange(MOE_BM, dtype=jnp.int32), slot_dst])
    dst_ext = dst_ext.reshape(nb + 1, 1, MOE_BM)
    idx_blk = lambda f: pl.BlockSpec((1, 1, MOE_BM), f, memory_space=pltpu.SMEM)
    grid_spec = pltpu.PrefetchScalarGridSpec(
        num_scalar_prefetch=2,
        grid=(nb,),
        in_specs=[idx_blk(lambda i, be, nu: (i, 0, 0)),
                  idx_blk(lambda i, be, nu: (jnp.minimum(i + 1, nb - 1), 0, 0)),
                  idx_blk(lambda i, be, nu: (i, 0, 0)),
                  idx_blk(lambda i, be, nu: (i + 1, 0, 0)),
                  pl.BlockSpec(memory_space=pl.ANY),
                  pl.BlockSpec((None, None, D, F2), lambda i, be, nu: (layer, be[i], 0, 0)),
                  pl.BlockSpec((None, None, 1, F2), lambda i, be, nu: (layer, be[i], 0, 0)),
                  pl.BlockSpec((None, None, F2 // 2, D), lambda i, be, nu: (layer, be[i], 0, 0)),
                  pl.BlockSpec((None, None, 1, D), lambda i, be, nu: (layer, be[i], 0, 0))],
        out_specs=pl.BlockSpec(memory_space=pl.ANY),
        scratch_shapes=[pltpu.VMEM((2, MOE_BM, D), F32), pltpu.VMEM((2, MOE_BM, D), F32),
                        pltpu.VMEM((D, F2), BF16), pltpu.VMEM((F2 // 2, D), BF16),
                        pltpu.SemaphoreType.DMA((2,)), pltpu.SemaphoreType.DMA((2,))],
    )
    tok3 = tok_buf.reshape(nb, 1, MOE_BM)
    return pl.pallas_call(
        functools.partial(_expert_body, dump_base=dump_base),
        grid_spec=grid_spec,
        out_shape=jax.ShapeDtypeStruct((dump_base + 2 * MOE_BM, D), F32),
        compiler_params=_cparams("arbitrary"),
        name="expert_ffn",
    )(block_expert, n_used, tok3, tok3, dst_ext, dst_ext, h,
      w_in, b_in.reshape(L, E, 1, F2), w_out, b_out.reshape(L, E, 1, D))


def _moe_combine_body(x_ref, y0_ref, y1_ref, y2_ref, y3_ref, gate_ref, out_ref):
    g = gate_ref[...]
    acc = x_ref[...]
    for k, y_ref in enumerate((y0_ref, y1_ref, y2_ref, y3_ref)):
        acc = acc + g[:, k:k + 1] * y_ref[...]
    out_ref[...] = acc


def moe_combine(x, y4, gates, *, tm):
    T, D = x.shape
    nblk = T // tm
    ysp = lambda k: pl.BlockSpec((tm, D), lambda i: (k * nblk + i, 0))
    return pl.pallas_call(
        _moe_combine_body,
        grid=(nblk,),
        in_specs=[pl.BlockSpec((tm, D), lambda i: (i, 0)), ysp(0), ysp(1), ysp(2), ysp(3),
                  pl.BlockSpec((tm, LANES), lambda i: (i, 0))],
        out_specs=pl.BlockSpec((tm, D), lambda i: (i, 0)),
        out_shape=jax.ShapeDtypeStruct((T, D), F32),
        compiler_params=_cparams("parallel"),
        name="moe_combine",
    )(x, y4, y4, y4, y4, gates)


def _pack_w_in(w_in):
    L, D, _ = w_in.shape
    o_alpha = 2048
    o_b = 2056
    o_c = o_b + 2304
    o_low = o_c + 1536
    o_gates = o_low + GLA_RANK
    c = w_in[:, :, o_c:o_low]
    parts = [w_in[:, :, o_gates:],
             w_in[:, :, 0:2048],
             w_in[:, :, o_b:o_c],
             w_in[:, :, o_alpha:o_b], w_in[:, :, o_low:o_gates],
             jnp.zeros((L, D, 256 - 8 - GLA_RANK), w_in.dtype),
             c[:, :, 512:1024], c[:, :, 1024:1536], c[:, :, 0:256], c[:, :, 256:512]]
    packed = jnp.concatenate(parts, axis=-1)
    assert packed.shape[-1] == N_PACKED
    return packed.astype(BF16)


def _to_residue_major(t, dilation):
    if dilation == 1:
        return t
    *lead, S, W = t.shape
    return jnp.swapaxes(t.reshape(*lead, S // dilation, dilation, W), -2, -3).reshape(*lead, S, W)


def _from_residue_major(t, dilation):
    if dilation == 1:
        return t
    *lead, S, W = t.shape
    return jnp.swapaxes(t.reshape(*lead, dilation, S // dilation, W), -2, -3).reshape(*lead, S, W)


def _routing_tables(idx, n_tokens):
    A = n_tokens * TOP_K
    P = A + N_EXPERTS * MOE_BM
    e_flat = idx.reshape(A)
    onehot = (e_flat[:, None] == jnp.arange(N_EXPERTS, dtype=jnp.int32)[None, :]).astype(jnp.int32)
    csum = jnp.cumsum(onehot, axis=0)
    rank = jnp.sum(onehot * csum, axis=1) - 1
    counts = csum[-1]
    padded = (counts + MOE_BM - 1) // MOE_BM * MOE_BM
    pad_ends = jnp.cumsum(padded)
    pad_starts = pad_ends - padded
    dest = (pad_starts[e_flat] + rank).astype(jnp.int32)
    slot_a = jnp.full((P,), -1, jnp.int32).at[dest].set(jnp.arange(A, dtype=jnp.int32))
    tok_buf = jnp.maximum(slot_a, 0) // TOP_K
    p = jnp.arange(P, dtype=jnp.int32)
    dump = TOP_K * n_tokens + ((p // MOE_BM) % 2) * MOE_BM + p % MOE_BM
    slot_dst = jnp.where(slot_a < 0, dump, (slot_a % TOP_K) * n_tokens + slot_a // TOP_K)
    block_start = jnp.arange(P // MOE_BM, dtype=jnp.int32) * MOE_BM
    block_expert = jnp.minimum(jnp.sum((pad_ends[None, :] <= block_start[:, None]).astype(jnp.int32), axis=1),
                               N_EXPERTS - 1)
    n_used = (pad_ends[-1:] // MOE_BM).astype(jnp.int32)
    return tok_buf, slot_dst, block_expert, n_used


def kernel(x, mem, positions, norm_mix, w_in, gate_bias, gdn_conv, gdn_a_log, gdn_dt_bias, gdn_norm, swa_q_norm, swa_k_norm, gla_gate_up, gla_gate_bias, gla_norm, w_branch_a, w_branch_b, w_branch_c, w_mix_out, norm_cross, norm_mem, xa_wq, xa_wkv, xa_q_norm, xa_k_norm, xa_wo, norm_ffn, router_w, router_b, moe_w_in, moe_b_in, moe_w_out, moe_b_out):
    B, S, D = x.shape
    T = B * S
    M = mem.shape[1]
    depth = w_in.shape[0]
    W = SWA_HEADS * SWA_DH

    inv_freq = ROPE_THETA ** (-jnp.arange(0, SWA_DH, 2, dtype=F32) / SWA_DH)
    ang = positions.astype(F32).reshape(T, 1) * inv_freq[None, :]
    cos, sin = jnp.cos(ang), jnp.sin(ang)
    cosf = jnp.tile(jnp.concatenate([cos, cos], axis=-1), (1, SWA_HEADS))
    sins = jnp.tile(jnp.concatenate([-sin, sin], axis=-1), (1, SWA_HEADS))

    w_in_p = _pack_w_in(w_in)
    xf = x.reshape(T, D)
    memf = mem.reshape(B * M, D)

    for l in range(depth):
        proj = norm_matmul(xf, norm_mix[l], w_in_p[l], tm=1024, tn=1536, name="in_proj")
        ya = gdn_mixer(proj, gdn_conv[l], gdn_a_log[l], gdn_dt_bias[l], gdn_norm[l], batch=B, seq=S, rows=512)
        qn, kn, vn = qk_norm_rope(proj, cosf, sins, swa_q_norm[l], swa_k_norm[l], tm=1024)
        o_grp, lse_grp = [], []
        for gi, (window, dil) in enumerate(SWA_GROUPS):
            assert window // dil == SWA_BACK
            rm = lambda t: _to_residue_major(t.reshape(B, S, W), dil)
            o_g, lse_g = band_attention(rm(qn[gi]), rm(kn[gi]), rm(vn[gi]), dilation=dil)
            o_grp.append(_from_residue_major(o_g, dil).reshape(T, W))
            lse_grp.append(_from_residue_major(lse_g, dil).reshape(T, W))
        yc = gla_mixer(proj, gla_gate_up[l], gla_gate_bias[l], gla_norm[l], batch=B, seq=S, rows=512)
        xf = mix_out(xf, proj, gate_bias[l], ya, yc, o_grp, lse_grp,
                     w_branch_a[l].astype(BF16), w_branch_b[l].astype(BF16), w_branch_c[l].astype(BF16),
                     w_mix_out[l].astype(BF16), tm=512)

        kv = norm_matmul(memf, norm_mem[l], xa_wkv[l].astype(BF16), tm=min(1024, B * M), tn=1024, name="mem_kv")
        xf = cross_attention(xf, kv, norm_cross[l], xa_wq[l].astype(BF16), xa_q_norm[l], xa_k_norm[l],
                             xa_wo[l].astype(BF16), batch=B, seq=S, mem_len=M, tm=512)

        h, idx, gates = moe_router(xf, norm_ffn[l], router_w[l], router_b[l], tm=1024)
        tok_buf, slot_dst, block_expert, n_used = _routing_tables(idx[:, :TOP_K], T)
        y4 = expert_ffn(h, tok_buf, slot_dst, block_expert, n_used, moe_w_in, moe_b_in, moe_w_out, moe_b_out,
                        layer=l)
        xf = moe_combine(xf, y4, gates, tm=512)

    return xf.reshape(B, S, D)
```

```python
import functools

import jax
import jax.numpy as jnp
from jax import lax
from jax.experimental import pallas as pl
from jax.experimental.pallas import tpu as pltpu

F32 = jnp.float32
BF16 = jnp.bfloat16
HIGHEST = lax.Precision.HIGHEST

RMS_EPS = 1e-6
L2_EPS = 1e-6
LANES = 128
VMEM_LIMIT = 56 * 1024 * 1024

D_MODEL = 1024
GDN_HEADS, GDN_D, GDN_CONV, CHUNK = 4, 128, 4, 64
SWA_GROUPS = ((128, 1), (512, 4), (2048, 16))
SWA_HEADS, SWA_DH, SWA_BACK = 4, 64, 128
ROPE_THETA = 10000.0
GLA_HEADS, GLA_DK, GLA_DV, GLA_RANK, GLA_TAU = 4, 64, 128, 16, 16.0
GLA_SUB = 8
XA_HEADS, XA_DH = 4, 128
N_EXPERTS, TOP_K, D_EXPERT = 32, 4, 1024
SWIGLU_ALPHA, SWIGLU_LIMIT = 1.702, 7.0
MOE_BM = 256

COL_GATES = 0
COL_AQ, COL_AK, COL_AV, COL_AZ = 3072, 3584, 4096, 4608
COL_BQ, COL_BK, COL_BV = 5120, 5888, 6656
COL_SMALL = 7424
COL_CV, COL_CR, COL_CQ, COL_CK = 7680, 8192, 8704, 8960
N_PACKED = 9216


def _cparams(*sem):
    return pltpu.CompilerParams(dimension_semantics=sem, vmem_limit_bytes=VMEM_LIMIT)


def _sigmoid(x):
    return 1.0 / (1.0 + jnp.exp(-x))


def _silu(x):
    return x * _sigmoid(x)


def _softplus(x):
    return jnp.maximum(x, 0.0) + jnp.log(1.0 + jnp.exp(-jnp.abs(x)))


def _dot(a, b, precision=None):
    return jnp.dot(a, b, preferred_element_type=F32, precision=precision)


def _dot_nt(a, b, precision=None):
    return lax.dot_general(a, b, (((1,), (1,)), ((), ())), preferred_element_type=F32, precision=precision)


def _dot_tn(a, b, precision=None):
    return lax.dot_general(a, b, (((0,), (0,)), ((), ())), preferred_element_type=F32, precision=precision)


def _bdot(a, b):
    return lax.dot_general(a, b, (((2,), (1,)), ((0,), (0,))), preferred_element_type=F32)


def _bdot_nt(a, b):
    return lax.dot_general(a, b, (((2,), (2,)), ((0,), (0,))), preferred_element_type=F32)


def _bdot3(a, b):
    ah = a.astype(BF16)
    bh = b.astype(BF16)
    al = (a - ah.astype(F32)).astype(BF16)
    bl = (b - bh.astype(F32)).astype(BF16)
    return _bdot(ah, bh) + (_bdot(ah, bl) + _bdot(al, bh))


def _iota(shape, axis):
    return lax.broadcasted_iota(jnp.int32, shape, axis)


ROW_TILE = D_MODEL // LANES


def _store_row_tiles(ref, x):
    n = x.shape[0]
    for s in range(ROW_TILE):
        ref[pl.ds(s, n, stride=ROW_TILE), :] = x[:, s * LANES:(s + 1) * LANES]


def _load_row_tiles(ref, n):
    return jnp.concatenate([ref[pl.ds(s, n, stride=ROW_TILE), :] for s in range(ROW_TILE)], axis=1)


def _norm_matmul_body(x_ref, g_ref, w_ref, o_ref, h_ref):
    @pl.when(pl.program_id(1) == 0)
    def _():
        x = x_ref[...]
        ms = jnp.mean(x * x, axis=-1, keepdims=True)
        h_ref[...] = (x * lax.rsqrt(ms + RMS_EPS) * g_ref[...]).astype(h_ref.dtype)

    o_ref[...] = _dot(h_ref[...], w_ref[...]).astype(o_ref.dtype)


def norm_matmul(x, gain, w, *, tm, tn, name):
    T, D = x.shape
    N = w.shape[1]
    return pl.pallas_call(
        _norm_matmul_body,
        grid=(T // tm, N // tn),
        in_specs=[pl.BlockSpec((tm, D), lambda i, j: (i, 0)),
                  pl.BlockSpec((1, D), lambda i, j: (0, 0)),
                  pl.BlockSpec((D, tn), lambda i, j: (0, j))],
        out_specs=pl.BlockSpec((tm, tn), lambda i, j: (i, j)),
        out_shape=jax.ShapeDtypeStruct((T, N), F32),
        scratch_shapes=[pltpu.VMEM((tm, D), BF16)],
        compiler_params=_cparams("parallel", "arbitrary"),
        name=name,
    )(x, gain.reshape(1, D), w)


def _gdn_body(q_ref, k_ref, v_ref, z_ref, sm_ref, smt_ref, cq_ref, ck_ref, cv_ref, alog_ref, dtb_ref, gn_ref,
              o_ref, state_ref, tail_ref, u_ref, wq_ref, kd_ref, in_ref, dl_ref, os_ref, *, rows):
    blk = pl.program_id(1)
    nchunk = rows // CHUNK

    @pl.when(blk == 0)
    def _():
        state_ref[...] = jnp.zeros_like(state_ref)
        tail_ref[...] = jnp.zeros_like(tail_ref)

    def conv_silu(x_ref, w_ref, slot):
        x = x_ref[...]
        xp = jnp.concatenate([tail_ref[slot], x], axis=0)
        w = w_ref[...]
        acc = x * w[GDN_CONV - 1:GDN_CONV, :]
        for s in range(1, GDN_CONV):
            acc = acc + pltpu.roll(xp, s, 0)[8:] * w[GDN_CONV - 1 - s:GDN_CONV - s, :]
        tail_ref[slot] = x[rows - 8:rows]
        return _silu(acc)

    q_all = conv_silu(q_ref, cq_ref, 0)
    k_all = conv_silu(k_ref, ck_ref, 1)
    v_all = conv_silu(v_ref, cv_ref, 2)
    sm = sm_ref[...]
    pos_c = _iota((rows, 1), 0) % CHUNK
    pos_r = _iota((1, rows), 1) % CHUNK
    ri = _iota((1, CHUNK, CHUNK), 1)
    ci = _iota((1, CHUNK, CHUNK), 2)
    incl = ri >= ci
    strict = ri > ci
    eye = (ri == ci).astype(F32)
    chunked = lambda t: t.reshape(nchunk, CHUNK, t.shape[-1])

    for hd in range(GDN_HEADS):
        _gdn_chunk_local(hd, q_all, k_all, v_all, sm, smt_ref, alog_ref, dtb_ref, pos_c, pos_r, incl, strict, eye,
                         chunked, u_ref, wq_ref, kd_ref, in_ref, dl_ref, rows=rows)

    def chunk_step(c, carry):
        for hd in range(GDN_HEADS):
            st = state_ref[hd]
            wq_s = _dot(wq_ref[hd, c], st.astype(BF16))
            v16 = (u_ref[hd, c] - wq_s[:CHUNK]).astype(BF16)
            o = wq_s[CHUNK:] + _dot(in_ref[hd, c], v16)
            state_ref[hd] = st * dl_ref[hd, c][:, :1] + _dot_tn(kd_ref[hd, c], v16)
            os_ref[pl.ds(pl.multiple_of(c * CHUNK, CHUNK), CHUNK), pl.ds(hd * GDN_D, GDN_D)] = o
        return carry

    lax.fori_loop(0, nchunk, chunk_step, 0, unroll=True)

    for hd in range(GDN_HEADS):
        cols = pl.ds(hd * GDN_D, GDN_D)
        o = os_ref[:, cols]
        ms = jnp.mean(o * o, axis=-1, keepdims=True)
        y = o * lax.rsqrt(ms + RMS_EPS) * gn_ref[...]
        o_ref[:, cols] = (y * _silu(z_ref[:, cols])).astype(o_ref.dtype)


def _gdn_chunk_local(hd, q_all, k_all, v_all, sm, smt_ref, alog_ref, dtb_ref, pos_c, pos_r, incl, strict, eye,
                     chunked, u_ref, wq_ref, kd_ref, in_ref, dl_ref, *, rows):
    nchunk = rows // CHUNK
    cols = slice(hd * GDN_D, (hd + 1) * GDN_D)
    q, k, v = q_all[:, cols], k_all[:, cols], v_all[:, cols]
    q = q * lax.rsqrt(jnp.sum(q * q, axis=-1, keepdims=True) + L2_EPS) * (GDN_D ** -0.5)
    k = k * lax.rsqrt(jnp.sum(k * k, axis=-1, keepdims=True) + L2_EPS)

    neg_a = -jnp.exp(alog_ref[:, hd:hd + 1])
    dt_b = dtb_ref[:, hd:hd + 1]
    g_col = jnp.broadcast_to(neg_a * _softplus(sm[:, hd:hd + 1] + dt_b), (rows, LANES))
    beta = jnp.broadcast_to(_sigmoid(sm[:, GDN_HEADS + hd:GDN_HEADS + hd + 1]), (rows, LANES))
    g_row = neg_a * _softplus(smt_ref[hd:hd + 1, :] + dt_b)

    step = 1
    while step < CHUNK:
        g_col = g_col + jnp.where(pos_c >= step, pltpu.roll(g_col, step, 0), 0.0)
        g_row = g_row + jnp.where(pos_r >= step, pltpu.roll(g_row, step, 1), 0.0)
        step *= 2
    eg = jnp.exp(g_col)

    q3, k3, g3 = chunked(q), chunked(k), chunked(g_col)
    kb3 = chunked(k * beta)
    eg3 = chunked(eg)
    g_row3 = jnp.stack([g_row[:, c * CHUNK:(c + 1) * CHUNK] for c in range(nchunk)], axis=0)
    decay = jnp.where(incl, jnp.exp(jnp.minimum(g3[:, :, :CHUNK] - g_row3, 0.0)), 0.0)
    kq = jnp.concatenate([kb3, q3], axis=1).astype(BF16)
    s = _bdot_nt(kq, k3.astype(BF16))
    lower = jnp.where(strict, s[:, :CHUNK] * decay, 0.0)
    intra = jnp.where(incl, s[:, CHUNK:] * decay, 0.0)
    pw = -lower
    inv = eye + pw
    pw = _bdot3(pw, pw)
    for _ in range(4):
        both = _bdot3(jnp.concatenate([inv, pw], axis=1), pw)
        inv = inv + both[:, :CHUNK]
        pw = both[:, CHUNK:]
    inv = inv + _bdot3(inv, pw)
    uw = _bdot3(inv, jnp.concatenate([chunked(v * beta), kb3 * eg3], axis=2))
    g_last = g3[:, CHUNK - 1:CHUNK, :]
    u_ref[hd] = uw[:, :, :GDN_D]
    wq_ref[hd] = jnp.concatenate([uw[:, :, GDN_D:], q3 * eg3], axis=1).astype(BF16)
    kd_ref[hd] = (k3 * jnp.exp(g_last - g3)).astype(BF16)
    in_ref[hd] = intra.astype(BF16)
    dl_ref[hd] = jnp.exp(g_last)


def gdn_mixer(proj, conv_w, a_log, dt_bias, gnorm, *, batch, seq, rows):
    T = batch * seq
    nblk = seq // rows
    nchunk = rows // CHUNK
    H = GDN_HEADS
    HW = H * GDN_D
    small_t = proj[:, COL_SMALL:COL_SMALL + 2 * H].T
    cb = lambda base: pl.BlockSpec((rows, HW), lambda b, i: (b * nblk + i, base // HW))
    pad = lambda v: jnp.zeros((1, LANES), F32).at[0, :H].set(v)
    wspec = lambda n: pl.BlockSpec((GDN_CONV, HW), lambda b, i: (0, n))
    vec = pl.BlockSpec((1, LANES), lambda b, i: (0, 0))
    return pl.pallas_call(
        functools.partial(_gdn_body, rows=rows),
        grid=(batch, nblk),
        in_specs=[cb(COL_AQ), cb(COL_AK), cb(COL_AV), cb(COL_AZ),
                  pl.BlockSpec((rows, LANES), lambda b, i: (b * nblk + i, COL_SMALL // LANES)),
                  pl.BlockSpec((2 * H, rows), lambda b, i: (0, b * nblk + i)),
                  wspec(0), wspec(1), wspec(2), vec, vec, vec],
        out_specs=pl.BlockSpec((rows, HW), lambda b, i: (b * nblk + i, 0)),
        out_shape=jax.ShapeDtypeStruct((T, HW), BF16),
        scratch_shapes=[pltpu.VMEM((H, GDN_D, GDN_D), F32),
                        pltpu.VMEM((3, 8, HW), F32),
                        pltpu.VMEM((H, nchunk, CHUNK, GDN_D), F32),
                        pltpu.VMEM((H, nchunk, 2 * CHUNK, GDN_D), BF16),
                        pltpu.VMEM((H, nchunk, CHUNK, GDN_D), BF16),
                        pltpu.VMEM((H, nchunk, CHUNK, CHUNK), BF16),
                        pltpu.VMEM((H, nchunk, 1, LANES), F32),
                        pltpu.VMEM((rows, HW), F32)],
        compiler_params=_cparams("parallel", "arbitrary"),
        name="gdn_mixer",
    )(proj, proj, proj, proj, proj, small_t, conv_w, conv_w, conv_w, pad(a_log), pad(dt_bias),
      gnorm.reshape(1, GDN_D))


def _qk_rope_body(q_ref, k_ref, v_ref, cos_ref, sin_ref, gq_ref, gk_ref, gm_ref, qo_ref, ko_ref, vo_ref):
    lane = _iota((1, 2 * LANES), 1)
    first_half = (lane % SWA_DH) < (SWA_DH // 2)
    cosf = cos_ref[...]
    sins = sin_ref[...]
    vo_ref[...] = v_ref[...].astype(vo_ref.dtype)

    def norm_rope(x, gain):
        sq = x * x
        hi = sq.astype(BF16)
        lo = (sq - hi.astype(F32)).astype(BF16)
        ms = _dot(hi, gm_ref[...]) + _dot(lo, gm_ref[...])
        xn = x * lax.rsqrt(ms + RMS_EPS) * gain
        other = jnp.where(first_half, pltpu.roll(xn, 2 * LANES - SWA_DH // 2, 1),
                          pltpu.roll(xn, SWA_DH // 2, 1))
        return xn * cosf + other * sins

    qo_ref[...] = (norm_rope(q_ref[...], gq_ref[...]) * (SWA_DH ** -0.5)).astype(qo_ref.dtype)
    ko_ref[...] = norm_rope(k_ref[...], gk_ref[...]).astype(ko_ref.dtype)


def qk_norm_rope(proj, cosf, sins, q_gain, k_gain, *, tm):
    T = proj.shape[0]
    W = SWA_HEADS * SWA_DH
    grp = jnp.arange(W) // SWA_DH
    gmean = ((grp[:, None] == grp[None, :]).astype(F32) / SWA_DH).astype(BF16)
    tile = lambda v: jnp.tile(v, SWA_HEADS).reshape(1, W)
    vec = pl.BlockSpec((1, W), lambda g, i: (0, 0))
    out = jax.ShapeDtypeStruct((len(SWA_GROUPS), T, W), BF16)
    ospec = pl.BlockSpec((None, tm, W), lambda g, i: (g, i, 0))
    return pl.pallas_call(
        _qk_rope_body,
        grid=(len(SWA_GROUPS), T // tm),
        in_specs=[pl.BlockSpec((tm, W), lambda g, i: (i, COL_BQ // W + g)),
                  pl.BlockSpec((tm, W), lambda g, i: (i, COL_BK // W + g)),
                  pl.BlockSpec((tm, W), lambda g, i: (i, COL_BV // W + g)),
                  pl.BlockSpec((tm, W), lambda g, i: (i, 0)),
                  pl.BlockSpec((tm, W), lambda g, i: (i, 0)),
                  vec, vec, pl.BlockSpec((W, W), lambda g, i: (0, 0))],
        out_specs=[ospec, ospec, ospec],
        out_shape=[out, out, out],
        compiler_params=_cparams("parallel", "parallel"),
        name="qk_norm_rope",
    )(proj, proj, proj, cosf, sins, tile(q_gain), tile(k_gain), gmean)


def _band_attn_body(q_ref, kp_ref, kc_ref, vp_ref, vc_ref, o_ref, lse_ref, *, tiles_per_residue):
    i = pl.program_id(1)
    first = (i % tiles_per_residue) == 0
    c = SWA_BACK
    rows = q_ref.shape[0]
    W = SWA_HEADS * SWA_DH
    q = q_ref[...]
    kcat = jnp.concatenate([kp_ref[...], kc_ref[...]], axis=0)
    vcat = jnp.concatenate([vp_ref[...], vc_ref[...]], axis=0)
    a = _iota((rows, c + rows), 0)
    b = _iota((rows, c + rows), 1)
    dist = a + c - b
    valid = (dist >= 0) & (dist <= SWA_BACK) & ((b >= c) | jnp.logical_not(first))
    lane = _iota((1, W), 1)
    o_acc = jnp.zeros((rows, W), F32)
    lse_acc = jnp.zeros((rows, W), F32)
    for hd in range(SWA_HEADS):
        hmask = (lane // SWA_DH) == hd
        qh = jnp.where(hmask, q, jnp.zeros_like(q))
        s = jnp.where(valid, _dot_nt(qh, kcat), -jnp.inf)
        m = jnp.max(s, axis=-1, keepdims=True)
        p = jnp.exp(s - m)
        l = jnp.sum(p, axis=-1, keepdims=True)
        pv = _dot(p.astype(BF16), vcat) / l
        o_acc = jnp.where(hmask, pv, o_acc)
        lse_acc = jnp.where(hmask, m + jnp.log(l), lse_acc)
    o_ref[...] = o_acc
    lse_ref[...] = lse_acc


def band_attention(q, k, v, *, dilation):
    B, S, W = q.shape
    c = SWA_BACK
    rows = 2 * c
    assert (S // dilation) % rows == 0
    ntile = S // rows
    cur = pl.BlockSpec((None, rows, W), lambda b, i: (b, i, 0))
    prev = pl.BlockSpec((None, c, W), lambda b, i: (b, jnp.maximum(i * (rows // c) - 1, 0), 0))
    out = jax.ShapeDtypeStruct((B, S, W), F32)
    return pl.pallas_call(
        functools.partial(_band_attn_body, tiles_per_residue=ntile // dilation),
        grid=(B, ntile),
        in_specs=[cur, prev, cur, prev, cur],
        out_specs=[cur, cur],
        out_shape=[out, out],
        compiler_params=_cparams("parallel", "parallel"),
        name=f"band_attention_d{dilation}",
    )(q, k, k, v, v)


def _gla_body(q_ref, k_ref, v_ref, r_ref, sm_ref, wg_ref, bg_ref, gn_ref, o_ref,
              state_ref, qs_ref, ks_ref, la_ref, os_ref, *, rows):
    blk = pl.program_id(1)
    nchunk = rows // CHUNK
    C = CHUNK

    @pl.when(blk == 0)
    def _():
        state_ref[...] = jnp.zeros_like(state_ref)

    x = _dot(sm_ref[...].astype(BF16), wg_ref[...]) + bg_ref[...]
    la_ref[...] = (jnp.minimum(x, 0.0) - jnp.log(1.0 + jnp.exp(-jnp.abs(x)))) * (1.0 / GLA_TAU)
    qs_ref[...] = q_ref[...] * (GLA_DK ** -0.5)
    ks_ref[...] = k_ref[...]

    ri = _iota((C, C), 0)
    ci = _iota((C, C), 1)
    tril = (ri >= ci).astype(F32)
    lane = _iota((1, LANES), 1)
    lrow = _iota((LANES, C), 0)
    gsum = [((lrow // GLA_DK) == h2).astype(BF16) for h2 in range(2)]
    hmask = [(lane // GLA_DK) == h2 for h2 in range(2)]
    nsub = C // GLA_SUB

    def chunk_step(c, carry):
        r0 = pl.multiple_of(c * C, C)
        for pair in range(GLA_HEADS // 2):
            cols = pl.ds(pair * LANES, LANES)
            qc = qs_ref[pl.ds(r0, C), cols]
            kc = ks_ref[pl.ds(r0, C), cols]
            la = la_ref[pl.ds(r0, C), cols]
            bcum = _dot(tril, la, HIGHEST)
            b_last = bcum[C - 1:C, :]
            s_rows = [[jnp.zeros((GLA_SUB, C), F32)] for _ in range(2)]
            for sb in range(1, nsub):
                lo = sb * GLA_SUB
                bref = bcum[lo - 1:lo, :]
                q_sb = (qc[lo:lo + GLA_SUB] * jnp.exp(bcum[lo:lo + GLA_SUB] - bref))
                k_sb = (kc * jnp.exp(jnp.minimum(bref - bcum, 0.0))).astype(BF16)
                for h2 in range(2):
                    qm = jnp.where(hmask[h2], q_sb, 0.0).astype(BF16)
                    s_rows[h2].append(_dot_nt(qm, k_sb))
            scores = []
            for h2 in range(2):
                s_off = jnp.concatenate(s_rows[h2], axis=0)
                scores.append(jnp.where((ri // GLA_SUB) > (ci // GLA_SUB), s_off, 0.0))
            for off in range(GLA_SUB):
                if off == 0:
                    prod = qc * kc
                else:
                    k_sh = pltpu.roll(kc, off, 0)
                    b_sh = pltpu.roll(bcum, off, 0)
                    prod = qc * k_sh * jnp.exp(jnp.minimum(bcum - b_sh, 0.0))
                p16 = prod.astype(BF16)
                on_diag = ((ri - ci) == off) & ((ri // GLA_SUB) == (ci // GLA_SUB))
                for h2 in range(2):
                    d = _dot(p16, gsum[h2])
                    scores[h2] = jnp.where(on_diag, d, scores[h2])
            q_dec = qc * jnp.exp(bcum)
            k_dec = kc * jnp.exp(b_last - bcum)
            dec_last = jnp.exp(b_last)
            for h2 in range(2):
                hd = pair * 2 + h2
                vc = v_ref[pl.ds(r0, C), pl.ds(hd * GLA_DV, GLA_DV)].astype(BF16)
                st = state_ref[hd]
                qd = jnp.where(hmask[h2], q_dec, 0.0).astype(BF16)
                kd = jnp.where(hmask[h2], k_dec, 0.0).astype(BF16)
                o = _dot(qd, st.astype(BF16)) + _dot(scores[h2].astype(BF16), vc)
                dl_col = jnp.sum(jnp.where(_iota((LANES, LANES), 0) == _iota((LANES, LANES), 1),
                                           jnp.broadcast_to(dec_last, (LANES, LANES)), 0.0),
                                 axis=-1, keepdims=True)
                state_ref[hd] = st * dl_col + _dot_tn(kd, vc)
                os_ref[pl.ds(r0, C), pl.ds(hd * GLA_DV, GLA_DV)] = o
        return carry

    lax.fori_loop(0, nchunk, chunk_step, 0, unroll=2)

    for hd in range(GLA_HEADS):
        cols = pl.ds(hd * GLA_DV, GLA_DV)
        o = os_ref[:, cols]
        ms = jnp.mean(o * o, axis=-1, keepdims=True)
        y = o * lax.rsqrt(ms + RMS_EPS) * gn_ref[...]
        o_ref[:, cols] = (y * _silu(r_ref[:, cols])).astype(o_ref.dtype)


def gla_mixer(proj, gate_up, gate_bias, gnorm, *, batch, seq, rows):
    T = batch * seq
    nblk = seq // rows
    QW = GLA_HEADS * GLA_DK
    VW = GLA_HEADS * GLA_DV
    row = lambda b, i: b * nblk + i
    wg = jnp.zeros((LANES, QW), F32).at[8:8 + GLA_RANK].set(gate_up).astype(BF16)
    return pl.pallas_call(
        functools.partial(_gla_body, rows=rows),
        grid=(batch, nblk),
        in_specs=[pl.BlockSpec((rows, QW), lambda b, i: (row(b, i), COL_CQ // QW)),
                  pl.BlockSpec((rows, QW), lambda b, i: (row(b, i), COL_CK // QW)),
                  pl.BlockSpec((rows, VW), lambda b, i: (row(b, i), COL_CV // VW)),
                  pl.BlockSpec((rows, VW), lambda b, i: (row(b, i), COL_CR // VW)),
                  pl.BlockSpec((rows, LANES), lambda b, i: (row(b, i), COL_SMALL // LANES)),
                  pl.BlockSpec((LANES, QW), lambda b, i: (0, 0)),
                  pl.BlockSpec((1, QW), lambda b, i: (0, 0)),
                  pl.BlockSpec((1, GLA_DV), lambda b, i: (0, 0))],
        out_specs=pl.BlockSpec((rows, VW), lambda b, i: (row(b, i), 0)),
        out_shape=jax.ShapeDtypeStruct((T, VW), BF16),
        scratch_shapes=[pltpu.VMEM((GLA_HEADS, LANES, GLA_DV), F32),
                        pltpu.VMEM((rows, QW), F32), pltpu.VMEM((rows, QW), F32),
                        pltpu.VMEM((rows, QW), F32), pltpu.VMEM((rows, VW), F32)],
        compiler_params=_cparams("parallel", "arbitrary"),
        name="gla_mixer",
    )(proj, proj, proj, proj, proj, wg, gate_bias.reshape(1, QW), gnorm.reshape(1, GLA_DV))


def _mix_out_body(x_ref, g0_ref, g1_ref, g2_ref, gb_ref, ya_ref, yc_ref,
                  o0_ref, o1_ref, o2_ref, l0_ref, l1_ref, l2_ref,
                  wa_ref, wb_ref, wc_ref, wo_ref, out_ref):
    l0, l1, l2 = l0_ref[...], l1_ref[...], l2_ref[...]
    m = jnp.maximum(jnp.maximum(l0, l1), l2)
    e0, e1, e2 = jnp.exp(l0 - m), jnp.exp(l1 - m), jnp.exp(l2 - m)
    ob = (e0 * o0_ref[...] + e1 * o1_ref[...] + e2 * o2_ref[...]) / (e0 + e1 + e2)
    gb = gb_ref[...]
    y = _sigmoid(g0_ref[...] + gb[0:1]) * _dot(ya_ref[...], wa_ref[...])
    y = y + _sigmoid(g1_ref[...] + gb[1:2]) * _dot(ob.astype(BF16), wb_ref[...])
    y = y + _sigmoid(g2_ref[...] + gb[2:3]) * _dot(yc_ref[...], wc_ref[...])
    out_ref[...] = x_ref[...] + _dot(y.astype(BF16), wo_ref[...])


def mix_out(x, proj, gate_bias, ya, yc, o_grp, lse_grp, wa, wb, wc, wo, *, tm):
    T, D = x.shape
    W = SWA_HEADS * SWA_DH
    rowblk = lambda w: pl.BlockSpec((tm, w), lambda i: (i, 0))
    full = lambda a: pl.BlockSpec(a.shape, lambda i: (0, 0))
    gate = lambda n: pl.BlockSpec((tm, D), lambda i: (i, COL_GATES // D + n))
    return pl.pallas_call(
        _mix_out_body,
        grid=(T // tm,),
        in_specs=[rowblk(D), gate(0), gate(1), gate(2), full(gate_bias), rowblk(ya.shape[1]),
                  rowblk(yc.shape[1]), rowblk(W), rowblk(W), rowblk(W), rowblk(W), rowblk(W), rowblk(W),
                  full(wa), full(wb), full(wc), full(wo)],
        out_specs=rowblk(D),
        out_shape=jax.ShapeDtypeStruct((T, D), F32),
        compiler_params=_cparams("parallel"),
        name="mix_out",
    )(x, proj, proj, proj, gate_bias, ya, yc, *o_grp, *lse_grp, wa, wb, wc, wo)


def _cross_attn_body(x_ref, gx_ref, wq_ref, kv_ref, gq_ref, gk_ref, wo_ref, out_ref):
    x = x_ref[...]
    ms = jnp.mean(x * x, axis=-1, keepdims=True)
    h = (x * lax.rsqrt(ms + RMS_EPS) * gx_ref[...]).astype(BF16)
    q = _dot(h, wq_ref[...])
    kv = kv_ref[...]
    KW = XA_HEADS * XA_DH
    outs = []
    for hd in range(XA_HEADS):
        qh = q[:, hd * XA_DH:(hd + 1) * XA_DH]
        kh = kv[:, hd * XA_DH:(hd + 1) * XA_DH]
        vh = kv[:, KW + hd * XA_DH:KW + (hd + 1) * XA_DH]
        qn = qh * lax.rsqrt(jnp.mean(qh * qh, axis=-1, keepdims=True) + RMS_EPS) * gq_ref[...]
        kn = kh * lax.rsqrt(jnp.mean(kh * kh, axis=-1, keepdims=True) + RMS_EPS) * gk_ref[...]
        s = _dot_nt(qn.astype(BF16), kn.astype(BF16)) * (XA_DH ** -0.5)
        m = jnp.max(s, axis=-1, keepdims=True)
        p = jnp.exp(s - m)
        l = jnp.sum(p, axis=-1, keepdims=True)
        outs.append((_dot(p.astype(BF16), vh.astype(BF16)) / l).astype(BF16))
    o = jnp.concatenate(outs, axis=-1)
    out_ref[...] = x + _dot(o, wo_ref[...])


def cross_attention(x, kv, gx, wq, gq, gk, wo, *, batch, seq, mem_len, tm):
    T, D = x.shape
    per_batch = seq // tm
    full = lambda a: pl.BlockSpec(a.shape, lambda i: (0, 0))
    gx, gq, gk = gx.reshape(1, D), gq.reshape(1, XA_DH), gk.reshape(1, XA_DH)
    return pl.pallas_call(
        _cross_attn_body,
        grid=(T // tm,),
        in_specs=[pl.BlockSpec((tm, D), lambda i: (i, 0)), full(gx), full(wq),
                  pl.BlockSpec((mem_len, kv.shape[1]), lambda i: (i // per_batch, 0)),
                  full(gq), full(gk), full(wo)],
        out_specs=pl.BlockSpec((tm, D), lambda i: (i, 0)),
        out_shape=jax.ShapeDtypeStruct((T, D), F32),
        compiler_params=_cparams("parallel"),
        name="cross_attention",
    )(x, gx, wq, kv, gq, gk, wo)


def _router_body(x_ref, gx_ref, wr_ref, br_ref, h_ref, idx_ref, gate_ref):
    x = x_ref[...]
    ms = jnp.mean(x * x, axis=-1, keepdims=True)
    h = x * lax.rsqrt(ms + RMS_EPS) * gx_ref[...]
    _store_row_tiles(h_ref, h)
    lane = _iota((1, LANES), 1)
    logits = _dot(h, wr_ref[...], HIGHEST) + br_ref[...]
    logits = jnp.where(lane < N_EXPERTS, logits, -jnp.inf)
    idx_out = jnp.zeros(logits.shape, jnp.int32)
    val_out = jnp.full(logits.shape, -jnp.inf, F32)
    for k in range(TOP_K):
        m = jnp.max(logits, axis=-1, keepdims=True)
        sel = jnp.min(jnp.where(logits == m, lane, LANES), axis=-1, keepdims=True)
        idx_out = jnp.where(lane == k, sel, idx_out)
        val_out = jnp.where(lane == k, m, val_out)
        logits = jnp.where(lane == sel, -jnp.inf, logits)
    top = jnp.max(val_out, axis=-1, keepdims=True)
    e = jnp.exp(val_out - top)
    idx_ref[...] = idx_out
    gate_ref[...] = e / jnp.sum(e, axis=-1, keepdims=True)


def moe_router(x, gx, wr, br, *, tm):
    T, D = x.shape
    wr_p = jnp.zeros((D, LANES), F32).at[:, :N_EXPERTS].set(wr)
    br_p = jnp.zeros((1, LANES), F32).at[0, :N_EXPERTS].set(br)
    full = lambda a: pl.BlockSpec(a.shape, lambda i: (0, 0))
    gx = gx.reshape(1, D)
    return pl.pallas_call(
        _router_body,
        grid=(T // tm,),
        in_specs=[pl.BlockSpec((tm, D), lambda i: (i, 0)), full(gx), full(wr_p), full(br_p)],
        out_specs=[pl.BlockSpec((tm * ROW_TILE, LANES), lambda i: (i, 0)),
                   pl.BlockSpec((tm, LANES), lambda i: (i, 0)),
                   pl.BlockSpec((tm, LANES), lambda i: (i, 0))],
        out_shape=[jax.ShapeDtypeStruct((T * ROW_TILE, LANES), F32),
                   jax.ShapeDtypeStruct((T, LANES), jnp.int32),
                   jax.ShapeDtypeStruct((T, LANES), F32)],
        compiler_params=_cparams("parallel"),
        name="moe_router",
    )(x, gx, wr_p, br_p)


def _expert_body(be_ref, nused_ref, tok0_ref, tokn_ref, dstp_ref, dstc_ref, h_ref, wi_ref, bi_ref, wo_ref, bo_ref,
                 out_ref, xbuf, ybuf, wi16, wo16, gsem, ssem, *, dump_base):
    i = pl.program_id(0)
    nu = nused_ref[0]
    slot = i % 2

    @pl.when((i == 0) | (be_ref[i] != be_ref[jnp.maximum(i - 1, 0)]))
    def _():
        wi16[...] = wi_ref[...].astype(BF16)
        wo16[...] = wo_ref[...].astype(BF16)

    def tile_rows(start):
        return pl.ds(pl.multiple_of(start, ROW_TILE), ROW_TILE)

    def gather(tok_ref, s, j):
        return pltpu.make_async_copy(h_ref.at[tile_rows(tok_ref[0, 0, j])], xbuf.at[s, tile_rows(j * ROW_TILE)],
                                     gsem.at[s])

    def scatter(dst_ref, s, j):
        return pltpu.make_async_copy(ybuf.at[s, tile_rows(j * ROW_TILE)], out_ref.at[tile_rows(dst_ref[0, 0, j])],
                                     ssem.at[s])

    def for_rows(fn):
        def body(j, carry):
            fn(j)
            return carry
        lax.fori_loop(0, MOE_BM, body, 0, unroll=8)

    def wait_gathered(s):
        pltpu.make_async_copy(xbuf.at[s], xbuf.at[s], gsem.at[s]).wait()

    def wait_scattered(s):
        pltpu.make_async_copy(ybuf.at[s], ybuf.at[s], ssem.at[s]).wait()

    @pl.when(i == 0)
    def _():
        for_rows(lambda j: gather(tok0_ref, 0, j).start())
        ybuf[...] = jnp.zeros_like(ybuf)
        pltpu.make_async_copy(ybuf.at[0], out_ref.at[pl.ds(dump_base * ROW_TILE, MOE_BM * ROW_TILE)],
                              ssem.at[0]).start()

    def step(s):
        wait_gathered(s)
        for j in range(MOE_BM):
            gather(tokn_ref, 1 - s, j).start()
            scatter(dstp_ref, 1 - s, j).start()
        hh = _dot(_load_row_tiles(xbuf.at[s], MOE_BM).astype(BF16), wi16[...]) + bi_ref[...]
        glu = jnp.minimum(hh[:, :D_EXPERT], SWIGLU_LIMIT)
        lin = jnp.clip(hh[:, D_EXPERT:], -SWIGLU_LIMIT, SWIGLU_LIMIT)
        act = glu * _sigmoid(SWIGLU_ALPHA * glu) * (lin + 1.0)
        y = _dot(act.astype(BF16), wo16[...]) + bo_ref[...]
        wait_scattered(s)
        _store_row_tiles(ybuf.at[s], y)

        @pl.when(i == nu - 1)
        def _():
            for_rows(lambda j: scatter(dstc_ref, s, j).start())
            wait_scattered(s)
            wait_scattered(1 - s)
            wait_gathered(1 - s)

    for s in range(2):
        pl.when((i < nu) & (slot == s))(functools.partial(step, s))


def expert_ffn(h, tok_buf, slot_dst, block_expert, n_used, w_in, b_in, w_out, b_out, *, layer):
    T, D = h.shape[0] // ROW_TILE, D_MODEL
    P = tok_buf.shape[0]
    nb = P // MOE_BM
    L, E, _, F2 = w_in.shape
    dump_base = TOP_K * T
    dst_ext = jnp.concatenate([dump_base + MOE_BM + jnp.arange(MOE_BM, dtype=jnp.int32), slot_dst]) * ROW_TILE
    dst_ext = dst_ext.reshape(nb + 1, 1, MOE_BM)
    idx_blk = lambda f: pl.BlockSpec((1, 1, MOE_BM), f, memory_space=pltpu.SMEM)
    grid_spec = pltpu.PrefetchScalarGridSpec(
        num_scalar_prefetch=2,
        grid=(nb,),
        in_specs=[idx_blk(lambda i, be, nu: (i, 0, 0)),
                  idx_blk(lambda i, be, nu: (jnp.minimum(i + 1, nb - 1), 0, 0)),
                  idx_blk(lambda i, be, nu: (i, 0, 0)),
                  idx_blk(lambda i, be, nu: (i + 1, 0, 0)),
                  pl.BlockSpec(memory_space=pl.ANY),
                  pl.BlockSpec((None, None, D, F2), lambda i, be, nu: (layer, be[i], 0, 0)),
                  pl.BlockSpec((None, None, 1, F2), lambda i, be, nu: (layer, be[i], 0, 0)),
                  pl.BlockSpec((None, None, F2 // 2, D), lambda i, be, nu: (layer, be[i], 0, 0)),
                  pl.BlockSpec((None, None, 1, D), lambda i, be, nu: (layer, be[i], 0, 0))],
        out_specs=pl.BlockSpec(memory_space=pl.ANY),
        scratch_shapes=[pltpu.VMEM((2, MOE_BM * ROW_TILE, LANES), F32), pltpu.VMEM((2, MOE_BM * ROW_TILE, LANES), F32),
                        pltpu.VMEM((D, F2), BF16), pltpu.VMEM((F2 // 2, D), BF16),
                        pltpu.SemaphoreType.DMA((2,)), pltpu.SemaphoreType.DMA((2,))],
    )
    tok3 = (tok_buf * ROW_TILE).reshape(nb, 1, MOE_BM)
    return pl.pallas_call(
        functools.partial(_expert_body, dump_base=dump_base),
        grid_spec=grid_spec,
        out_shape=jax.ShapeDtypeStruct(((dump_base + 2 * MOE_BM) * ROW_TILE, LANES), F32),
        compiler_params=_cparams("arbitrary"),
        name="expert_ffn",
    )(block_expert, n_used, tok3, tok3, dst_ext, dst_ext, h,
      w_in, b_in.reshape(L, E, 1, F2), w_out, b_out.reshape(L, E, 1, D))


def _moe_combine_body(x_ref, y0_ref, y1_ref, y2_ref, y3_ref, gate_ref, out_ref):
    g = gate_ref[...]
    acc = x_ref[...]
    for k, y_ref in enumerate((y0_ref, y1_ref, y2_ref, y3_ref)):
        acc = acc + g[:, k:k + 1] * _load_row_tiles(y_ref, x_ref.shape[0])
    out_ref[...] = acc


def moe_combine(x, y4, gates, *, tm):
    T, D = x.shape
    nblk = T // tm
    ysp = lambda k: pl.BlockSpec((tm * ROW_TILE, LANES), lambda i: (k * nblk + i, 0))
    return pl.pallas_call(
        _moe_combine_body,
        grid=(nblk,),
        in_specs=[pl.BlockSpec((tm, D), lambda i: (i, 0)), ysp(0), ysp(1), ysp(2), ysp(3),
                  pl.BlockSpec((tm, LANES), lambda i: (i, 0))],
        out_specs=pl.BlockSpec((tm, D), lambda i: (i, 0)),
        out_shape=jax.ShapeDtypeStruct((T, D), F32),
        compiler_params=_cparams("parallel"),
        name="moe_combine",
    )(x, y4, y4, y4, y4, gates)


def _pack_w_in(w_in):
    L, D, _ = w_in.shape
    o_alpha = 2048
    o_b = 2056
    o_c = o_b + 2304
    o_low = o_c + 1536
    o_gates = o_low + GLA_RANK
    c = w_in[:, :, o_c:o_low]
    parts = [w_in[:, :, o_gates:],
             w_in[:, :, 0:2048],
             w_in[:, :, o_b:o_c],
             w_in[:, :, o_alpha:o_b], w_in[:, :, o_low:o_gates],
             jnp.zeros((L, D, 256 - 8 - GLA_RANK), w_in.dtype),
             c[:, :, 512:1024], c[:, :, 1024:1536], c[:, :, 0:256], c[:, :, 256:512]]
    packed = jnp.concatenate(parts, axis=-1)
    assert packed.shape[-1] == N_PACKED
    return packed.astype(BF16)


def _to_residue_major(t, dilation):
    if dilation == 1:
        return t
    *lead, S, W = t.shape
    return jnp.swapaxes(t.reshape(*lead, S // dilation, dilation, W), -2, -3).reshape(*lead, S, W)


def _from_residue_major(t, dilation):
    if dilation == 1:
        return t
    *lead, S, W = t.shape
    return jnp.swapaxes(t.reshape(*lead, dilation, S // dilation, W), -2, -3).reshape(*lead, S, W)


def _routing_tables(idx, n_tokens):
    A = n_tokens * TOP_K
    P = A + N_EXPERTS * MOE_BM
    e_flat = idx.reshape(A)
    onehot = (e_flat[:, None] == jnp.arange(N_EXPERTS, dtype=jnp.int32)[None, :]).astype(jnp.int32)
    csum = jnp.cumsum(onehot, axis=0)
    rank = jnp.sum(onehot * csum, axis=1) - 1
    counts = csum[-1]
    padded = (counts + MOE_BM - 1) // MOE_BM * MOE_BM
    pad_ends = jnp.cumsum(padded)
    pad_starts = pad_ends - padded
    dest = (pad_starts[e_flat] + rank).astype(jnp.int32)
    slot_a = jnp.full((P,), -1, jnp.int32).at[dest].set(jnp.arange(A, dtype=jnp.int32))
    tok_buf = jnp.maximum(slot_a, 0) // TOP_K
    p = jnp.arange(P, dtype=jnp.int32)
    dump = TOP_K * n_tokens + ((p // MOE_BM) % 2) * MOE_BM + p % MOE_BM
    slot_dst = jnp.where(slot_a < 0, dump, (slot_a % TOP_K) * n_tokens + slot_a // TOP_K)
    block_start = jnp.arange(P // MOE_BM, dtype=jnp.int32) * MOE_BM
    block_expert = jnp.minimum(jnp.sum((pad_ends[None, :] <= block_start[:, None]).astype(jnp.int32), axis=1),
                               N_EXPERTS - 1)
    n_used = (pad_ends[-1:] // MOE_BM).astype(jnp.int32)
    return tok_buf, slot_dst, block_expert, n_used


def kernel(x, mem, positions, norm_mix, w_in, gate_bias, gdn_conv, gdn_a_log, gdn_dt_bias, gdn_norm, swa_q_norm, swa_k_norm, gla_gate_up, gla_gate_bias, gla_norm, w_branch_a, w_branch_b, w_branch_c, w_mix_out, norm_cross, norm_mem, xa_wq, xa_wkv, xa_q_norm, xa_k_norm, xa_wo, norm_ffn, router_w, router_b, moe_w_in, moe_b_in, moe_w_out, moe_b_out):
    B, S, D = x.shape
    T = B * S
    M = mem.shape[1]
    depth = w_in.shape[0]
    W = SWA_HEADS * SWA_DH

    inv_freq = ROPE_THETA ** (-jnp.arange(0, SWA_DH, 2, dtype=F32) / SWA_DH)
    ang = positions.astype(F32).reshape(T, 1) * inv_freq[None, :]
    cos, sin = jnp.cos(ang), jnp.sin(ang)
    cosf = jnp.tile(jnp.concatenate([cos, cos], axis=-1), (1, SWA_HEADS))
    sins = jnp.tile(jnp.concatenate([-sin, sin], axis=-1), (1, SWA_HEADS))

    w_in_p = _pack_w_in(w_in)
    xf = x.reshape(T, D)
    memf = mem.reshape(B * M, D)

    for l in range(depth):
        proj = norm_matmul(xf, norm_mix[l], w_in_p[l], tm=1024, tn=1536, name="in_proj")
        ya = gdn_mixer(proj, gdn_conv[l], gdn_a_log[l], gdn_dt_bias[l], gdn_norm[l], batch=B, seq=S, rows=512)
        qn, kn, vn = qk_norm_rope(proj, cosf, sins, swa_q_norm[l], swa_k_norm[l], tm=1024)
        o_grp, lse_grp = [], []
        for gi, (window, dil) in enumerate(SWA_GROUPS):
            assert window // dil == SWA_BACK
            rm = lambda t: _to_residue_major(t.reshape(B, S, W), dil)
            o_g, lse_g = band_attention(rm(qn[gi]), rm(kn[gi]), rm(vn[gi]), dilation=dil)
            o_grp.append(_from_residue_major(o_g, dil).reshape(T, W))
            lse_grp.append(_from_residue_major(lse_g, dil).reshape(T, W))
        yc = gla_mixer(proj, gla_gate_up[l], gla_gate_bias[l], gla_norm[l], batch=B, seq=S, rows=512)
        xf = mix_out(xf, proj, gate_bias[l], ya, yc, o_grp, lse_grp,
                     w_branch_a[l].astype(BF16), w_branch_b[l].astype(BF16), w_branch_c[l].astype(BF16),
                     w_mix_out[l].astype(BF16), tm=512)

        kv = norm_matmul(memf, norm_mem[l], xa_wkv[l].astype(BF16), tm=min(1024, B * M), tn=1024, name="mem_kv")
        xf = cross_attention(xf, kv, norm_cross[l], xa_wq[l].astype(BF16), xa_q_norm[l], xa_k_norm[l],
                             xa_wo[l].astype(BF16), batch=B, seq=S, mem_len=M, tm=512)

        h, idx, gates = moe_router(xf, norm_ffn[l], router_w[l], router_b[l], tm=1024)
        tok_buf, slot_dst, block_expert, n_used = _routing_tables(idx[:, :TOP_K], T)
        y4 = expert_ffn(h, tok_buf, slot_dst, block_expert, n_used, moe_w_in, moe_b_in, moe_w_out, moe_b_out,
                        layer=l)
        xf = moe_combine(xf, y4, gates, tm=512)

    return xf.reshape(B, S, D)
```

```python
import functools

import jax
import jax.numpy as jnp
from jax import lax
from jax.experimental import pallas as pl
from jax.experimental.pallas import tpu as pltpu

F32 = jnp.float32
BF16 = jnp.bfloat16
HIGHEST = lax.Precision.HIGHEST

RMS_EPS = 1e-6
L2_EPS = 1e-6
LANES = 128
VMEM_LIMIT = 56 * 1024 * 1024

D_MODEL = 1024
GDN_HEADS, GDN_D, GDN_CONV, CHUNK = 4, 128, 4, 64
SWA_GROUPS = ((128, 1), (512, 4), (2048, 16))
SWA_HEADS, SWA_DH, SWA_BACK = 4, 64, 128
ROPE_THETA = 10000.0
GLA_HEADS, GLA_DK, GLA_DV, GLA_RANK, GLA_TAU = 4, 64, 128, 16, 16.0
GLA_SUB = 8
XA_HEADS, XA_DH = 4, 128
N_EXPERTS, TOP_K, D_EXPERT = 32, 4, 1024
SWIGLU_ALPHA, SWIGLU_LIMIT = 1.702, 7.0
MOE_BM = 256

COL_GATES = 0
COL_AQ, COL_AK, COL_AV, COL_AZ = 3072, 3584, 4096, 4608
COL_BQ, COL_BK, COL_BV = 5120, 5888, 6656
COL_SMALL = 7424
COL_CV, COL_CR, COL_CQ, COL_CK = 7680, 8192, 8704, 8960
N_PACKED = 9216


def _cparams(*sem):
    return pltpu.CompilerParams(dimension_semantics=sem, vmem_limit_bytes=VMEM_LIMIT)


def _sigmoid(x):
    return 1.0 / (1.0 + jnp.exp(-x))


def _silu(x):
    return x * _sigmoid(x)


def _softplus(x):
    return jnp.maximum(x, 0.0) + jnp.log(1.0 + jnp.exp(-jnp.abs(x)))


def _dot(a, b, precision=None):
    return jnp.dot(a, b, preferred_element_type=F32, precision=precision)


def _dot_nt(a, b, precision=None):
    return lax.dot_general(a, b, (((1,), (1,)), ((), ())), preferred_element_type=F32, precision=precision)


def _dot_tn(a, b, precision=None):
    return lax.dot_general(a, b, (((0,), (0,)), ((), ())), preferred_element_type=F32, precision=precision)


def _bdot(a, b):
    return lax.dot_general(a, b, (((2,), (1,)), ((0,), (0,))), preferred_element_type=F32)


def _bdot_nt(a, b):
    return lax.dot_general(a, b, (((2,), (2,)), ((0,), (0,))), preferred_element_type=F32)


def _split_bf16(a):
    ah = a.astype(BF16)
    return ah, (a - ah.astype(F32)).astype(BF16)


def _bdot3(a, b):
    (ah, al), (bh, bl) = a, b
    return _bdot(ah, bh) + (_bdot(ah, bl) + _bdot(al, bh))


def _iota(shape, axis):
    return lax.broadcasted_iota(jnp.int32, shape, axis)


ROW_TILE = D_MODEL // LANES


def _store_row_tiles(ref, x):
    n = x.shape[0]
    for s in range(ROW_TILE):
        ref[pl.ds(s, n, stride=ROW_TILE), :] = x[:, s * LANES:(s + 1) * LANES]


def _load_row_tiles(ref, n):
    return jnp.concatenate([ref[pl.ds(s, n, stride=ROW_TILE), :] for s in range(ROW_TILE)], axis=1)


def _norm_matmul_body(x_ref, g_ref, w_ref, o_ref, h_ref):
    @pl.when(pl.program_id(1) == 0)
    def _():
        x = x_ref[...]
        ms = jnp.mean(x * x, axis=-1, keepdims=True)
        h_ref[...] = (x * lax.rsqrt(ms + RMS_EPS) * g_ref[...]).astype(h_ref.dtype)

    o_ref[...] = _dot(h_ref[...], w_ref[...]).astype(o_ref.dtype)


def norm_matmul(x, gain, w, *, tm, tn, name):
    T, D = x.shape
    N = w.shape[1]
    return pl.pallas_call(
        _norm_matmul_body,
        grid=(T // tm, N // tn),
        in_specs=[pl.BlockSpec((tm, D), lambda i, j: (i, 0)),
                  pl.BlockSpec((1, D), lambda i, j: (0, 0)),
                  pl.BlockSpec((D, tn), lambda i, j: (0, j))],
        out_specs=pl.BlockSpec((tm, tn), lambda i, j: (i, j)),
        out_shape=jax.ShapeDtypeStruct((T, N), F32),
        scratch_shapes=[pltpu.VMEM((tm, D), BF16)],
        compiler_params=_cparams("parallel", "arbitrary"),
        name=name,
    )(x, gain.reshape(1, D), w)


def _gdn_body(q_ref, k_ref, v_ref, z_ref, sm_ref, smt_ref, cq_ref, ck_ref, cv_ref, alog_ref, dtb_ref, gn_ref,
              o_ref, state_ref, tail_ref, u_ref, wq_ref, kd_ref, in_ref, dl_ref, os_ref, *, rows):
    blk = pl.program_id(1)
    nchunk = rows // CHUNK

    @pl.when(blk == 0)
    def _():
        state_ref[...] = jnp.zeros_like(state_ref)
        tail_ref[...] = jnp.zeros_like(tail_ref)

    def conv_silu(x_ref, w_ref, slot):
        x = x_ref[...]
        xp = jnp.concatenate([tail_ref[slot], x], axis=0)
        w = w_ref[...]
        acc = x * w[GDN_CONV - 1:GDN_CONV, :]
        for s in range(1, GDN_CONV):
            acc = acc + pltpu.roll(xp, s, 0)[8:] * w[GDN_CONV - 1 - s:GDN_CONV - s, :]
        tail_ref[slot] = x[rows - 8:rows]
        return _silu(acc)

    q_all = conv_silu(q_ref, cq_ref, 0)
    k_all = conv_silu(k_ref, ck_ref, 1)
    v_all = conv_silu(v_ref, cv_ref, 2)
    sm = sm_ref[...]
    pos_c = _iota((rows, 1), 0) % CHUNK
    pos_r = _iota((1, rows), 1) % CHUNK
    ri = _iota((1, CHUNK, CHUNK), 1)
    ci = _iota((1, CHUNK, CHUNK), 2)
    incl = ri >= ci
    strict = ri > ci
    eye = (ri == ci).astype(F32)
    chunked = lambda t: t.reshape(nchunk, CHUNK, t.shape[-1])

    for hd in range(GDN_HEADS):
        _gdn_chunk_local(hd, q_all, k_all, v_all, sm, smt_ref, alog_ref, dtb_ref, pos_c, pos_r, incl, strict, eye,
                         chunked, u_ref, wq_ref, kd_ref, in_ref, dl_ref, rows=rows)

    def chunk_step(c, carry):
        for hd in range(GDN_HEADS):
            st = state_ref[hd]
            wq_s = _dot(wq_ref[hd, c], st.astype(BF16))
            v16 = (u_ref[hd, c] - wq_s[:CHUNK]).astype(BF16)
            o = wq_s[CHUNK:] + _dot(in_ref[hd, c], v16)
            state_ref[hd] = st * dl_ref[hd, c][:, :1] + _dot_tn(kd_ref[hd, c], v16)
            os_ref[pl.ds(pl.multiple_of(c * CHUNK, CHUNK), CHUNK), pl.ds(hd * GDN_D, GDN_D)] = o
        return carry

    lax.fori_loop(0, nchunk, chunk_step, 0, unroll=True)

    for hd in range(GDN_HEADS):
        cols = pl.ds(hd * GDN_D, GDN_D)
        o = os_ref[:, cols]
        ms = jnp.mean(o * o, axis=-1, keepdims=True)
        y = o * lax.rsqrt(ms + RMS_EPS) * gn_ref[...]
        o_ref[:, cols] = (y * _silu(z_ref[:, cols])).astype(o_ref.dtype)


def _gdn_chunk_local(hd, q_all, k_all, v_all, sm, smt_ref, alog_ref, dtb_ref, pos_c, pos_r, incl, strict, eye,
                     chunked, u_ref, wq_ref, kd_ref, in_ref, dl_ref, *, rows):
    nchunk = rows // CHUNK
    cols = slice(hd * GDN_D, (hd + 1) * GDN_D)
    q, k, v = q_all[:, cols], k_all[:, cols], v_all[:, cols]
    q = q * lax.rsqrt(jnp.sum(q * q, axis=-1, keepdims=True) + L2_EPS) * (GDN_D ** -0.5)
    k = k * lax.rsqrt(jnp.sum(k * k, axis=-1, keepdims=True) + L2_EPS)

    neg_a = -jnp.exp(alog_ref[:, hd:hd + 1])
    dt_b = dtb_ref[:, hd:hd + 1]
    g_col = jnp.broadcast_to(neg_a * _softplus(sm[:, hd:hd + 1] + dt_b), (rows, LANES))
    beta = jnp.broadcast_to(_sigmoid(sm[:, GDN_HEADS + hd:GDN_HEADS + hd + 1]), (rows, LANES))
    g_row = neg_a * _softplus(smt_ref[hd:hd + 1, :] + dt_b)

    step = 1
    while step < CHUNK:
        g_col = g_col + jnp.where(pos_c >= step, pltpu.roll(g_col, step, 0), 0.0)
        g_row = g_row + jnp.where(pos_r >= step, pltpu.roll(g_row, step, 1), 0.0)
        step *= 2
    eg = jnp.exp(g_col)

    q3, k3, g3 = chunked(q), chunked(k), chunked(g_col)
    kb3 = chunked(k * beta)
    eg3 = chunked(eg)
    g_row3 = jnp.stack([g_row[:, c * CHUNK:(c + 1) * CHUNK] for c in range(nchunk)], axis=0)
    decay = jnp.where(incl, jnp.exp(jnp.minimum(g3[:, :, :CHUNK] - g_row3, 0.0)), 0.0)
    kq = jnp.concatenate([kb3, q3], axis=1).astype(BF16)
    s = _bdot_nt(kq, k3.astype(BF16))
    lower = jnp.where(strict, s[:, :CHUNK] * decay, 0.0)
    intra = jnp.where(incl, s[:, CHUNK:] * decay, 0.0)
    pw = -lower
    inv = eye + pw
    pw_s = _split_bf16(pw)
    pw = _bdot3(pw_s, pw_s)
    for _ in range(4):
        inv_s, pw_s = _split_bf16(inv), _split_bf16(pw)
        stacked = tuple(jnp.concatenate([i_, p_], axis=1) for i_, p_ in zip(inv_s, pw_s))
        both = _bdot3(stacked, pw_s)
        inv = inv + both[:, :CHUNK]
        pw = both[:, CHUNK:]
    inv = inv + _bdot3(_split_bf16(inv), _split_bf16(pw))
    rhs = jnp.concatenate([chunked(v * beta), kb3 * eg3], axis=2)
    uw = _bdot3(_split_bf16(inv), _split_bf16(rhs))
    g_last = g3[:, CHUNK - 1:CHUNK, :]
    u_ref[hd] = uw[:, :, :GDN_D]
    wq_ref[hd] = jnp.concatenate([uw[:, :, GDN_D:], q3 * eg3], axis=1).astype(BF16)
    kd_ref[hd] = (k3 * jnp.exp(g_last - g3)).astype(BF16)
    in_ref[hd] = intra.astype(BF16)
    dl_ref[hd] = jnp.exp(g_last)


def gdn_mixer(proj, conv_w, a_log, dt_bias, gnorm, *, batch, seq, rows):
    T = batch * seq
    nblk = seq // rows
    nchunk = rows // CHUNK
    H = GDN_HEADS
    HW = H * GDN_D
    small_t = proj[:, COL_SMALL:COL_SMALL + 2 * H].T
    cb = lambda base: pl.BlockSpec((rows, HW), lambda b, i: (b * nblk + i, base // HW))
    pad = lambda v: jnp.zeros((1, LANES), F32).at[0, :H].set(v)
    wspec = lambda n: pl.BlockSpec((GDN_CONV, HW), lambda b, i: (0, n))
    vec = pl.BlockSpec((1, LANES), lambda b, i: (0, 0))
    return pl.pallas_call(
        functools.partial(_gdn_body, rows=rows),
        grid=(batch, nblk),
        in_specs=[cb(COL_AQ), cb(COL_AK), cb(COL_AV), cb(COL_AZ),
                  pl.BlockSpec((rows, LANES), lambda b, i: (b * nblk + i, COL_SMALL // LANES)),
                  pl.BlockSpec((2 * H, rows), lambda b, i: (0, b * nblk + i)),
                  wspec(0), wspec(1), wspec(2), vec, vec, vec],
        out_specs=pl.BlockSpec((rows, HW), lambda b, i: (b * nblk + i, 0)),
        out_shape=jax.ShapeDtypeStruct((T, HW), BF16),
        scratch_shapes=[pltpu.VMEM((H, GDN_D, GDN_D), F32),
                        pltpu.VMEM((3, 8, HW), F32),
                        pltpu.VMEM((H, nchunk, CHUNK, GDN_D), F32),
                        pltpu.VMEM((H, nchunk, 2 * CHUNK, GDN_D), BF16),
                        pltpu.VMEM((H, nchunk, CHUNK, GDN_D), BF16),
                        pltpu.VMEM((H, nchunk, CHUNK, CHUNK), BF16),
                        pltpu.VMEM((H, nchunk, 1, LANES), F32),
                        pltpu.VMEM((rows, HW), F32)],
        compiler_params=_cparams("parallel", "arbitrary"),
        name="gdn_mixer",
    )(proj, proj, proj, proj, proj, small_t, conv_w, conv_w, conv_w, pad(a_log), pad(dt_bias),
      gnorm.reshape(1, GDN_D))


def _qk_rope_body(*refs):
    ng = len(SWA_GROUPS)
    q_refs, k_refs, v_refs = refs[0:ng], refs[ng:2 * ng], refs[2 * ng:3 * ng]
    cos_ref, sin_ref, gq_ref, gk_ref, gm_ref = refs[3 * ng:3 * ng + 5]
    outs = refs[3 * ng + 5:]
    qo_refs, ko_refs, vo_refs = outs[0:ng], outs[ng:2 * ng], outs[2 * ng:3 * ng]
    lane = _iota((1, 2 * LANES), 1)
    first_half = (lane % SWA_DH) < (SWA_DH // 2)
    cosf = cos_ref[...]
    sins = sin_ref[...]

    def norm_rope(x, gain):
        sq = x * x
        hi = sq.astype(BF16)
        lo = (sq - hi.astype(F32)).astype(BF16)
        ms = _dot(hi, gm_ref[...]) + _dot(lo, gm_ref[...])
        xn = x * lax.rsqrt(ms + RMS_EPS) * gain
        other = jnp.where(first_half, pltpu.roll(xn, 2 * LANES - SWA_DH // 2, 1),
                          pltpu.roll(xn, SWA_DH // 2, 1))
        return xn * cosf + other * sins

    for g in range(ng):
        qo_refs[g][...] = (norm_rope(q_refs[g][...], gq_ref[...]) * (SWA_DH ** -0.5)).astype(BF16)
        ko_refs[g][...] = norm_rope(k_refs[g][...], gk_ref[...]).astype(BF16)
        vo_refs[g][...] = v_refs[g][...].astype(BF16)


def qk_norm_rope(proj, cosf, sins, q_gain, k_gain, *, tm):
    T = proj.shape[0]
    W = SWA_HEADS * SWA_DH
    ng = len(SWA_GROUPS)
    grp = jnp.arange(W) // SWA_DH
    gmean = ((grp[:, None] == grp[None, :]).astype(F32) / SWA_DH).astype(BF16)
    tile = lambda v: jnp.tile(v, SWA_HEADS).reshape(1, W)
    vec = pl.BlockSpec((1, W), lambda i: (0, 0))
    col = lambda base, g: pl.BlockSpec((tm, W), lambda i: (i, base // W + g))
    row = pl.BlockSpec((tm, W), lambda i: (i, 0))
    out = jax.ShapeDtypeStruct((T, W), BF16)
    res = pl.pallas_call(
        _qk_rope_body,
        grid=(T // tm,),
        in_specs=([col(COL_BQ, g) for g in range(ng)] + [col(COL_BK, g) for g in range(ng)]
                  + [col(COL_BV, g) for g in range(ng)] + [row, row, vec, vec, pl.BlockSpec((W, W), lambda i: (0, 0))]),
        out_specs=[row] * (3 * ng),
        out_shape=[out] * (3 * ng),
        compiler_params=_cparams("parallel"),
        name="qk_norm_rope",
    )(*([proj] * (3 * ng)), cosf, sins, tile(q_gain), tile(k_gain), gmean)
    return res[0:ng], res[ng:2 * ng], res[2 * ng:3 * ng]


def _band_attn_body(q_ref, kp_ref, kc_ref, vp_ref, vc_ref, o_ref, lse_ref, *, tiles_per_residue):
    i = pl.program_id(1)
    first = (i % tiles_per_residue) == 0
    c = SWA_BACK
    rows = q_ref.shape[0]
    W = SWA_HEADS * SWA_DH
    q = q_ref[...]
    kcat = jnp.concatenate([kp_ref[...], kc_ref[...]], axis=0)
    vcat = jnp.concatenate([vp_ref[...], vc_ref[...]], axis=0)
    a = _iota((rows, c + rows), 0)
    b = _iota((rows, c + rows), 1)
    dist = a + c - b
    valid = (dist >= 0) & (dist <= SWA_BACK) & ((b >= c) | jnp.logical_not(first))
    lane = _iota((1, W), 1)
    o_acc = jnp.zeros((rows, W), F32)
    lse_acc = jnp.zeros((rows, W), F32)
    for hd in range(SWA_HEADS):
        hmask = (lane // SWA_DH) == hd
        qh = jnp.where(hmask, q, jnp.zeros_like(q))
        s = jnp.where(valid, _dot_nt(qh, kcat), -jnp.inf)
        m = jnp.max(s, axis=-1, keepdims=True)
        p = jnp.exp(s - m)
        l = jnp.sum(p, axis=-1, keepdims=True)
        pv = _dot(p.astype(BF16), vcat) / l
        o_acc = jnp.where(hmask, pv, o_acc)
        lse_acc = jnp.where(hmask, m + jnp.log(l), lse_acc)
    o_ref[...] = o_acc
    lse_ref[...] = lse_acc


def band_attention(q, k, v, *, dilation):
    B, S, W = q.shape
    c = SWA_BACK
    rows = 2 * c
    assert (S // dilation) % rows == 0
    ntile = S // rows
    cur = pl.BlockSpec((None, rows, W), lambda b, i: (b, i, 0))
    prev = pl.BlockSpec((None, c, W), lambda b, i: (b, jnp.maximum(i * (rows // c) - 1, 0), 0))
    out = jax.ShapeDtypeStruct((B, S, W), F32)
    return pl.pallas_call(
        functools.partial(_band_attn_body, tiles_per_residue=ntile // dilation),
        grid=(B, ntile),
        in_specs=[cur, prev, cur, prev, cur],
        out_specs=[cur, cur],
        out_shape=[out, out],
        compiler_params=_cparams("parallel", "parallel"),
        name=f"band_attention_d{dilation}",
    )(q, k, k, v, v)


def _gla_body(q_ref, k_ref, v_ref, r_ref, sm_ref, wg_ref, bg_ref, gn_ref, o_ref,
              state_ref, qs_ref, ks_ref, la_ref, os_ref, *, rows):
    blk = pl.program_id(1)
    nchunk = rows // CHUNK
    C = CHUNK

    @pl.when(blk == 0)
    def _():
        state_ref[...] = jnp.zeros_like(state_ref)

    x = _dot(sm_ref[...].astype(BF16), wg_ref[...]) + bg_ref[...]
    la_ref[...] = (jnp.minimum(x, 0.0) - jnp.log(1.0 + jnp.exp(-jnp.abs(x)))) * (1.0 / GLA_TAU)
    qs_ref[...] = q_ref[...] * (GLA_DK ** -0.5)
    ks_ref[...] = k_ref[...]

    ri = _iota((C, C), 0)
    ci = _iota((C, C), 1)
    tril = (ri >= ci).astype(F32)
    lane = _iota((1, LANES), 1)
    lrow = _iota((LANES, C), 0)
    gsum = [((lrow // GLA_DK) == h2).astype(BF16) for h2 in range(2)]
    hmask = [(lane // GLA_DK) == h2 for h2 in range(2)]
    nsub = C // GLA_SUB

    def chunk_step(c, carry):
        r0 = pl.multiple_of(c * C, C)
        for pair in range(GLA_HEADS // 2):
            cols = pl.ds(pair * LANES, LANES)
            qc = qs_ref[pl.ds(r0, C), cols]
            kc = ks_ref[pl.ds(r0, C), cols]
            la = la_ref[pl.ds(r0, C), cols]
            bcum = _dot(tril, la, HIGHEST)
            b_last = bcum[C - 1:C, :]
            s_rows = [[jnp.zeros((GLA_SUB, C), F32)] for _ in range(2)]
            for sb in range(1, nsub):
                lo = sb * GLA_SUB
                bref = bcum[lo - 1:lo, :]
                q_sb = (qc[lo:lo + GLA_SUB] * jnp.exp(bcum[lo:lo + GLA_SUB] - bref))
                k_sb = (kc * jnp.exp(jnp.minimum(bref - bcum, 0.0))).astype(BF16)
                for h2 in range(2):
                    qm = jnp.where(hmask[h2], q_sb, 0.0).astype(BF16)
                    s_rows[h2].append(_dot_nt(qm, k_sb))
            scores = []
            for h2 in range(2):
                s_off = jnp.concatenate(s_rows[h2], axis=0)
                scores.append(jnp.where((ri // GLA_SUB) > (ci // GLA_SUB), s_off, 0.0))
            for off in range(GLA_SUB):
                if off == 0:
                    prod = qc * kc
                else:
                    k_sh = pltpu.roll(kc, off, 0)
                    b_sh = pltpu.roll(bcum, off, 0)
                    prod = qc * k_sh * jnp.exp(jnp.minimum(bcum - b_sh, 0.0))
                p16 = prod.astype(BF16)
                on_diag = ((ri - ci) == off) & ((ri // GLA_SUB) == (ci // GLA_SUB))
                for h2 in range(2):
                    d = _dot(p16, gsum[h2])
                    scores[h2] = jnp.where(on_diag, d, scores[h2])
            q_dec = qc * jnp.exp(bcum)
            k_dec = kc * jnp.exp(b_last - bcum)
            dec_last = jnp.exp(b_last)
            for h2 in range(2):
                hd = pair * 2 + h2
                vc = v_ref[pl.ds(r0, C), pl.ds(hd * GLA_DV, GLA_DV)].astype(BF16)
                st = state_ref[hd]
                qd = jnp.where(hmask[h2], q_dec, 0.0).astype(BF16)
                kd = jnp.where(hmask[h2], k_dec, 0.0).astype(BF16)
                o = _dot(qd, st.astype(BF16)) + _dot(scores[h2].astype(BF16), vc)
                dl_col = jnp.sum(jnp.where(_iota((LANES, LANES), 0) == _iota((LANES, LANES), 1),
                                           jnp.broadcast_to(dec_last, (LANES, LANES)), 0.0),
                                 axis=-1, keepdims=True)
                state_ref[hd] = st * dl_col + _dot_tn(kd, vc)
                os_ref[pl.ds(r0, C), pl.ds(hd * GLA_DV, GLA_DV)] = o
        return carry

    lax.fori_loop(0, nchunk, chunk_step, 0, unroll=2)

    for hd in range(GLA_HEADS):
        cols = pl.ds(hd * GLA_DV, GLA_DV)
        o = os_ref[:, cols]
        ms = jnp.mean(o * o, axis=-1, keepdims=True)
        y = o * lax.rsqrt(ms + RMS_EPS) * gn_ref[...]
        o_ref[:, cols] = (y * _silu(r_ref[:, cols])).astype(o_ref.dtype)


def gla_mixer(proj, gate_up, gate_bias, gnorm, *, batch, seq, rows):
    T = batch * seq
    nblk = seq // rows
    QW = GLA_HEADS * GLA_DK
    VW = GLA_HEADS * GLA_DV
    row = lambda b, i: b * nblk + i
    wg = jnp.zeros((LANES, QW), F32).at[8:8 + GLA_RANK].set(gate_up).astype(BF16)
    return pl.pallas_call(
        functools.partial(_gla_body, rows=rows),
        grid=(batch, nblk),
        in_specs=[pl.BlockSpec((rows, QW), lambda b, i: (row(b, i), COL_CQ // QW)),
                  pl.BlockSpec((rows, QW), lambda b, i: (row(b, i), COL_CK // QW)),
                  pl.BlockSpec((rows, VW), lambda b, i: (row(b, i), COL_CV // VW)),
                  pl.BlockSpec((rows, VW), lambda b, i: (row(b, i), COL_CR // VW)),
                  pl.BlockSpec((rows, LANES), lambda b, i: (row(b, i), COL_SMALL // LANES)),
                  pl.BlockSpec((LANES, QW), lambda b, i: (0, 0)),
                  pl.BlockSpec((1, QW), lambda b, i: (0, 0)),
                  pl.BlockSpec((1, GLA_DV), lambda b, i: (0, 0))],
        out_specs=pl.BlockSpec((rows, VW), lambda b, i: (row(b, i), 0)),
        out_shape=jax.ShapeDtypeStruct((T, VW), BF16),
        scratch_shapes=[pltpu.VMEM((GLA_HEADS, LANES, GLA_DV), F32),
                        pltpu.VMEM((rows, QW), F32), pltpu.VMEM((rows, QW), F32),
                        pltpu.VMEM((rows, QW), F32), pltpu.VMEM((rows, VW), F32)],
        compiler_params=_cparams("parallel", "arbitrary"),
        name="gla_mixer",
    )(proj, proj, proj, proj, proj, wg, gate_bias.reshape(1, QW), gnorm.reshape(1, GLA_DV))


def _mix_out_body(x_ref, g0_ref, g1_ref, g2_ref, gb_ref, ya_ref, yc_ref,
                  o0_ref, o1_ref, o2_ref, l0_ref, l1_ref, l2_ref,
                  wa_ref, wb_ref, wc_ref, wo_ref, out_ref):
    l0, l1, l2 = l0_ref[...], l1_ref[...], l2_ref[...]
    m = jnp.maximum(jnp.maximum(l0, l1), l2)
    e0, e1, e2 = jnp.exp(l0 - m), jnp.exp(l1 - m), jnp.exp(l2 - m)
    ob = (e0 * o0_ref[...] + e1 * o1_ref[...] + e2 * o2_ref[...]) / (e0 + e1 + e2)
    gb = gb_ref[...]
    y = _sigmoid(g0_ref[...] + gb[0:1]) * _dot(ya_ref[...], wa_ref[...])
    y = y + _sigmoid(g1_ref[...] + gb[1:2]) * _dot(ob.astype(BF16), wb_ref[...])
    y = y + _sigmoid(g2_ref[...] + gb[2:3]) * _dot(yc_ref[...], wc_ref[...])
    out_ref[...] = x_ref[...] + _dot(y.astype(BF16), wo_ref[...])


def mix_out(x, proj, gate_bias, ya, yc, o_grp, lse_grp, wa, wb, wc, wo, *, tm):
    T, D = x.shape
    W = SWA_HEADS * SWA_DH
    rowblk = lambda w: pl.BlockSpec((tm, w), lambda i: (i, 0))
    full = lambda a: pl.BlockSpec(a.shape, lambda i: (0, 0))
    gate = lambda n: pl.BlockSpec((tm, D), lambda i: (i, COL_GATES // D + n))
    return pl.pallas_call(
        _mix_out_body,
        grid=(T // tm,),
        in_specs=[rowblk(D), gate(0), gate(1), gate(2), full(gate_bias), rowblk(ya.shape[1]),
                  rowblk(yc.shape[1]), rowblk(W), rowblk(W), rowblk(W), rowblk(W), rowblk(W), rowblk(W),
                  full(wa), full(wb), full(wc), full(wo)],
        out_specs=rowblk(D),
        out_shape=jax.ShapeDtypeStruct((T, D), F32),
        compiler_params=_cparams("parallel"),
        name="mix_out",
    )(x, proj, proj, proj, gate_bias, ya, yc, *o_grp, *lse_grp, wa, wb, wc, wo)


def _cross_attn_body(x_ref, gx_ref, wq_ref, kv_ref, gq_ref, gk_ref, wo_ref, out_ref):
    x = x_ref[...]
    ms = jnp.mean(x * x, axis=-1, keepdims=True)
    h = (x * lax.rsqrt(ms + RMS_EPS) * gx_ref[...]).astype(BF16)
    q = _dot(h, wq_ref[...])
    kv = kv_ref[...]
    KW = XA_HEADS * XA_DH
    outs = []
    for hd in range(XA_HEADS):
        qh = q[:, hd * XA_DH:(hd + 1) * XA_DH]
        kh = kv[:, hd * XA_DH:(hd + 1) * XA_DH]
        vh = kv[:, KW + hd * XA_DH:KW + (hd + 1) * XA_DH]
        qn = qh * lax.rsqrt(jnp.mean(qh * qh, axis=-1, keepdims=True) + RMS_EPS) * gq_ref[...]
        kn = kh * lax.rsqrt(jnp.mean(kh * kh, axis=-1, keepdims=True) + RMS_EPS) * gk_ref[...]
        s = _dot_nt(qn.astype(BF16), kn.astype(BF16)) * (XA_DH ** -0.5)
        m = jnp.max(s, axis=-1, keepdims=True)
        p = jnp.exp(s - m)
        l = jnp.sum(p, axis=-1, keepdims=True)
        outs.append((_dot(p.astype(BF16), vh.astype(BF16)) / l).astype(BF16))
    o = jnp.concatenate(outs, axis=-1)
    out_ref[...] = x + _dot(o, wo_ref[...])


def cross_attention(x, kv, gx, wq, gq, gk, wo, *, batch, seq, mem_len, tm):
    T, D = x.shape
    per_batch = seq // tm
    full = lambda a: pl.BlockSpec(a.shape, lambda i: (0, 0))
    gx, gq, gk = gx.reshape(1, D), gq.reshape(1, XA_DH), gk.reshape(1, XA_DH)
    return pl.pallas_call(
        _cross_attn_body,
        grid=(T // tm,),
        in_specs=[pl.BlockSpec((tm, D), lambda i: (i, 0)), full(gx), full(wq),
                  pl.BlockSpec((mem_len, kv.shape[1]), lambda i: (i // per_batch, 0)),
                  full(gq), full(gk), full(wo)],
        out_specs=pl.BlockSpec((tm, D), lambda i: (i, 0)),
        out_shape=jax.ShapeDtypeStruct((T, D), F32),
        compiler_params=_cparams("parallel"),
        name="cross_attention",
    )(x, gx, wq, kv, gq, gk, wo)


def _router_body(x_ref, gx_ref, wr_ref, br_ref, h_ref, idx_ref, gate_ref):
    x = x_ref[...]
    ms = jnp.mean(x * x, axis=-1, keepdims=True)
    h = x * lax.rsqrt(ms + RMS_EPS) * gx_ref[...]
    _store_row_tiles(h_ref, h)
    lane = _iota((1, LANES), 1)
    logits = _dot(h, wr_ref[...], HIGHEST) + br_ref[...]
    logits = jnp.where(lane < N_EXPERTS, logits, -jnp.inf)
    idx_out = jnp.zeros(logits.shape, jnp.int32)
    val_out = jnp.full(logits.shape, -jnp.inf, F32)
    for k in range(TOP_K):
        m = jnp.max(logits, axis=-1, keepdims=True)
        sel = jnp.min(jnp.where(logits == m, lane, LANES), axis=-1, keepdims=True)
        idx_out = jnp.where(lane == k, sel, idx_out)
        val_out = jnp.where(lane == k, m, val_out)
        logits = jnp.where(lane == sel, -jnp.inf, logits)
    top = jnp.max(val_out, axis=-1, keepdims=True)
    e = jnp.exp(val_out - top)
    idx_ref[...] = idx_out
    gate_ref[...] = e / jnp.sum(e, axis=-1, keepdims=True)


def moe_router(x, gx, wr, br, *, tm):
    T, D = x.shape
    wr_p = jnp.zeros((D, LANES), F32).at[:, :N_EXPERTS].set(wr)
    br_p = jnp.zeros((1, LANES), F32).at[0, :N_EXPERTS].set(br)
    full = lambda a: pl.BlockSpec(a.shape, lambda i: (0, 0))
    gx = gx.reshape(1, D)
    return pl.pallas_call(
        _router_body,
        grid=(T // tm,),
        in_specs=[pl.BlockSpec((tm, D), lambda i: (i, 0)), full(gx), full(wr_p), full(br_p)],
        out_specs=[pl.BlockSpec((tm * ROW_TILE, LANES), lambda i: (i, 0)),
                   pl.BlockSpec((tm, LANES), lambda i: (i, 0)),
                   pl.BlockSpec((tm, LANES), lambda i: (i, 0))],
        out_shape=[jax.ShapeDtypeStruct((T * ROW_TILE, LANES), F32),
                   jax.ShapeDtypeStruct((T, LANES), jnp.int32),
                   jax.ShapeDtypeStruct((T, LANES), F32)],
        compiler_params=_cparams("parallel"),
        name="moe_router",
    )(x, gx, wr_p, br_p)


def _expert_body(be_ref, nused_ref, tok0_ref, tokn_ref, dstp_ref, dstc_ref, h_ref, wi_ref, bi_ref, wo_ref, bo_ref,
                 out_ref, xbuf, ybuf, wi16, wo16, gsem, ssem, *, dump_base):
    i = pl.program_id(0)
    nu = nused_ref[0]
    slot = i % 2

    @pl.when((i == 0) | (be_ref[i] != be_ref[jnp.maximum(i - 1, 0)]))
    def _():
        wi16[...] = wi_ref[...].astype(BF16)
        wo16[...] = wo_ref[...].astype(BF16)

    def tile_rows(start):
        return pl.ds(pl.multiple_of(start, ROW_TILE), ROW_TILE)

    def gather(tok_ref, s, j):
        return pltpu.make_async_copy(h_ref.at[tile_rows(tok_ref[0, 0, j])], xbuf.at[s, tile_rows(j * ROW_TILE)],
                                     gsem.at[s])

    def scatter(dst_ref, s, j):
        return pltpu.make_async_copy(ybuf.at[s, tile_rows(j * ROW_TILE)], out_ref.at[tile_rows(dst_ref[0, 0, j])],
                                     ssem.at[s])

    def for_rows(fn):
        def body(j, carry):
            fn(j)
            return carry
        lax.fori_loop(0, MOE_BM, body, 0, unroll=8)

    def wait_gathered(s):
        pltpu.make_async_copy(xbuf.at[s], xbuf.at[s], gsem.at[s]).wait()

    def wait_scattered(s):
        pltpu.make_async_copy(ybuf.at[s], ybuf.at[s], ssem.at[s]).wait()

    @pl.when(i == 0)
    def _():
        for_rows(lambda j: gather(tok0_ref, 0, j).start())
        ybuf[...] = jnp.zeros_like(ybuf)
        pltpu.make_async_copy(ybuf.at[0], out_ref.at[pl.ds(dump_base * ROW_TILE, MOE_BM * ROW_TILE)],
                              ssem.at[0]).start()

    def step(s):
        wait_gathered(s)
        for j in range(MOE_BM):
            gather(tokn_ref, 1 - s, j).start()
            scatter(dstp_ref, 1 - s, j).start()
        hh = _dot(_load_row_tiles(xbuf.at[s], MOE_BM).astype(BF16), wi16[...]) + bi_ref[...]
        glu = jnp.minimum(hh[:, :D_EXPERT], SWIGLU_LIMIT)
        lin = jnp.clip(hh[:, D_EXPERT:], -SWIGLU_LIMIT, SWIGLU_LIMIT)
        act = glu * _sigmoid(SWIGLU_ALPHA * glu) * (lin + 1.0)
        y = _dot(act.astype(BF16), wo16[...]) + bo_ref[...]
        wait_scattered(s)
        _store_row_tiles(ybuf.at[s], y)

        @pl.when(i == nu - 1)
        def _():
            for_rows(lambda j: scatter(dstc_ref, s, j).start())
            wait_scattered(s)
            wait_scattered(1 - s)
            wait_gathered(1 - s)

    for s in range(2):
        pl.when((i < nu) & (slot == s))(functools.partial(step, s))


def expert_ffn(h, tok_buf, slot_dst, block_expert, n_used, w_in, b_in, w_out, b_out, *, layer):
    T, D = h.shape[0] // ROW_TILE, D_MODEL
    P = tok_buf.shape[0]
    nb = P // MOE_BM
    L, E, _, F2 = w_in.shape
    dump_base = TOP_K * T
    dst_ext = jnp.concatenate([dump_base + MOE_BM + jnp.arange(MOE_BM, dtype=jnp.int32), slot_dst]) * ROW_TILE
    dst_ext = dst_ext.reshape(nb + 1, 1, MOE_BM)
    idx_blk = lambda f: pl.BlockSpec((1, 1, MOE_BM), f, memory_space=pltpu.SMEM)
    grid_spec = pltpu.PrefetchScalarGridSpec(
        num_scalar_prefetch=2,
        grid=(nb,),
        in_specs=[idx_blk(lambda i, be, nu: (i, 0, 0)),
                  idx_blk(lambda i, be, nu: (jnp.minimum(i + 1, nb - 1), 0, 0)),
                  idx_blk(lambda i, be, nu: (i, 0, 0)),
                  idx_blk(lambda i, be, nu: (i + 1, 0, 0)),
                  pl.BlockSpec(memory_space=pl.ANY),
                  pl.BlockSpec((None, None, D, F2), lambda i, be, nu: (layer, be[i], 0, 0)),
                  pl.BlockSpec((None, None, 1, F2), lambda i, be, nu: (layer, be[i], 0, 0)),
                  pl.BlockSpec((None, None, F2 // 2, D), lambda i, be, nu: (layer, be[i], 0, 0)),
                  pl.BlockSpec((None, None, 1, D), lambda i, be, nu: (layer, be[i], 0, 0))],
        out_specs=pl.BlockSpec(memory_space=pl.ANY),
        scratch_shapes=[pltpu.VMEM((2, MOE_BM * ROW_TILE, LANES), F32), pltpu.VMEM((2, MOE_BM * ROW_TILE, LANES), F32),
                        pltpu.VMEM((D, F2), BF16), pltpu.VMEM((F2 // 2, D), BF16),
                        pltpu.SemaphoreType.DMA((2,)), pltpu.SemaphoreType.DMA((2,))],
    )
    tok3 = (tok_buf * ROW_TILE).reshape(nb, 1, MOE_BM)
    return pl.pallas_call(
        functools.partial(_expert_body, dump_base=dump_base),
        grid_spec=grid_spec,
        out_shape=jax.ShapeDtypeStruct(((dump_base + 2 * MOE_BM) * ROW_TILE, LANES), F32),
        compiler_params=_cparams("arbitrary"),
        name="expert_ffn",
    )(block_expert, n_used, tok3, tok3, dst_ext, dst_ext, h,
      w_in, b_in.reshape(L, E, 1, F2), w_out, b_out.reshape(L, E, 1, D))


def _moe_combine_body(x_ref, y0_ref, y1_ref, y2_ref, y3_ref, gate_ref, out_ref):
    g = gate_ref[...]
    acc = x_ref[...]
    for k, y_ref in enumerate((y0_ref, y1_ref, y2_ref, y3_ref)):
        acc = acc + g[:, k:k + 1] * _load_row_tiles(y_ref, x_ref.shape[0])
    out_ref[...] = acc


def moe_combine(x, y4, gates, *, tm):
    T, D = x.shape
    nblk = T // tm
    ysp = lambda k: pl.BlockSpec((tm * ROW_TILE, LANES), lambda i: (k * nblk + i, 0))
    return pl.pallas_call(
        _moe_combine_body,
        grid=(nblk,),
        in_specs=[pl.BlockSpec((tm, D), lambda i: (i, 0)), ysp(0), ysp(1), ysp(2), ysp(3),
                  pl.BlockSpec((tm, LANES), lambda i: (i, 0))],
        out_specs=pl.BlockSpec((tm, D), lambda i: (i, 0)),
        out_shape=jax.ShapeDtypeStruct((T, D), F32),
        compiler_params=_cparams("parallel"),
        name="moe_combine",
    )(x, y4, y4, y4, y4, gates)


def _pack_w_in(w_in):
    L, D, _ = w_in.shape
    o_alpha = 2048
    o_b = 2056
    o_c = o_b + 2304
    o_low = o_c + 1536
    o_gates = o_low + GLA_RANK
    c = w_in[:, :, o_c:o_low]
    parts = [w_in[:, :, o_gates:],
             w_in[:, :, 0:2048],
             w_in[:, :, o_b:o_c],
             w_in[:, :, o_alpha:o_b], w_in[:, :, o_low:o_gates],
             jnp.zeros((L, D, 256 - 8 - GLA_RANK), w_in.dtype),
             c[:, :, 512:1024], c[:, :, 1024:1536], c[:, :, 0:256], c[:, :, 256:512]]
    packed = jnp.concatenate(parts, axis=-1)
    assert packed.shape[-1] == N_PACKED
    return packed.astype(BF16)


def _to_residue_major(t, dilation):
    if dilation == 1:
        return t
    *lead, S, W = t.shape
    return jnp.swapaxes(t.reshape(*lead, S // dilation, dilation, W), -2, -3).reshape(*lead, S, W)


def _from_residue_major(t, dilation):
    if dilation == 1:
        return t
    *lead, S, W = t.shape
    return jnp.swapaxes(t.reshape(*lead, dilation, S // dilation, W), -2, -3).reshape(*lead, S, W)


def _routing_tables(idx, n_tokens):
    A = n_tokens * TOP_K
    P = A + N_EXPERTS * MOE_BM
    e_flat = idx.reshape(A)
    onehot = (e_flat[:, None] == jnp.arange(N_EXPERTS, dtype=jnp.int32)[None, :]).astype(jnp.int32)
    csum = jnp.cumsum(onehot, axis=0)
    rank = jnp.sum(onehot * csum, axis=1) - 1
    counts = csum[-1]
    padded = (counts + MOE_BM - 1) // MOE_BM * MOE_BM
    pad_ends = jnp.cumsum(padded)
    pad_starts = pad_ends - padded
    dest = (pad_starts[e_flat] + rank).astype(jnp.int32)
    slot_a = jnp.full((P,), -1, jnp.int32).at[dest].set(jnp.arange(A, dtype=jnp.int32))
    tok_buf = jnp.maximum(slot_a, 0) // TOP_K
    p = jnp.arange(P, dtype=jnp.int32)
    dump = TOP_K * n_tokens + ((p // MOE_BM) % 2) * MOE_BM + p % MOE_BM
    slot_dst = jnp.where(slot_a < 0, dump, (slot_a % TOP_K) * n_tokens + slot_a // TOP_K)
    block_start = jnp.arange(P // MOE_BM, dtype=jnp.int32) * MOE_BM
    block_expert = jnp.minimum(jnp.sum((pad_ends[None, :] <= block_start[:, None]).astype(jnp.int32), axis=1),
                               N_EXPERTS - 1)
    n_used = (pad_ends[-1:] // MOE_BM).astype(jnp.int32)
    return tok_buf, slot_dst, block_expert, n_used


def kernel(x, mem, positions, norm_mix, w_in, gate_bias, gdn_conv, gdn_a_log, gdn_dt_bias, gdn_norm, swa_q_norm, swa_k_norm, gla_gate_up, gla_gate_bias, gla_norm, w_branch_a, w_branch_b, w_branch_c, w_mix_out, norm_cross, norm_mem, xa_wq, xa_wkv, xa_q_norm, xa_k_norm, xa_wo, norm_ffn, router_w, router_b, moe_w_in, moe_b_in, moe_w_out, moe_b_out):
    B, S, D = x.shape
    T = B * S
    M = mem.shape[1]
    depth = w_in.shape[0]
    W = SWA_HEADS * SWA_DH

    inv_freq = ROPE_THETA ** (-jnp.arange(0, SWA_DH, 2, dtype=F32) / SWA_DH)
    ang = positions.astype(F32).reshape(T, 1) * inv_freq[None, :]
    cos, sin = jnp.cos(ang), jnp.sin(ang)
    cosf = jnp.tile(jnp.concatenate([cos, cos], axis=-1), (1, SWA_HEADS))
    sins = jnp.tile(jnp.concatenate([-sin, sin], axis=-1), (1, SWA_HEADS))

    w_in_p = _pack_w_in(w_in)
    xf = x.reshape(T, D)
    memf = mem.reshape(B * M, D)

    for l in range(depth):
        proj = norm_matmul(xf, norm_mix[l], w_in_p[l], tm=1024, tn=1536, name="in_proj")
        ya = gdn_mixer(proj, gdn_conv[l], gdn_a_log[l], gdn_dt_bias[l], gdn_norm[l], batch=B, seq=S, rows=512)
        qn, kn, vn = qk_norm_rope(proj, cosf, sins, swa_q_norm[l], swa_k_norm[l], tm=1024)
        o_grp, lse_grp = [], []
        for gi, (window, dil) in enumerate(SWA_GROUPS):
            assert window // dil == SWA_BACK
            rm = lambda t: _to_residue_major(t.reshape(B, S, W), dil)
            o_g, lse_g = band_attention(rm(qn[gi]), rm(kn[gi]), rm(vn[gi]), dilation=dil)
            o_grp.append(_from_residue_major(o_g, dil).reshape(T, W))
            lse_grp.append(_from_residue_major(lse_g, dil).reshape(T, W))
        yc = gla_mixer(proj, gla_gate_up[l], gla_gate_bias[l], gla_norm[l], batch=B, seq=S, rows=512)
        xf = mix_out(xf, proj, gate_bias[l], ya, yc, o_grp, lse_grp,
                     w_branch_a[l].astype(BF16), w_branch_b[l].astype(BF16), w_branch_c[l].astype(BF16),
                     w_mix_out[l].astype(BF16), tm=512)

        kv = norm_matmul(memf, norm_mem[l], xa_wkv[l].astype(BF16), tm=min(1024, B * M), tn=1024, name="mem_kv")
        xf = cross_attention(xf, kv, norm_cross[l], xa_wq[l].astype(BF16), xa_q_norm[l], xa_k_norm[l],
                             xa_wo[l].astype(BF16), batch=B, seq=S, mem_len=M, tm=512)

        h, idx, gates = moe_router(xf, norm_ffn[l], router_w[l], router_b[l], tm=1024)
        tok_buf, slot_dst, block_expert, n_used = _routing_tables(idx[:, :TOP_K], T)
        y4 = expert_ffn(h, tok_buf, slot_dst, block_expert, n_used, moe_w_in, moe_b_in, moe_w_out, moe_b_out,
                        layer=l)
        xf = moe_combine(xf, y4, gates, tm=512)

    return xf.reshape(B, S, D)
```

```python
import functools

import jax
import jax.numpy as jnp
from jax import lax
from jax.experimental import pallas as pl
from jax.experimental.pallas import tpu as pltpu

F32 = jnp.float32
BF16 = jnp.bfloat16
HIGHEST = lax.Precision.HIGHEST

RMS_EPS = 1e-6
L2_EPS = 1e-6
LANES = 128
VMEM_LIMIT = 56 * 1024 * 1024

D_MODEL = 1024
GDN_HEADS, GDN_D, GDN_CONV, CHUNK = 4, 128, 4, 64
SWA_GROUPS = ((128, 1), (512, 4), (2048, 16))
SWA_HEADS, SWA_DH, SWA_BACK = 4, 64, 128
ROPE_THETA = 10000.0
GLA_HEADS, GLA_DK, GLA_DV, GLA_RANK, GLA_TAU = 4, 64, 128, 16, 16.0
GLA_SUB = 8
XA_HEADS, XA_DH = 4, 128
N_EXPERTS, TOP_K, D_EXPERT = 32, 4, 1024
SWIGLU_ALPHA, SWIGLU_LIMIT = 1.702, 7.0
MOE_BM = 256

COL_GATES = 0
COL_AQ, COL_AK, COL_AV, COL_AZ = 3072, 3584, 4096, 4608
COL_BQ, COL_BK, COL_BV = 5120, 5888, 6656
COL_SMALL = 7424
COL_CV, COL_CR, COL_CQ, COL_CK = 7680, 8192, 8704, 8960
N_PACKED = 9216


def _cparams(*sem):
    return pltpu.CompilerParams(dimension_semantics=sem, vmem_limit_bytes=VMEM_LIMIT)


def _sigmoid(x):
    return 1.0 / (1.0 + jnp.exp(-x))


def _silu(x):
    return x * _sigmoid(x)


def _softplus(x):
    return jnp.maximum(x, 0.0) + jnp.log(1.0 + jnp.exp(-jnp.abs(x)))


def _dot(a, b, precision=None):
    return jnp.dot(a, b, preferred_element_type=F32, precision=precision)


def _dot_nt(a, b, precision=None):
    return lax.dot_general(a, b, (((1,), (1,)), ((), ())), preferred_element_type=F32, precision=precision)


def _dot_tn(a, b, precision=None):
    return lax.dot_general(a, b, (((0,), (0,)), ((), ())), preferred_element_type=F32, precision=precision)


def _bdot(a, b):
    return lax.dot_general(a, b, (((2,), (1,)), ((0,), (0,))), preferred_element_type=F32)


def _bdot_nt(a, b):
    return lax.dot_general(a, b, (((2,), (2,)), ((0,), (0,))), preferred_element_type=F32)


def _split_bf16(a):
    ah = a.astype(BF16)
    return ah, (a - ah.astype(F32)).astype(BF16)


def _bdot3(a, b):
    (ah, al), (bh, bl) = a, b
    return _bdot(ah, bh) + (_bdot(ah, bl) + _bdot(al, bh))


def _iota(shape, axis):
    return lax.broadcasted_iota(jnp.int32, shape, axis)


ROW_TILE = D_MODEL // LANES


def _store_row_tiles(ref, x):
    n = x.shape[0]
    for s in range(ROW_TILE):
        ref[pl.ds(s, n, stride=ROW_TILE), :] = x[:, s * LANES:(s + 1) * LANES]


def _load_row_tiles(ref, n):
    return jnp.concatenate([ref[pl.ds(s, n, stride=ROW_TILE), :] for s in range(ROW_TILE)], axis=1)


def _norm_matmul_body(x_ref, g_ref, w_ref, o_ref, h_ref):
    @pl.when(pl.program_id(1) == 0)
    def _():
        x = x_ref[...]
        ms = jnp.mean(x * x, axis=-1, keepdims=True)
        h_ref[...] = (x * lax.rsqrt(ms + RMS_EPS) * g_ref[...]).astype(h_ref.dtype)

    o_ref[...] = _dot(h_ref[...], w_ref[...]).astype(o_ref.dtype)


def norm_matmul(x, gain, w, *, tm, tn, name):
    T, D = x.shape
    N = w.shape[1]
    return pl.pallas_call(
        _norm_matmul_body,
        grid=(T // tm, N // tn),
        in_specs=[pl.BlockSpec((tm, D), lambda i, j: (i, 0)),
                  pl.BlockSpec((1, D), lambda i, j: (0, 0)),
                  pl.BlockSpec((D, tn), lambda i, j: (0, j))],
        out_specs=pl.BlockSpec((tm, tn), lambda i, j: (i, j)),
        out_shape=jax.ShapeDtypeStruct((T, N), F32),
        scratch_shapes=[pltpu.VMEM((tm, D), BF16)],
        compiler_params=_cparams("parallel", "arbitrary"),
        name=name,
    )(x, gain.reshape(1, D), w)


def _gdn_body(q_ref, k_ref, v_ref, z_ref, sm_ref, smt_ref, cq_ref, ck_ref, cv_ref, alog_ref, dtb_ref, gn_ref,
              o_ref, state_ref, tail_ref, u_ref, wq_ref, kd_ref, in_ref, dl_ref, os_ref, *, rows):
    blk = pl.program_id(1)
    nchunk = rows // CHUNK

    @pl.when(blk == 0)
    def _():
        state_ref[...] = jnp.zeros_like(state_ref)
        tail_ref[...] = jnp.zeros_like(tail_ref)

    def conv_silu(x_ref, w_ref, slot):
        x = x_ref[...]
        xp = jnp.concatenate([tail_ref[slot], x], axis=0)
        w = w_ref[...]
        acc = x * w[GDN_CONV - 1:GDN_CONV, :]
        for s in range(1, GDN_CONV):
            acc = acc + pltpu.roll(xp, s, 0)[8:] * w[GDN_CONV - 1 - s:GDN_CONV - s, :]
        tail_ref[slot] = x[rows - 8:rows]
        return _silu(acc)

    q_all = conv_silu(q_ref, cq_ref, 0)
    k_all = conv_silu(k_ref, ck_ref, 1)
    v_all = conv_silu(v_ref, cv_ref, 2)
    sm = sm_ref[...]
    pos_c = _iota((rows, 1), 0) % CHUNK
    pos_r = _iota((1, rows), 1) % CHUNK
    ri = _iota((1, CHUNK, CHUNK), 1)
    ci = _iota((1, CHUNK, CHUNK), 2)
    incl = ri >= ci
    strict = ri > ci
    eye = (ri == ci).astype(F32)
    chunked = lambda t: t.reshape(nchunk, CHUNK, t.shape[-1])

    for hd in range(GDN_HEADS):
        _gdn_chunk_local(hd, q_all, k_all, v_all, sm, smt_ref, alog_ref, dtb_ref, pos_c, pos_r, incl, strict, eye,
                         chunked, u_ref, wq_ref, kd_ref, in_ref, dl_ref, rows=rows)

    def chunk_step(c, carry):
        for hd in range(GDN_HEADS):
            st = state_ref[hd]
            wq_s = _dot(wq_ref[hd, c], st.astype(BF16))
            v16 = (u_ref[hd, c] - wq_s[:CHUNK]).astype(BF16)
            o = wq_s[CHUNK:] + _dot(in_ref[hd, c], v16)
            state_ref[hd] = st * dl_ref[hd, c][:, :1] + _dot_tn(kd_ref[hd, c], v16)
            os_ref[pl.ds(pl.multiple_of(c * CHUNK, CHUNK), CHUNK), pl.ds(hd * GDN_D, GDN_D)] = o
        return carry

    lax.fori_loop(0, nchunk, chunk_step, 0, unroll=True)

    for hd in range(GDN_HEADS):
        cols = pl.ds(hd * GDN_D, GDN_D)
        o = os_ref[:, cols]
        ms = jnp.mean(o * o, axis=-1, keepdims=True)
        y = o * lax.rsqrt(ms + RMS_EPS) * gn_ref[...]
        o_ref[:, cols] = (y * _silu(z_ref[:, cols])).astype(o_ref.dtype)


def _gdn_chunk_local(hd, q_all, k_all, v_all, sm, smt_ref, alog_ref, dtb_ref, pos_c, pos_r, incl, strict, eye,
                     chunked, u_ref, wq_ref, kd_ref, in_ref, dl_ref, *, rows):
    nchunk = rows // CHUNK
    cols = slice(hd * GDN_D, (hd + 1) * GDN_D)
    q, k, v = q_all[:, cols], k_all[:, cols], v_all[:, cols]
    q = q * lax.rsqrt(jnp.sum(q * q, axis=-1, keepdims=True) + L2_EPS) * (GDN_D ** -0.5)
    k = k * lax.rsqrt(jnp.sum(k * k, axis=-1, keepdims=True) + L2_EPS)

    neg_a = -jnp.exp(alog_ref[:, hd:hd + 1])
    dt_b = dtb_ref[:, hd:hd + 1]
    g_col = jnp.broadcast_to(neg_a * _softplus(sm[:, hd:hd + 1] + dt_b), (rows, LANES))
    beta = jnp.broadcast_to(_sigmoid(sm[:, GDN_HEADS + hd:GDN_HEADS + hd + 1]), (rows, LANES))
    g_row = neg_a * _softplus(smt_ref[hd:hd + 1, :] + dt_b)

    step = 1
    while step < CHUNK:
        g_col = g_col + jnp.where(pos_c >= step, pltpu.roll(g_col, step, 0), 0.0)
        g_row = g_row + jnp.where(pos_r >= step, pltpu.roll(g_row, step, 1), 0.0)
        step *= 2
    eg = jnp.exp(g_col)

    q3, k3, g3 = chunked(q), chunked(k), chunked(g_col)
    kb3 = chunked(k * beta)
    eg3 = chunked(eg)
    g_row3 = jnp.stack([g_row[:, c * CHUNK:(c + 1) * CHUNK] for c in range(nchunk)], axis=0)
    decay = jnp.where(incl, jnp.exp(jnp.minimum(g3[:, :, :CHUNK] - g_row3, 0.0)), 0.0)
    kq = jnp.concatenate([kb3, q3], axis=1).astype(BF16)
    s = _bdot_nt(kq, k3.astype(BF16))
    lower = jnp.where(strict, s[:, :CHUNK] * decay, 0.0)
    intra = jnp.where(incl, s[:, CHUNK:] * decay, 0.0)
    pw = -lower
    inv = eye + pw
    pw_s = _split_bf16(pw)
    pw = _bdot3(pw_s, pw_s)
    for _ in range(4):
        inv_s, pw_s = _split_bf16(inv), _split_bf16(pw)
        stacked = tuple(jnp.concatenate([i_, p_], axis=1) for i_, p_ in zip(inv_s, pw_s))
        both = _bdot3(stacked, pw_s)
        inv = inv + both[:, :CHUNK]
        pw = both[:, CHUNK:]
    inv = inv + _bdot3(_split_bf16(inv), _split_bf16(pw))
    rhs = jnp.concatenate([chunked(v * beta), kb3 * eg3], axis=2)
    uw = _bdot3(_split_bf16(inv), _split_bf16(rhs))
    g_last = g3[:, CHUNK - 1:CHUNK, :]
    u_ref[hd] = uw[:, :, :GDN_D]
    wq_ref[hd] = jnp.concatenate([uw[:, :, GDN_D:], q3 * eg3], axis=1).astype(BF16)
    kd_ref[hd] = (k3 * jnp.exp(g_last - g3)).astype(BF16)
    in_ref[hd] = intra.astype(BF16)
    dl_ref[hd] = jnp.exp(g_last)


def gdn_mixer(proj, conv_w, a_log, dt_bias, gnorm, *, batch, seq, rows):
    T = batch * seq
    nblk = seq // rows
    nchunk = rows // CHUNK
    H = GDN_HEADS
    HW = H * GDN_D
    small_t = proj[:, COL_SMALL:COL_SMALL + 2 * H].T
    cb = lambda base: pl.BlockSpec((rows, HW), lambda b, i: (b * nblk + i, base // HW))
    pad = lambda v: jnp.zeros((1, LANES), F32).at[0, :H].set(v)
    wspec = lambda n: pl.BlockSpec((GDN_CONV, HW), lambda b, i: (0, n))
    vec = pl.BlockSpec((1, LANES), lambda b, i: (0, 0))
    return pl.pallas_call(
        functools.partial(_gdn_body, rows=rows),
        grid=(batch, nblk),
        in_specs=[cb(COL_AQ), cb(COL_AK), cb(COL_AV), cb(COL_AZ),
                  pl.BlockSpec((rows, LANES), lambda b, i: (b * nblk + i, COL_SMALL // LANES)),
                  pl.BlockSpec((2 * H, rows), lambda b, i: (0, b * nblk + i)),
                  wspec(0), wspec(1), wspec(2), vec, vec, vec],
        out_specs=pl.BlockSpec((rows, HW), lambda b, i: (b * nblk + i, 0)),
        out_shape=jax.ShapeDtypeStruct((T, HW), BF16),
        scratch_shapes=[pltpu.VMEM((H, GDN_D, GDN_D), F32),
                        pltpu.VMEM((3, 8, HW), F32),
                        pltpu.VMEM((H, nchunk, CHUNK, GDN_D), F32),
                        pltpu.VMEM((H, nchunk, 2 * CHUNK, GDN_D), BF16),
                        pltpu.VMEM((H, nchunk, CHUNK, GDN_D), BF16),
                        pltpu.VMEM((H, nchunk, CHUNK, CHUNK), BF16),
                        pltpu.VMEM((H, nchunk, 1, LANES), F32),
                        pltpu.VMEM((rows, HW), F32)],
        compiler_params=_cparams("parallel", "arbitrary"),
        name="gdn_mixer",
    )(proj, proj, proj, proj, proj, small_t, conv_w, conv_w, conv_w, pad(a_log), pad(dt_bias),
      gnorm.reshape(1, GDN_D))


def _qk_rope_body(*refs):
    ng = len(SWA_GROUPS)
    q_refs, k_refs = refs[0:ng], refs[ng:2 * ng]
    cos_ref, sin_ref, gq_ref, gk_ref, gm_ref = refs[2 * ng:2 * ng + 5]
    outs = refs[2 * ng + 5:]
    qo_refs, ko_refs = outs[0:ng], outs[ng:2 * ng]
    lane = _iota((1, 2 * LANES), 1)
    first_half = (lane % SWA_DH) < (SWA_DH // 2)
    cosf = cos_ref[...]
    sins = sin_ref[...]

    def norm_rope(x, gain):
        sq = x * x
        hi = sq.astype(BF16)
        lo = (sq - hi.astype(F32)).astype(BF16)
        ms = _dot(hi, gm_ref[...]) + _dot(lo, gm_ref[...])
        xn = x * lax.rsqrt(ms + RMS_EPS) * gain
        other = jnp.where(first_half, pltpu.roll(xn, 2 * LANES - SWA_DH // 2, 1),
                          pltpu.roll(xn, SWA_DH // 2, 1))
        return xn * cosf + other * sins

    for g in range(ng):
        qo_refs[g][...] = norm_rope(q_refs[g][...], gq_ref[...]) * (SWA_DH ** -0.5)
        ko_refs[g][...] = norm_rope(k_refs[g][...], gk_ref[...])


def qk_norm_rope(proj, cosf, sins, q_gain, k_gain, *, tm):
    T = proj.shape[0]
    W = SWA_HEADS * SWA_DH
    ng = len(SWA_GROUPS)
    grp = jnp.arange(W) // SWA_DH
    gmean = ((grp[:, None] == grp[None, :]).astype(F32) / SWA_DH).astype(BF16)
    tile = lambda v: jnp.tile(v, SWA_HEADS).reshape(1, W)
    vec = pl.BlockSpec((1, W), lambda i: (0, 0))
    col = lambda base, g: pl.BlockSpec((tm, W), lambda i: (i, base // W + g))
    row = pl.BlockSpec((tm, W), lambda i: (i, 0))
    out = jax.ShapeDtypeStruct((T, W), F32)
    res = pl.pallas_call(
        _qk_rope_body,
        grid=(T // tm,),
        in_specs=([col(COL_BQ, g) for g in range(ng)] + [col(COL_BK, g) for g in range(ng)]
                  + [row, row, vec, vec, pl.BlockSpec((W, W), lambda i: (0, 0))]),
        out_specs=[row] * (2 * ng),
        out_shape=[out] * (2 * ng),
        compiler_params=_cparams("parallel"),
        name="qk_norm_rope",
    )(*([proj] * (2 * ng)), cosf, sins, tile(q_gain), tile(k_gain), gmean)
    return res[0:ng], res[ng:2 * ng]


SWA_TILE = 2 * SWA_BACK


def _band_attn_body(q_ref, k_ref, v_ref, o_ref, lse_ref, *, dilation):
    d = dilation
    c = SWA_BACK
    seq = q_ref.shape[0]
    ntile = seq // d // SWA_TILE
    lane = _iota((1, LANES), 1)

    def band(nkeys, lead):
        dist = _iota((SWA_TILE, nkeys), 0) + lead - _iota((SWA_TILE, nkeys), 1)
        return (dist >= 0) & (dist <= SWA_BACK)

    valid_first = band(SWA_TILE, 0)
    valid_next = band(c + SWA_TILE, c)

    def rows_of(start, n):
        return pl.ds(start, n) if d == 1 else pl.ds(start, n, stride=d)

    for r in range(d):
        for j in range(ntile):
            start = r + d * SWA_TILE * j
            q = q_ref[rows_of(start, SWA_TILE), :]
            if j == 0:
                keys, valid = rows_of(start, SWA_TILE), valid_first
            else:
                keys, valid = rows_of(start - d * c, c + SWA_TILE), valid_next
            kk = k_ref[keys, :].astype(BF16)
            vv = v_ref[keys, :].astype(BF16)
            o_acc = jnp.zeros((SWA_TILE, LANES), F32)
            lse_acc = jnp.zeros((SWA_TILE, LANES), F32)
            for h2 in range(LANES // SWA_DH):
                hmask = (lane // SWA_DH) == h2
                qh = jnp.where(hmask, q, 0.0).astype(BF16)
                s = jnp.where(valid, _dot_nt(qh, kk), -jnp.inf)
                m = jnp.max(s, axis=-1, keepdims=True)
                p = jnp.exp(s - m)
                l = jnp.sum(p, axis=-1, keepdims=True)
                pv = _dot(p.astype(BF16), vv) / l
                o_acc = jnp.where(hmask, pv, o_acc)
                lse_acc = jnp.where(hmask, m + jnp.log(l), lse_acc)
            o_ref[rows_of(start, SWA_TILE), :] = o_acc
            lse_ref[rows_of(start, SWA_TILE), :] = lse_acc


def band_attention(q, k, proj, *, group, batch, seq, dilation):
    T, W = q.shape
    assert (seq // dilation) % SWA_TILE == 0
    blk = pl.BlockSpec((seq, LANES), lambda b, hp: (b, hp))
    v_col = (COL_BV + group * W) // LANES
    out = jax.ShapeDtypeStruct((T, W), F32)
    return pl.pallas_call(
        functools.partial(_band_attn_body, dilation=dilation),
        grid=(batch, W // LANES),
        in_specs=[blk, blk, pl.BlockSpec((seq, LANES), lambda b, hp: (b, v_col + hp))],
        out_specs=[blk, blk],
        out_shape=[out, out],
        compiler_params=_cparams("parallel", "parallel"),
        name=f"band_attention_d{dilation}",
    )(q, k, proj)


def _gla_body(q_ref, k_ref, v_ref, r_ref, sm_ref, wg_ref, bg_ref, gn_ref, o_ref,
              state_ref, qs_ref, ks_ref, la_ref, os_ref, *, rows):
    blk = pl.program_id(1)
    nchunk = rows // CHUNK
    C = CHUNK

    @pl.when(blk == 0)
    def _():
        state_ref[...] = jnp.zeros_like(state_ref)

    x = _dot(sm_ref[...].astype(BF16), wg_ref[...]) + bg_ref[...]
    la_ref[...] = (jnp.minimum(x, 0.0) - jnp.log(1.0 + jnp.exp(-jnp.abs(x)))) * (1.0 / GLA_TAU)
    qs_ref[...] = q_ref[...] * (GLA_DK ** -0.5)
    ks_ref[...] = k_ref[...]

    ri = _iota((C, C), 0)
    ci = _iota((C, C), 1)
    tril = (ri >= ci).astype(F32)
    lane = _iota((1, LANES), 1)
    lrow = _iota((LANES, C), 0)
    gsum = [((lrow // GLA_DK) == h2).astype(BF16) for h2 in range(2)]
    hmask = [(lane // GLA_DK) == h2 for h2 in range(2)]
    nsub = C // GLA_SUB

    def chunk_step(c, carry):
        r0 = pl.multiple_of(c * C, C)
        for pair in range(GLA_HEADS // 2):
            cols = pl.ds(pair * LANES, LANES)
            qc = qs_ref[pl.ds(r0, C), cols]
            kc = ks_ref[pl.ds(r0, C), cols]
            la = la_ref[pl.ds(r0, C), cols]
            bcum = _dot(tril, la, HIGHEST)
            b_last = bcum[C - 1:C, :]
            s_rows = [[jnp.zeros((GLA_SUB, C), F32)] for _ in range(2)]
            for sb in range(1, nsub):
                lo = sb * GLA_SUB
                bref = bcum[lo - 1:lo, :]
                q_sb = (qc[lo:lo + GLA_SUB] * jnp.exp(bcum[lo:lo + GLA_SUB] - bref))
                k_sb = (kc * jnp.exp(jnp.minimum(bref - bcum, 0.0))).astype(BF16)
                for h2 in range(2):
                    qm = jnp.where(hmask[h2], q_sb, 0.0).astype(BF16)
                    s_rows[h2].append(_dot_nt(qm, k_sb))
            scores = []
            for h2 in range(2):
                s_off = jnp.concatenate(s_rows[h2], axis=0)
                scores.append(jnp.where((ri // GLA_SUB) > (ci // GLA_SUB), s_off, 0.0))
            for off in range(GLA_SUB):
                if off == 0:
                    prod = qc * kc
                else:
                    k_sh = pltpu.roll(kc, off, 0)
                    b_sh = pltpu.roll(bcum, off, 0)
                    prod = qc * k_sh * jnp.exp(jnp.minimum(bcum - b_sh, 0.0))
                p16 = prod.astype(BF16)
                on_diag = ((ri - ci) == off) & ((ri // GLA_SUB) == (ci // GLA_SUB))
                for h2 in range(2):
                    d = _dot(p16, gsum[h2])
                    scores[h2] = jnp.where(on_diag, d, scores[h2])
            q_dec = qc * jnp.exp(bcum)
            k_dec = kc * jnp.exp(b_last - bcum)
            dec_last = jnp.exp(b_last)
            for h2 in range(2):
                hd = pair * 2 + h2
                vc = v_ref[pl.ds(r0, C), pl.ds(hd * GLA_DV, GLA_DV)].astype(BF16)
                st = state_ref[hd]
                qd = jnp.where(hmask[h2], q_dec, 0.0).astype(BF16)
                kd = jnp.where(hmask[h2], k_dec, 0.0).astype(BF16)
                o = _dot(qd, st.astype(BF16)) + _dot(scores[h2].astype(BF16), vc)
                dl_col = jnp.sum(jnp.where(_iota((LANES, LANES), 0) == _iota((LANES, LANES), 1),
                                           jnp.broadcast_to(dec_last, (LANES, LANES)), 0.0),
                                 axis=-1, keepdims=True)
                state_ref[hd] = st * dl_col + _dot_tn(kd, vc)
                os_ref[pl.ds(r0, C), pl.ds(hd * GLA_DV, GLA_DV)] = o
        return carry

    lax.fori_loop(0, nchunk, chunk_step, 0, unroll=2)

    for hd in range(GLA_HEADS):
        cols = pl.ds(hd * GLA_DV, GLA_DV)
        o = os_ref[:, cols]
        ms = jnp.mean(o * o, axis=-1, keepdims=True)
        y = o * lax.rsqrt(ms + RMS_EPS) * gn_ref[...]
        o_ref[:, cols] = (y * _silu(r_ref[:, cols])).astype(o_ref.dtype)


def gla_mixer(proj, gate_up, gate_bias, gnorm, *, batch, seq, rows):
    T = batch * seq
    nblk = seq // rows
    QW = GLA_HEADS * GLA_DK
    VW = GLA_HEADS * GLA_DV
    row = lambda b, i: b * nblk + i
    wg = jnp.zeros((LANES, QW), F32).at[8:8 + GLA_RANK].set(gate_up).astype(BF16)
    return pl.pallas_call(
        functools.partial(_gla_body, rows=rows),
        grid=(batch, nblk),
        in_specs=[pl.BlockSpec((rows, QW), lambda b, i: (row(b, i), COL_CQ // QW)),
                  pl.BlockSpec((rows, QW), lambda b, i: (row(b, i), COL_CK // QW)),
                  pl.BlockSpec((rows, VW), lambda b, i: (row(b, i), COL_CV // VW)),
                  pl.BlockSpec((rows, VW), lambda b, i: (row(b, i), COL_CR // VW)),
                  pl.BlockSpec((rows, LANES), lambda b, i: (row(b, i), COL_SMALL // LANES)),
                  pl.BlockSpec((LANES, QW), lambda b, i: (0, 0)),
                  pl.BlockSpec((1, QW), lambda b, i: (0, 0)),
                  pl.BlockSpec((1, GLA_DV), lambda b, i: (0, 0))],
        out_specs=pl.BlockSpec((rows, VW), lambda b, i: (row(b, i), 0)),
        out_shape=jax.ShapeDtypeStruct((T, VW), BF16),
        scratch_shapes=[pltpu.VMEM((GLA_HEADS, LANES, GLA_DV), F32),
                        pltpu.VMEM((rows, QW), F32), pltpu.VMEM((rows, QW), F32),
                        pltpu.VMEM((rows, QW), F32), pltpu.VMEM((rows, VW), F32)],
        compiler_params=_cparams("parallel", "arbitrary"),
        name="gla_mixer",
    )(proj, proj, proj, proj, proj, wg, gate_bias.reshape(1, QW), gnorm.reshape(1, GLA_DV))


def _mix_out_body(x_ref, g0_ref, g1_ref, g2_ref, gb_ref, ya_ref, yc_ref,
                  o0_ref, o1_ref, o2_ref, l0_ref, l1_ref, l2_ref,
                  wa_ref, wb_ref, wc_ref, wo_ref, out_ref):
    l0, l1, l2 = l0_ref[...], l1_ref[...], l2_ref[...]
    m = jnp.maximum(jnp.maximum(l0, l1), l2)
    e0, e1, e2 = jnp.exp(l0 - m), jnp.exp(l1 - m), jnp.exp(l2 - m)
    ob = (e0 * o0_ref[...] + e1 * o1_ref[...] + e2 * o2_ref[...]) / (e0 + e1 + e2)
    gb = gb_ref[...]
    y = _sigmoid(g0_ref[...] + gb[0:1]) * _dot(ya_ref[...], wa_ref[...])
    y = y + _sigmoid(g1_ref[...] + gb[1:2]) * _dot(ob.astype(BF16), wb_ref[...])
    y = y + _sigmoid(g2_ref[...] + gb[2:3]) * _dot(yc_ref[...], wc_ref[...])
    out_ref[...] = x_ref[...] + _dot(y.astype(BF16), wo_ref[...])


def mix_out(x, proj, gate_bias, ya, yc, o_grp, lse_grp, wa, wb, wc, wo, *, tm):
    T, D = x.shape
    W = SWA_HEADS * SWA_DH
    rowblk = lambda w: pl.BlockSpec((tm, w), lambda i: (i, 0))
    full = lambda a: pl.BlockSpec(a.shape, lambda i: (0, 0))
    gate = lambda n: pl.BlockSpec((tm, D), lambda i: (i, COL_GATES // D + n))
    return pl.pallas_call(
        _mix_out_body,
        grid=(T // tm,),
        in_specs=[rowblk(D), gate(0), gate(1), gate(2), full(gate_bias), rowblk(ya.shape[1]),
                  rowblk(yc.shape[1]), rowblk(W), rowblk(W), rowblk(W), rowblk(W), rowblk(W), rowblk(W),
                  full(wa), full(wb), full(wc), full(wo)],
        out_specs=rowblk(D),
        out_shape=jax.ShapeDtypeStruct((T, D), F32),
        compiler_params=_cparams("parallel"),
        name="mix_out",
    )(x, proj, proj, proj, gate_bias, ya, yc, *o_grp, *lse_grp, wa, wb, wc, wo)


def _cross_attn_body(x_ref, gx_ref, wq_ref, kv_ref, gq_ref, gk_ref, wo_ref, out_ref):
    x = x_ref[...]
    ms = jnp.mean(x * x, axis=-1, keepdims=True)
    h = (x * lax.rsqrt(ms + RMS_EPS) * gx_ref[...]).astype(BF16)
    q = _dot(h, wq_ref[...])
    kv = kv_ref[...]
    KW = XA_HEADS * XA_DH
    outs = []
    for hd in range(XA_HEADS):
        qh = q[:, hd * XA_DH:(hd + 1) * XA_DH]
        kh = kv[:, hd * XA_DH:(hd + 1) * XA_DH]
        vh = kv[:, KW + hd * XA_DH:KW + (hd + 1) * XA_DH]
        qn = qh * lax.rsqrt(jnp.mean(qh * qh, axis=-1, keepdims=True) + RMS_EPS) * gq_ref[...]
        kn = kh * lax.rsqrt(jnp.mean(kh * kh, axis=-1, keepdims=True) + RMS_EPS) * gk_ref[...]
        s = _dot_nt(qn.astype(BF16), kn.astype(BF16)) * (XA_DH ** -0.5)
        m = jnp.max(s, axis=-1, keepdims=True)
        p = jnp.exp(s - m)
        l = jnp.sum(p, axis=-1, keepdims=True)
        outs.append((_dot(p.astype(BF16), vh.astype(BF16)) / l).astype(BF16))
    o = jnp.concatenate(outs, axis=-1)
    out_ref[...] = x + _dot(o, wo_ref[...])


def cross_attention(x, kv, gx, wq, gq, gk, wo, *, batch, seq, mem_len, tm):
    T, D = x.shape
    per_batch = seq // tm
    full = lambda a: pl.BlockSpec(a.shape, lambda i: (0, 0))
    gx, gq, gk = gx.reshape(1, D), gq.reshape(1, XA_DH), gk.reshape(1, XA_DH)
    return pl.pallas_call(
        _cross_attn_body,
        grid=(T // tm,),
        in_specs=[pl.BlockSpec((tm, D), lambda i: (i, 0)), full(gx), full(wq),
                  pl.BlockSpec((mem_len, kv.shape[1]), lambda i: (i // per_batch, 0)),
                  full(gq), full(gk), full(wo)],
        out_specs=pl.BlockSpec((tm, D), lambda i: (i, 0)),
        out_shape=jax.ShapeDtypeStruct((T, D), F32),
        compiler_params=_cparams("parallel"),
        name="cross_attention",
    )(x, gx, wq, kv, gq, gk, wo)


def _router_body(x_ref, gx_ref, wr_ref, br_ref, h_ref, idx_ref, gate_ref):
    x = x_ref[...]
    ms = jnp.mean(x * x, axis=-1, keepdims=True)
    h = x * lax.rsqrt(ms + RMS_EPS) * gx_ref[...]
    _store_row_tiles(h_ref, h)
    lane = _iota((1, LANES), 1)
    logits = _dot(h, wr_ref[...], HIGHEST) + br_ref[...]
    logits = jnp.where(lane < N_EXPERTS, logits, -jnp.inf)
    idx_out = jnp.zeros(logits.shape, jnp.int32)
    val_out = jnp.full(logits.shape, -jnp.inf, F32)
    for k in range(TOP_K):
        m = jnp.max(logits, axis=-1, keepdims=True)
        sel = jnp.min(jnp.where(logits == m, lane, LANES), axis=-1, keepdims=True)
        idx_out = jnp.where(lane == k, sel, idx_out)
        val_out = jnp.where(lane == k, m, val_out)
        logits = jnp.where(lane == sel, -jnp.inf, logits)
    top = jnp.max(val_out, axis=-1, keepdims=True)
    e = jnp.exp(val_out - top)
    idx_ref[...] = idx_out
    gate_ref[...] = e / jnp.sum(e, axis=-1, keepdims=True)


def moe_router(x, gx, wr, br, *, tm):
    T, D = x.shape
    wr_p = jnp.zeros((D, LANES), F32).at[:, :N_EXPERTS].set(wr)
    br_p = jnp.zeros((1, LANES), F32).at[0, :N_EXPERTS].set(br)
    full = lambda a: pl.BlockSpec(a.shape, lambda i: (0, 0))
    gx = gx.reshape(1, D)
    return pl.pallas_call(
        _router_body,
        grid=(T // tm,),
        in_specs=[pl.BlockSpec((tm, D), lambda i: (i, 0)), full(gx), full(wr_p), full(br_p)],
        out_specs=[pl.BlockSpec((tm * ROW_TILE, LANES), lambda i: (i, 0)),
                   pl.BlockSpec((tm, LANES), lambda i: (i, 0)),
                   pl.BlockSpec((tm, LANES), lambda i: (i, 0))],
        out_shape=[jax.ShapeDtypeStruct((T * ROW_TILE, LANES), F32),
                   jax.ShapeDtypeStruct((T, LANES), jnp.int32),
                   jax.ShapeDtypeStruct((T, LANES), F32)],
        compiler_params=_cparams("parallel"),
        name="moe_router",
    )(x, gx, wr_p, br_p)


def _expert_body(be_ref, nused_ref, tok0_ref, tokn_ref, dstp_ref, dstc_ref, h_ref, wi_ref, bi_ref, wo_ref, bo_ref,
                 out_ref, xbuf, ybuf, wi16, wo16, gsem, ssem, *, dump_base):
    i = pl.program_id(0)
    nu = nused_ref[0]
    slot = i % 2

    @pl.when((i == 0) | (be_ref[i] != be_ref[jnp.maximum(i - 1, 0)]))
    def _():
        wi16[...] = wi_ref[...].astype(BF16)
        wo16[...] = wo_ref[...].astype(BF16)

    def tile_rows(start):
        return pl.ds(pl.multiple_of(start, ROW_TILE), ROW_TILE)

    def gather(tok_ref, s, j):
        return pltpu.make_async_copy(h_ref.at[tile_rows(tok_ref[0, 0, j])], xbuf.at[s, tile_rows(j * ROW_TILE)],
                                     gsem.at[s])

    def scatter(dst_ref, s, j):
        return pltpu.make_async_copy(ybuf.at[s, tile_rows(j * ROW_TILE)], out_ref.at[tile_rows(dst_ref[0, 0, j])],
                                     ssem.at[s])

    def for_rows(fn):
        def body(j, carry):
            fn(j)
            return carry
        lax.fori_loop(0, MOE_BM, body, 0, unroll=8)

    def wait_gathered(s):
        pltpu.make_async_copy(xbuf.at[s], xbuf.at[s], gsem.at[s]).wait()

    def wait_scattered(s):
        pltpu.make_async_copy(ybuf.at[s], ybuf.at[s], ssem.at[s]).wait()

    @pl.when(i == 0)
    def _():
        for_rows(lambda j: gather(tok0_ref, 0, j).start())
        ybuf[...] = jnp.zeros_like(ybuf)
        pltpu.make_async_copy(ybuf.at[0], out_ref.at[pl.ds(dump_base * ROW_TILE, MOE_BM * ROW_TILE)],
                              ssem.at[0]).start()

    def step(s):
        wait_gathered(s)
        for j in range(MOE_BM):
            gather(tokn_ref, 1 - s, j).start()
            scatter(dstp_ref, 1 - s, j).start()
        hh = _dot(_load_row_tiles(xbuf.at[s], MOE_BM).astype(BF16), wi16[...]) + bi_ref[...]
        glu = jnp.minimum(hh[:, :D_EXPERT], SWIGLU_LIMIT)
        lin = jnp.clip(hh[:, D_EXPERT:], -SWIGLU_LIMIT, SWIGLU_LIMIT)
        act = glu * _sigmoid(SWIGLU_ALPHA * glu) * (lin + 1.0)
        y = _dot(act.astype(BF16), wo16[...]) + bo_ref[...]
        wait_scattered(s)
        _store_row_tiles(ybuf.at[s], y)

        @pl.when(i == nu - 1)
        def _():
            for_rows(lambda j: scatter(dstc_ref, s, j).start())
            wait_scattered(s)
            wait_scattered(1 - s)
            wait_gathered(1 - s)

    for s in range(2):
        pl.when((i < nu) & (slot == s))(functools.partial(step, s))


def expert_ffn(h, tok_buf, slot_dst, block_expert, n_used, w_in, b_in, w_out, b_out, *, layer):
    T, D = h.shape[0] // ROW_TILE, D_MODEL
    P = tok_buf.shape[0]
    nb = P // MOE_BM
    L, E, _, F2 = w_in.shape
    dump_base = TOP_K * T
    dst_ext = jnp.concatenate([dump_base + MOE_BM + jnp.arange(MOE_BM, dtype=jnp.int32), slot_dst]) * ROW_TILE
    dst_ext = dst_ext.reshape(nb + 1, 1, MOE_BM)
    idx_blk = lambda f: pl.BlockSpec((1, 1, MOE_BM), f, memory_space=pltpu.SMEM)
    grid_spec = pltpu.PrefetchScalarGridSpec(
        num_scalar_prefetch=2,
        grid=(nb,),
        in_specs=[idx_blk(lambda i, be, nu: (i, 0, 0)),
                  idx_blk(lambda i, be, nu: (jnp.minimum(i + 1, nb - 1), 0, 0)),
                  idx_blk(lambda i, be, nu: (i, 0, 0)),
                  idx_blk(lambda i, be, nu: (i + 1, 0, 0)),
                  pl.BlockSpec(memory_space=pl.ANY),
                  pl.BlockSpec((None, None, D, F2), lambda i, be, nu: (layer, be[i], 0, 0)),
                  pl.BlockSpec((None, None, 1, F2), lambda i, be, nu: (layer, be[i], 0, 0)),
                  pl.BlockSpec((None, None, F2 // 2, D), lambda i, be, nu: (layer, be[i], 0, 0)),
                  pl.BlockSpec((None, None, 1, D), lambda i, be, nu: (layer, be[i], 0, 0))],
        out_specs=pl.BlockSpec(memory_space=pl.ANY),
        scratch_shapes=[pltpu.VMEM((2, MOE_BM * ROW_TILE, LANES), F32), pltpu.VMEM((2, MOE_BM * ROW_TILE, LANES), F32),
                        pltpu.VMEM((D, F2), BF16), pltpu.VMEM((F2 // 2, D), BF16),
                        pltpu.SemaphoreType.DMA((2,)), pltpu.SemaphoreType.DMA((2,))],
    )
    tok3 = (tok_buf * ROW_TILE).reshape(nb, 1, MOE_BM)
    return pl.pallas_call(
        functools.partial(_expert_body, dump_base=dump_base),
        grid_spec=grid_spec,
        out_shape=jax.ShapeDtypeStruct(((dump_base + 2 * MOE_BM) * ROW_TILE, LANES), F32),
        compiler_params=_cparams("arbitrary"),
        name="expert_ffn",
    )(block_expert, n_used, tok3, tok3, dst_ext, dst_ext, h,
      w_in, b_in.reshape(L, E, 1, F2), w_out, b_out.reshape(L, E, 1, D))


def _moe_combine_body(x_ref, y0_ref, y1_ref, y2_ref, y3_ref, gate_ref, out_ref):
    g = gate_ref[...]
    acc = x_ref[...]
    for k, y_ref in enumerate((y0_ref, y1_ref, y2_ref, y3_ref)):
        acc = acc + g[:, k:k + 1] * _load_row_tiles(y_ref, x_ref.shape[0])
    out_ref[...] = acc


def moe_combine(x, y4, gates, *, tm):
    T, D = x.shape
    nblk = T // tm
    ysp = lambda k: pl.BlockSpec((tm * ROW_TILE, LANES), lambda i: (k * nblk + i, 0))
    return pl.pallas_call(
        _moe_combine_body,
        grid=(nblk,),
        in_specs=[pl.BlockSpec((tm, D), lambda i: (i, 0)), ysp(0), ysp(1), ysp(2), ysp(3),
                  pl.BlockSpec((tm, LANES), lambda i: (i, 0))],
        out_specs=pl.BlockSpec((tm, D), lambda i: (i, 0)),
        out_shape=jax.ShapeDtypeStruct((T, D), F32),
        compiler_params=_cparams("parallel"),
        name="moe_combine",
    )(x, y4, y4, y4, y4, gates)


def _pack_w_in(w_in):
    L, D, _ = w_in.shape
    o_alpha = 2048
    o_b = 2056
    o_c = o_b + 2304
    o_low = o_c + 1536
    o_gates = o_low + GLA_RANK
    c = w_in[:, :, o_c:o_low]
    parts = [w_in[:, :, o_gates:],
             w_in[:, :, 0:2048],
             w_in[:, :, o_b:o_c],
             w_in[:, :, o_alpha:o_b], w_in[:, :, o_low:o_gates],
             jnp.zeros((L, D, 256 - 8 - GLA_RANK), w_in.dtype),
             c[:, :, 512:1024], c[:, :, 1024:1536], c[:, :, 0:256], c[:, :, 256:512]]
    packed = jnp.concatenate(parts, axis=-1)
    assert packed.shape[-1] == N_PACKED
    return packed.astype(BF16)


def _routing_tables(idx, n_tokens):
    A = n_tokens * TOP_K
    P = A + N_EXPERTS * MOE_BM
    e_flat = idx.reshape(A)
    onehot = (e_flat[:, None] == jnp.arange(N_EXPERTS, dtype=jnp.int32)[None, :]).astype(jnp.int32)
    csum = jnp.cumsum(onehot, axis=0)
    rank = jnp.sum(onehot * csum, axis=1) - 1
    counts = csum[-1]
    padded = (counts + MOE_BM - 1) // MOE_BM * MOE_BM
    pad_ends = jnp.cumsum(padded)
    pad_starts = pad_ends - padded
    dest = (pad_starts[e_flat] + rank).astype(jnp.int32)
    slot_a = jnp.full((P,), -1, jnp.int32).at[dest].set(jnp.arange(A, dtype=jnp.int32))
    tok_buf = jnp.maximum(slot_a, 0) // TOP_K
    p = jnp.arange(P, dtype=jnp.int32)
    dump = TOP_K * n_tokens + ((p // MOE_BM) % 2) * MOE_BM + p % MOE_BM
    slot_dst = jnp.where(slot_a < 0, dump, (slot_a % TOP_K) * n_tokens + slot_a // TOP_K)
    block_start = jnp.arange(P // MOE_BM, dtype=jnp.int32) * MOE_BM
    block_expert = jnp.minimum(jnp.sum((pad_ends[None, :] <= block_start[:, None]).astype(jnp.int32), axis=1),
                               N_EXPERTS - 1)
    n_used = (pad_ends[-1:] // MOE_BM).astype(jnp.int32)
    return tok_buf, slot_dst, block_expert, n_used


def kernel(x, mem, positions, norm_mix, w_in, gate_bias, gdn_conv, gdn_a_log, gdn_dt_bias, gdn_norm, swa_q_norm, swa_k_norm, gla_gate_up, gla_gate_bias, gla_norm, w_branch_a, w_branch_b, w_branch_c, w_mix_out, norm_cross, norm_mem, xa_wq, xa_wkv, xa_q_norm, xa_k_norm, xa_wo, norm_ffn, router_w, router_b, moe_w_in, moe_b_in, moe_w_out, moe_b_out):
    B, S, D = x.shape
    T = B * S
    M = mem.shape[1]
    depth = w_in.shape[0]
    W = SWA_HEADS * SWA_DH

    inv_freq = ROPE_THETA ** (-jnp.arange(0, SWA_DH, 2, dtype=F32) / SWA_DH)
    ang = positions.astype(F32).reshape(T, 1) * inv_freq[None, :]
    cos, sin = jnp.cos(ang), jnp.sin(ang)
    cosf = jnp.tile(jnp.concatenate([cos, cos], axis=-1), (1, SWA_HEADS))
    sins = jnp.tile(jnp.concatenate([-sin, sin], axis=-1), (1, SWA_HEADS))

    w_in_p = _pack_w_in(w_in)
    xf = x.reshape(T, D)
    memf = mem.reshape(B * M, D)

    for l in range(depth):
        proj = norm_matmul(xf, norm_mix[l], w_in_p[l], tm=1024, tn=1536, name="in_proj")
        ya = gdn_mixer(proj, gdn_conv[l], gdn_a_log[l], gdn_dt_bias[l], gdn_norm[l], batch=B, seq=S, rows=512)
        qn, kn = qk_norm_rope(proj, cosf, sins, swa_q_norm[l], swa_k_norm[l], tm=1024)
        o_grp, lse_grp = [], []
        for gi, (window, dil) in enumerate(SWA_GROUPS):
            assert window // dil == SWA_BACK
            o_g, lse_g = band_attention(qn[gi], kn[gi], proj, group=gi, batch=B, seq=S, dilation=dil)
            o_grp.append(o_g)
            lse_grp.append(lse_g)
        yc = gla_mixer(proj, gla_gate_up[l], gla_gate_bias[l], gla_norm[l], batch=B, seq=S, rows=512)
        xf = mix_out(xf, proj, gate_bias[l], ya, yc, o_grp, lse_grp,
                     w_branch_a[l].astype(BF16), w_branch_b[l].astype(BF16), w_branch_c[l].astype(BF16),
                     w_mix_out[l].astype(BF16), tm=512)

        kv = norm_matmul(memf, norm_mem[l], xa_wkv[l].astype(BF16), tm=min(1024, B * M), tn=1024, name="mem_kv")
        xf = cross_attention(xf, kv, norm_cross[l], xa_wq[l].astype(BF16), xa_q_norm[l], xa_k_norm[l],
                             xa_wo[l].astype(BF16), batch=B, seq=S, mem_len=M, tm=512)

        h, idx, gates = moe_router(xf, norm_ffn[l], router_w[l], router_b[l], tm=1024)
        tok_buf, slot_dst, block_expert, n_used = _routing_tables(idx[:, :TOP_K], T)
        y4 = expert_ffn(h, tok_buf, slot_dst, block_expert, n_used, moe_w_in, moe_b_in, moe_w_out, moe_b_out,
                        layer=l)
        xf = moe_combine(xf, y4, gates, tm=512)

    return xf.reshape(B, S, D)
```

```python
import functools

import jax
import jax.numpy as jnp
from jax import lax
from jax.experimental import pallas as pl
from jax.experimental.pallas import tpu as pltpu

F32 = jnp.float32
BF16 = jnp.bfloat16
HIGHEST = lax.Precision.HIGHEST

RMS_EPS = 1e-6
L2_EPS = 1e-6
LANES = 128
VMEM_LIMIT = 56 * 1024 * 1024

D_MODEL = 1024
GDN_HEADS, GDN_D, GDN_CONV, CHUNK = 4, 128, 4, 64
SWA_GROUPS = ((128, 1), (512, 4), (2048, 16))
SWA_HEADS, SWA_DH, SWA_BACK = 4, 64, 128
ROPE_THETA = 10000.0
GLA_HEADS, GLA_DK, GLA_DV, GLA_RANK, GLA_TAU = 4, 64, 128, 16, 16.0
GLA_SUB = 8
XA_HEADS, XA_DH = 4, 128
N_EXPERTS, TOP_K, D_EXPERT = 32, 4, 1024
SWIGLU_ALPHA, SWIGLU_LIMIT = 1.702, 7.0
MOE_BM = 256

COL_GATES = 0
COL_AQ, COL_AK, COL_AV, COL_AZ = 3072, 3584, 4096, 4608
COL_BQ, COL_BK, COL_BV = 5120, 5888, 6656
COL_SMALL = 7424
COL_CV, COL_CR, COL_CQ, COL_CK = 7680, 8192, 8704, 8960
N_PACKED = 9216


def _cparams(*sem):
    return pltpu.CompilerParams(dimension_semantics=sem, vmem_limit_bytes=VMEM_LIMIT)


def _sigmoid(x):
    return 1.0 / (1.0 + jnp.exp(-x))


def _silu(x):
    return x * _sigmoid(x)


def _softplus(x):
    return jnp.maximum(x, 0.0) + jnp.log(1.0 + jnp.exp(-jnp.abs(x)))


def _dot(a, b, precision=None):
    return jnp.dot(a, b, preferred_element_type=F32, precision=precision)


def _dot_nt(a, b, precision=None):
    return lax.dot_general(a, b, (((1,), (1,)), ((), ())), preferred_element_type=F32, precision=precision)


def _dot_tn(a, b, precision=None):
    return lax.dot_general(a, b, (((0,), (0,)), ((), ())), preferred_element_type=F32, precision=precision)


def _bdot(a, b):
    return lax.dot_general(a, b, (((2,), (1,)), ((0,), (0,))), preferred_element_type=F32)


def _bdot_nt(a, b):
    return lax.dot_general(a, b, (((2,), (2,)), ((0,), (0,))), preferred_element_type=F32)


def _split_bf16(a):
    ah = a.astype(BF16)
    return ah, (a - ah.astype(F32)).astype(BF16)


def _bdot3(a, b):
    (ah, al), (bh, bl) = a, b
    return _bdot(ah, bh) + (_bdot(ah, bl) + _bdot(al, bh))


def _iota(shape, axis):
    return lax.broadcasted_iota(jnp.int32, shape, axis)


ROW_TILE = D_MODEL // LANES


def _store_row_tiles(ref, x):
    n = x.shape[0]
    for s in range(ROW_TILE):
        ref[pl.ds(s, n, stride=ROW_TILE), :] = x[:, s * LANES:(s + 1) * LANES]


def _load_row_tiles(ref, n):
    return jnp.concatenate([ref[pl.ds(s, n, stride=ROW_TILE), :] for s in range(ROW_TILE)], axis=1)


def _norm_matmul_body(x_ref, g_ref, w_ref, o_ref, h_ref):
    @pl.when(pl.program_id(1) == 0)
    def _():
        x = x_ref[...]
        ms = jnp.mean(x * x, axis=-1, keepdims=True)
        h_ref[...] = (x * lax.rsqrt(ms + RMS_EPS) * g_ref[...]).astype(h_ref.dtype)

    o_ref[...] = _dot(h_ref[...], w_ref[...]).astype(o_ref.dtype)


def norm_matmul(x, gain, w, *, tm, tn, name):
    T, D = x.shape
    N = w.shape[1]
    return pl.pallas_call(
        _norm_matmul_body,
        grid=(T // tm, N // tn),
        in_specs=[pl.BlockSpec((tm, D), lambda i, j: (i, 0)),
                  pl.BlockSpec((1, D), lambda i, j: (0, 0)),
                  pl.BlockSpec((D, tn), lambda i, j: (0, j))],
        out_specs=pl.BlockSpec((tm, tn), lambda i, j: (i, j)),
        out_shape=jax.ShapeDtypeStruct((T, N), F32),
        scratch_shapes=[pltpu.VMEM((tm, D), BF16)],
        compiler_params=_cparams("parallel", "arbitrary"),
        name=name,
    )(x, gain.reshape(1, D), w)


def _gdn_body(q_ref, k_ref, v_ref, z_ref, sm_ref, smt_ref, cq_ref, ck_ref, cv_ref, alog_ref, dtb_ref, gn_ref,
              o_ref, state_ref, tail_ref, u_ref, wq_ref, kd_ref, in_ref, dl_ref, os_ref, *, rows):
    blk = pl.program_id(1)
    nchunk = rows // CHUNK

    @pl.when(blk == 0)
    def _():
        state_ref[...] = jnp.zeros_like(state_ref)
        tail_ref[...] = jnp.zeros_like(tail_ref)

    def conv_silu(x_ref, w_ref, slot):
        x = x_ref[...]
        xp = jnp.concatenate([tail_ref[slot], x], axis=0)
        w = w_ref[...]
        acc = x * w[GDN_CONV - 1:GDN_CONV, :]
        for s in range(1, GDN_CONV):
            acc = acc + pltpu.roll(xp, s, 0)[8:] * w[GDN_CONV - 1 - s:GDN_CONV - s, :]
        tail_ref[slot] = x[rows - 8:rows]
        return _silu(acc)

    q_all = conv_silu(q_ref, cq_ref, 0)
    k_all = conv_silu(k_ref, ck_ref, 1)
    v_all = conv_silu(v_ref, cv_ref, 2)
    sm = sm_ref[...]
    pos_c = _iota((rows, 1), 0) % CHUNK
    pos_r = _iota((1, rows), 1) % CHUNK
    ri = _iota((1, CHUNK, CHUNK), 1)
    ci = _iota((1, CHUNK, CHUNK), 2)
    incl = ri >= ci
    strict = ri > ci
    eye = (ri == ci).astype(F32)
    chunked = lambda t: t.reshape(nchunk, CHUNK, t.shape[-1])

    for hd in range(GDN_HEADS):
        _gdn_chunk_local(hd, q_all, k_all, v_all, sm, smt_ref, alog_ref, dtb_ref, pos_c, pos_r, incl, strict, eye,
                         chunked, u_ref, wq_ref, kd_ref, in_ref, dl_ref, rows=rows)

    def chunk_step(c, carry):
        for hd in range(GDN_HEADS):
            st = state_ref[hd]
            wq_s = _dot(wq_ref[hd, c], st.astype(BF16))
            v16 = (u_ref[hd, c] - wq_s[:CHUNK]).astype(BF16)
            o = wq_s[CHUNK:] + _dot(in_ref[hd, c], v16)
            state_ref[hd] = st * dl_ref[hd, c][:, :1] + _dot_tn(kd_ref[hd, c], v16)
            os_ref[pl.ds(pl.multiple_of(c * CHUNK, CHUNK), CHUNK), pl.ds(hd * GDN_D, GDN_D)] = o
        return carry

    lax.fori_loop(0, nchunk, chunk_step, 0, unroll=True)

    for hd in range(GDN_HEADS):
        cols = pl.ds(hd * GDN_D, GDN_D)
        o = os_ref[:, cols]
        ms = jnp.mean(o * o, axis=-1, keepdims=True)
        y = o * lax.rsqrt(ms + RMS_EPS) * gn_ref[...]
        o_ref[:, cols] = (y * _silu(z_ref[:, cols])).astype(o_ref.dtype)


def _gdn_chunk_local(hd, q_all, k_all, v_all, sm, smt_ref, alog_ref, dtb_ref, pos_c, pos_r, incl, strict, eye,
                     chunked, u_ref, wq_ref, kd_ref, in_ref, dl_ref, *, rows):
    nchunk = rows // CHUNK
    cols = slice(hd * GDN_D, (hd + 1) * GDN_D)
    q, k, v = q_all[:, cols], k_all[:, cols], v_all[:, cols]
    q = q * lax.rsqrt(jnp.sum(q * q, axis=-1, keepdims=True) + L2_EPS) * (GDN_D ** -0.5)
    k = k * lax.rsqrt(jnp.sum(k * k, axis=-1, keepdims=True) + L2_EPS)

    neg_a = -jnp.exp(alog_ref[:, hd:hd + 1])
    dt_b = dtb_ref[:, hd:hd + 1]
    g_col = jnp.broadcast_to(neg_a * _softplus(sm[:, hd:hd + 1] + dt_b), (rows, LANES))
    beta = jnp.broadcast_to(_sigmoid(sm[:, GDN_HEADS + hd:GDN_HEADS + hd + 1]), (rows, LANES))
    g_row = neg_a * _softplus(smt_ref[hd:hd + 1, :] + dt_b)

    step = 1
    while step < CHUNK:
        g_col = g_col + jnp.where(pos_c >= step, pltpu.roll(g_col, step, 0), 0.0)
        g_row = g_row + jnp.where(pos_r >= step, pltpu.roll(g_row, step, 1), 0.0)
        step *= 2
    eg = jnp.exp(g_col)

    q3, k3, g3 = chunked(q), chunked(k), chunked(g_col)
    kb3 = chunked(k * beta)
    eg3 = chunked(eg)
    g_row3 = jnp.stack([g_row[:, c * CHUNK:(c + 1) * CHUNK] for c in range(nchunk)], axis=0)
    decay = jnp.where(incl, jnp.exp(jnp.minimum(g3[:, :, :CHUNK] - g_row3, 0.0)), 0.0)
    kq = jnp.concatenate([kb3, q3], axis=1).astype(BF16)
    s = _bdot_nt(kq, k3.astype(BF16))
    lower = jnp.where(strict, s[:, :CHUNK] * decay, 0.0)
    intra = jnp.where(incl, s[:, CHUNK:] * decay, 0.0)
    pw = -lower
    inv = eye + pw
    pw_s = _split_bf16(pw)
    pw = _bdot3(pw_s, pw_s)
    for _ in range(4):
        inv_s, pw_s = _split_bf16(inv), _split_bf16(pw)
        stacked = tuple(jnp.concatenate([i_, p_], axis=1) for i_, p_ in zip(inv_s, pw_s))
        both = _bdot3(stacked, pw_s)
        inv = inv + both[:, :CHUNK]
        pw = both[:, CHUNK:]
    inv = inv + _bdot3(_split_bf16(inv), _split_bf16(pw))
    rhs = jnp.concatenate([chunked(v * beta), kb3 * eg3], axis=2)
    uw = _bdot3(_split_bf16(inv), _split_bf16(rhs))
    g_last = g3[:, CHUNK - 1:CHUNK, :]
    u_ref[hd] = uw[:, :, :GDN_D]
    wq_ref[hd] = jnp.concatenate([uw[:, :, GDN_D:], q3 * eg3], axis=1).astype(BF16)
    kd_ref[hd] = (k3 * jnp.exp(g_last - g3)).astype(BF16)
    in_ref[hd] = intra.astype(BF16)
    dl_ref[hd] = jnp.exp(g_last)


def gdn_mixer(proj, conv_w, a_log, dt_bias, gnorm, *, batch, seq, rows):
    T = batch * seq
    nblk = seq // rows
    nchunk = rows // CHUNK
    H = GDN_HEADS
    HW = H * GDN_D
    small_t = proj[:, COL_SMALL:COL_SMALL + 2 * H].T
    cb = lambda base: pl.BlockSpec((rows, HW), lambda b, i: (b * nblk + i, base // HW))
    pad = lambda v: jnp.zeros((1, LANES), F32).at[0, :H].set(v)
    wspec = lambda n: pl.BlockSpec((GDN_CONV, HW), lambda b, i: (0, n))
    vec = pl.BlockSpec((1, LANES), lambda b, i: (0, 0))
    return pl.pallas_call(
        functools.partial(_gdn_body, rows=rows),
        grid=(batch, nblk),
        in_specs=[cb(COL_AQ), cb(COL_AK), cb(COL_AV), cb(COL_AZ),
                  pl.BlockSpec((rows, LANES), lambda b, i: (b * nblk + i, COL_SMALL // LANES)),
                  pl.BlockSpec((2 * H, rows), lambda b, i: (0, b * nblk + i)),
                  wspec(0), wspec(1), wspec(2), vec, vec, vec],
        out_specs=pl.BlockSpec((rows, HW), lambda b, i: (b * nblk + i, 0)),
        out_shape=jax.ShapeDtypeStruct((T, HW), BF16),
        scratch_shapes=[pltpu.VMEM((H, GDN_D, GDN_D), F32),
                        pltpu.VMEM((3, 8, HW), F32),
                        pltpu.VMEM((H, nchunk, CHUNK, GDN_D), F32),
                        pltpu.VMEM((H, nchunk, 2 * CHUNK, GDN_D), BF16),
                        pltpu.VMEM((H, nchunk, CHUNK, GDN_D), BF16),
                        pltpu.VMEM((H, nchunk, CHUNK, CHUNK), BF16),
                        pltpu.VMEM((H, nchunk, 1, LANES), F32),
                        pltpu.VMEM((rows, HW), F32)],
        compiler_params=_cparams("parallel", "arbitrary"),
        name="gdn_mixer",
    )(proj, proj, proj, proj, proj, small_t, conv_w, conv_w, conv_w, pad(a_log), pad(dt_bias),
      gnorm.reshape(1, GDN_D))


def _qk_rope_body(*refs):
    ng = len(SWA_GROUPS)
    q_refs, k_refs = refs[0:ng], refs[ng:2 * ng]
    cos_ref, sin_ref, gq_ref, gk_ref, gm_ref = refs[2 * ng:2 * ng + 5]
    outs = refs[2 * ng + 5:]
    qo_refs, ko_refs = outs[0:ng], outs[ng:2 * ng]
    lane = _iota((1, 2 * LANES), 1)
    first_half = (lane % SWA_DH) < (SWA_DH // 2)
    cosf = cos_ref[...]
    sins = sin_ref[...]

    def norm_rope(x, gain):
        sq = x * x
        hi = sq.astype(BF16)
        lo = (sq - hi.astype(F32)).astype(BF16)
        ms = _dot(hi, gm_ref[...]) + _dot(lo, gm_ref[...])
        xn = x * lax.rsqrt(ms + RMS_EPS) * gain
        other = jnp.where(first_half, pltpu.roll(xn, 2 * LANES - SWA_DH // 2, 1),
                          pltpu.roll(xn, SWA_DH // 2, 1))
        return xn * cosf + other * sins

    for g in range(ng):
        qo_refs[g][...] = norm_rope(q_refs[g][...], gq_ref[...]) * (SWA_DH ** -0.5)
        ko_refs[g][...] = norm_rope(k_refs[g][...], gk_ref[...])


def qk_norm_rope(proj, cosf, sins, q_gain, k_gain, *, tm):
    T = proj.shape[0]
    W = SWA_HEADS * SWA_DH
    ng = len(SWA_GROUPS)
    grp = jnp.arange(W) // SWA_DH
    gmean = ((grp[:, None] == grp[None, :]).astype(F32) / SWA_DH).astype(BF16)
    tile = lambda v: jnp.tile(v, SWA_HEADS).reshape(1, W)
    vec = pl.BlockSpec((1, W), lambda i: (0, 0))
    col = lambda base, g: pl.BlockSpec((tm, W), lambda i: (i, base // W + g))
    row = pl.BlockSpec((tm, W), lambda i: (i, 0))
    out = jax.ShapeDtypeStruct((T, W), F32)
    res = pl.pallas_call(
        _qk_rope_body,
        grid=(T // tm,),
        in_specs=([col(COL_BQ, g) for g in range(ng)] + [col(COL_BK, g) for g in range(ng)]
                  + [row, row, vec, vec, pl.BlockSpec((W, W), lambda i: (0, 0))]),
        out_specs=[row] * (2 * ng),
        out_shape=[out] * (2 * ng),
        compiler_params=_cparams("parallel"),
        name="qk_norm_rope",
    )(*([proj] * (2 * ng)), cosf, sins, tile(q_gain), tile(k_gain), gmean)
    return res[0:ng], res[ng:2 * ng]


SWA_TILE = 2 * SWA_BACK


def _band_attn_body(q_ref, k_ref, v_ref, o_ref, lse_ref, *, dilation):
    d = dilation
    c = SWA_BACK
    seq = q_ref.shape[0]
    ntile = seq // d // SWA_TILE
    lane = _iota((1, LANES), 1)

    def band(nkeys, lead):
        dist = _iota((SWA_TILE, nkeys), 0) + lead - _iota((SWA_TILE, nkeys), 1)
        return (dist >= 0) & (dist <= SWA_BACK)

    valid_first = band(SWA_TILE, 0)
    valid_next = band(c + SWA_TILE, c)

    def rows_of(start, n):
        return pl.ds(start, n) if d == 1 else pl.ds(start, n, stride=d)

    for r in range(d):
        for j in range(ntile):
            start = r + d * SWA_TILE * j
            q = q_ref[rows_of(start, SWA_TILE), :]
            if j == 0:
                keys, valid = rows_of(start, SWA_TILE), valid_first
            else:
                keys, valid = rows_of(start - d * c, c + SWA_TILE), valid_next
            kk = k_ref[keys, :].astype(BF16)
            vv = v_ref[keys, :].astype(BF16)
            o_acc = jnp.zeros((SWA_TILE, LANES), F32)
            lse_acc = jnp.zeros((SWA_TILE, LANES), F32)
            for h2 in range(LANES // SWA_DH):
                hmask = (lane // SWA_DH) == h2
                qh = jnp.where(hmask, q, 0.0).astype(BF16)
                s = jnp.where(valid, _dot_nt(qh, kk), -jnp.inf)
                m = jnp.max(s, axis=-1, keepdims=True)
                p = jnp.exp(s - m)
                l = jnp.sum(p, axis=-1, keepdims=True)
                pv = _dot(p.astype(BF16), vv) / l
                o_acc = jnp.where(hmask, pv, o_acc)
                lse_acc = jnp.where(hmask, m + jnp.log(l), lse_acc)
            o_ref[rows_of(start, SWA_TILE), :] = o_acc
            lse_ref[rows_of(start, SWA_TILE), :] = lse_acc


def band_attention(q, k, proj, *, group, batch, seq, dilation):
    T, W = q.shape
    assert (seq // dilation) % SWA_TILE == 0
    blk = pl.BlockSpec((seq, LANES), lambda b, hp: (b, hp))
    v_col = (COL_BV + group * W) // LANES
    out = jax.ShapeDtypeStruct((T, W), F32)
    return pl.pallas_call(
        functools.partial(_band_attn_body, dilation=dilation),
        grid=(batch, W // LANES),
        in_specs=[blk, blk, pl.BlockSpec((seq, LANES), lambda b, hp: (b, v_col + hp))],
        out_specs=[blk, blk],
        out_shape=[out, out],
        compiler_params=_cparams("parallel", "parallel"),
        name=f"band_attention_d{dilation}",
    )(q, k, proj)


def _gla_body(q_ref, k_ref, v_ref, r_ref, sm_ref, wg_ref, bg_ref, gn_ref, o_ref,
              state_ref, qs_ref, ks_ref, la_ref, os_ref, *, rows):
    blk = pl.program_id(1)
    nchunk = rows // CHUNK
    C = CHUNK

    @pl.when(blk == 0)
    def _():
        state_ref[...] = jnp.zeros_like(state_ref)

    x = _dot(sm_ref[...].astype(BF16), wg_ref[...]) + bg_ref[...]
    la_ref[...] = (jnp.minimum(x, 0.0) - jnp.log(1.0 + jnp.exp(-jnp.abs(x)))) * (1.0 / GLA_TAU)
    qs_ref[...] = q_ref[...] * (GLA_DK ** -0.5)
    ks_ref[...] = k_ref[...]

    ri = _iota((C, C), 0)
    ci = _iota((C, C), 1)
    tril = (ri >= ci).astype(F32)
    lane = _iota((1, LANES), 1)
    lrow = _iota((LANES, C), 0)
    gsum = [((lrow // GLA_DK) == h2).astype(BF16) for h2 in range(2)]
    hmask = [(lane // GLA_DK) == h2 for h2 in range(2)]
    nsub = C // GLA_SUB

    def chunk_step(c, carry):
        r0 = pl.multiple_of(c * C, C)
        for pair in range(GLA_HEADS // 2):
            cols = pl.ds(pair * LANES, LANES)
            qc = qs_ref[pl.ds(r0, C), cols]
            kc = ks_ref[pl.ds(r0, C), cols]
            la = la_ref[pl.ds(r0, C), cols]
            bcum = _dot(tril, la, HIGHEST)
            b_last = bcum[C - 1:C, :]
            s_rows = [[jnp.zeros((GLA_SUB, C), F32)] for _ in range(2)]
            for sb in range(1, nsub):
                lo = sb * GLA_SUB
                bref = bcum[lo - 1:lo, :]
                q_sb = (qc[lo:lo + GLA_SUB] * jnp.exp(bcum[lo:lo + GLA_SUB] - bref))
                k_sb = (kc * jnp.exp(jnp.minimum(bref - bcum, 0.0))).astype(BF16)
                for h2 in range(2):
                    qm = jnp.where(hmask[h2], q_sb, 0.0).astype(BF16)
                    s_rows[h2].append(_dot_nt(qm, k_sb))
            scores = []
            for h2 in range(2):
                s_off = jnp.concatenate(s_rows[h2], axis=0)
                scores.append(jnp.where((ri // GLA_SUB) > (ci // GLA_SUB), s_off, 0.0))
            for off in range(GLA_SUB):
                if off == 0:
                    prod = qc * kc
                else:
                    k_sh = pltpu.roll(kc, off, 0)
                    b_sh = pltpu.roll(bcum, off, 0)
                    prod = qc * k_sh * jnp.exp(jnp.minimum(bcum - b_sh, 0.0))
                p16 = prod.astype(BF16)
                on_diag = ((ri - ci) == off) & ((ri // GLA_SUB) == (ci // GLA_SUB))
                for h2 in range(2):
                    d = _dot(p16, gsum[h2])
                    scores[h2] = jnp.where(on_diag, d, scores[h2])
            q_dec = qc * jnp.exp(bcum)
            k_dec = kc * jnp.exp(b_last - bcum)
            dec_last = jnp.exp(b_last)
            for h2 in range(2):
                hd = pair * 2 + h2
                vc = v_ref[pl.ds(r0, C), pl.ds(hd * GLA_DV, GLA_DV)].astype(BF16)
                st = state_ref[hd]
                qd = jnp.where(hmask[h2], q_dec, 0.0).astype(BF16)
                kd = jnp.where(hmask[h2], k_dec, 0.0).astype(BF16)
                o = _dot(qd, st.astype(BF16)) + _dot(scores[h2].astype(BF16), vc)
                dl_col = jnp.sum(jnp.where(_iota((LANES, LANES), 0) == _iota((LANES, LANES), 1),
                                           jnp.broadcast_to(dec_last, (LANES, LANES)), 0.0),
                                 axis=-1, keepdims=True)
                state_ref[hd] = st * dl_col + _dot_tn(kd, vc)
                os_ref[pl.ds(r0, C), pl.ds(hd * GLA_DV, GLA_DV)] = o
        return carry

    lax.fori_loop(0, nchunk, chunk_step, 0, unroll=2)

    for hd in range(GLA_HEADS):
        cols = pl.ds(hd * GLA_DV, GLA_DV)
        o = os_ref[:, cols]
        ms = jnp.mean(o * o, axis=-1, keepdims=True)
        y = o * lax.rsqrt(ms + RMS_EPS) * gn_ref[...]
        o_ref[:, cols] = (y * _silu(r_ref[:, cols])).astype(o_ref.dtype)


def gla_mixer(proj, gate_up, gate_bias, gnorm, *, batch, seq, rows):
    T = batch * seq
    nblk = seq // rows
    QW = GLA_HEADS * GLA_DK
    VW = GLA_HEADS * GLA_DV
    row = lambda b, i: b * nblk + i
    wg = jnp.zeros((LANES, QW), F32).at[8:8 + GLA_RANK].set(gate_up).astype(BF16)
    return pl.pallas_call(
        functools.partial(_gla_body, rows=rows),
        grid=(batch, nblk),
        in_specs=[pl.BlockSpec((rows, QW), lambda b, i: (row(b, i), COL_CQ // QW)),
                  pl.BlockSpec((rows, QW), lambda b, i: (row(b, i), COL_CK // QW)),
                  pl.BlockSpec((rows, VW), lambda b, i: (row(b, i), COL_CV // VW)),
                  pl.BlockSpec((rows, VW), lambda b, i: (row(b, i), COL_CR // VW)),
                  pl.BlockSpec((rows, LANES), lambda b, i: (row(b, i), COL_SMALL // LANES)),
                  pl.BlockSpec((LANES, QW), lambda b, i: (0, 0)),
                  pl.BlockSpec((1, QW), lambda b, i: (0, 0)),
                  pl.BlockSpec((1, GLA_DV), lambda b, i: (0, 0))],
        out_specs=pl.BlockSpec((rows, VW), lambda b, i: (row(b, i), 0)),
        out_shape=jax.ShapeDtypeStruct((T, VW), BF16),
        scratch_shapes=[pltpu.VMEM((GLA_HEADS, LANES, GLA_DV), F32),
                        pltpu.VMEM((rows, QW), F32), pltpu.VMEM((rows, QW), F32),
                        pltpu.VMEM((rows, QW), F32), pltpu.VMEM((rows, VW), F32)],
        compiler_params=_cparams("parallel", "arbitrary"),
        name="gla_mixer",
    )(proj, proj, proj, proj, proj, wg, gate_bias.reshape(1, QW), gnorm.reshape(1, GLA_DV))


def _mix_out_body(x_ref, g0_ref, g1_ref, g2_ref, gb_ref, ya_ref, yc_ref,
                  o0_ref, o1_ref, o2_ref, l0_ref, l1_ref, l2_ref,
                  wa_ref, wb_ref, wc_ref, wo_ref, out_ref):
    l0, l1, l2 = l0_ref[...], l1_ref[...], l2_ref[...]
    m = jnp.maximum(jnp.maximum(l0, l1), l2)
    e0, e1, e2 = jnp.exp(l0 - m), jnp.exp(l1 - m), jnp.exp(l2 - m)
    ob = (e0 * o0_ref[...] + e1 * o1_ref[...] + e2 * o2_ref[...]) / (e0 + e1 + e2)
    gb = gb_ref[...]
    y = _sigmoid(g0_ref[...] + gb[0:1]) * _dot(ya_ref[...], wa_ref[...])
    y = y + _sigmoid(g1_ref[...] + gb[1:2]) * _dot(ob.astype(BF16), wb_ref[...])
    y = y + _sigmoid(g2_ref[...] + gb[2:3]) * _dot(yc_ref[...], wc_ref[...])
    out_ref[...] = x_ref[...] + _dot(y.astype(BF16), wo_ref[...])


def mix_out(x, proj, gate_bias, ya, yc, o_grp, lse_grp, wa, wb, wc, wo, *, tm):
    T, D = x.shape
    W = SWA_HEADS * SWA_DH
    rowblk = lambda w: pl.BlockSpec((tm, w), lambda i: (i, 0))
    full = lambda a: pl.BlockSpec(a.shape, lambda i: (0, 0))
    gate = lambda n: pl.BlockSpec((tm, D), lambda i: (i, COL_GATES // D + n))
    return pl.pallas_call(
        _mix_out_body,
        grid=(T // tm,),
        in_specs=[rowblk(D), gate(0), gate(1), gate(2), full(gate_bias), rowblk(ya.shape[1]),
                  rowblk(yc.shape[1]), rowblk(W), rowblk(W), rowblk(W), rowblk(W), rowblk(W), rowblk(W),
                  full(wa), full(wb), full(wc), full(wo)],
        out_specs=rowblk(D),
        out_shape=jax.ShapeDtypeStruct((T, D), F32),
        compiler_params=_cparams("parallel"),
        name="mix_out",
    )(x, proj, proj, proj, gate_bias, ya, yc, *o_grp, *lse_grp, wa, wb, wc, wo)


def _cross_attn_body(x_ref, gx_ref, wq_ref, kv_ref, gq_ref, gk_ref, wo_ref, out_ref):
    x = x_ref[...]
    ms = jnp.mean(x * x, axis=-1, keepdims=True)
    h = (x * lax.rsqrt(ms + RMS_EPS) * gx_ref[...]).astype(BF16)
    q = _dot(h, wq_ref[...])
    kv = kv_ref[...]
    KW = XA_HEADS * XA_DH
    outs = []
    for hd in range(XA_HEADS):
        qh = q[:, hd * XA_DH:(hd + 1) * XA_DH]
        kh = kv[:, hd * XA_DH:(hd + 1) * XA_DH]
        vh = kv[:, KW + hd * XA_DH:KW + (hd + 1) * XA_DH]
        qn = qh * lax.rsqrt(jnp.mean(qh * qh, axis=-1, keepdims=True) + RMS_EPS) * gq_ref[...]
        kn = kh * lax.rsqrt(jnp.mean(kh * kh, axis=-1, keepdims=True) + RMS_EPS) * gk_ref[...]
        s = _dot_nt(qn.astype(BF16), kn.astype(BF16)) * (XA_DH ** -0.5)
        m = jnp.max(s, axis=-1, keepdims=True)
        p = jnp.exp(s - m)
        l = jnp.sum(p, axis=-1, keepdims=True)
        outs.append((_dot(p.astype(BF16), vh.astype(BF16)) / l).astype(BF16))
    o = jnp.concatenate(outs, axis=-1)
    out_ref[...] = x + _dot(o, wo_ref[...])


def cross_attention(x, kv, gx, wq, gq, gk, wo, *, batch, seq, mem_len, tm):
    T, D = x.shape
    per_batch = seq // tm
    full = lambda a: pl.BlockSpec(a.shape, lambda i: (0, 0))
    gx, gq, gk = gx.reshape(1, D), gq.reshape(1, XA_DH), gk.reshape(1, XA_DH)
    return pl.pallas_call(
        _cross_attn_body,
        grid=(T // tm,),
        in_specs=[pl.BlockSpec((tm, D), lambda i: (i, 0)), full(gx), full(wq),
                  pl.BlockSpec((mem_len, kv.shape[1]), lambda i: (i // per_batch, 0)),
                  full(gq), full(gk), full(wo)],
        out_specs=pl.BlockSpec((tm, D), lambda i: (i, 0)),
        out_shape=jax.ShapeDtypeStruct((T, D), F32),
        compiler_params=_cparams("parallel"),
        name="cross_attention",
    )(x, gx, wq, kv, gq, gk, wo)


def _router_body(x_ref, gx_ref, wrh_ref, wrl_ref, br_ref, h_ref, idx_ref, gate_ref):
    x = x_ref[...]
    ms = jnp.mean(x * x, axis=-1, keepdims=True)
    h = x * lax.rsqrt(ms + RMS_EPS) * gx_ref[...]
    _store_row_tiles(h_ref, h)
    lane = _iota((1, LANES), 1)
    hh, hl = _split_bf16(h)
    logits = _dot(hh, wrh_ref[...]) + (_dot(hh, wrl_ref[...]) + _dot(hl, wrh_ref[...])) + br_ref[...]
    logits = jnp.where(lane < N_EXPERTS, logits, -jnp.inf)
    idx_out = jnp.zeros(logits.shape, jnp.int32)
    val_out = jnp.full(logits.shape, -jnp.inf, F32)
    for k in range(TOP_K):
        m = jnp.max(logits, axis=-1, keepdims=True)
        sel = jnp.min(jnp.where(logits == m, lane, LANES), axis=-1, keepdims=True)
        idx_out = jnp.where(lane == k, sel, idx_out)
        val_out = jnp.where(lane == k, m, val_out)
        logits = jnp.where(lane == sel, -jnp.inf, logits)
    top = jnp.max(val_out, axis=-1, keepdims=True)
    e = jnp.exp(val_out - top)
    idx_ref[...] = idx_out
    gate_ref[...] = e / jnp.sum(e, axis=-1, keepdims=True)


def moe_router(x, gx, wr, br, *, tm):
    T, D = x.shape
    wr_h, wr_l = _split_bf16(jnp.zeros((D, LANES), F32).at[:, :N_EXPERTS].set(wr))
    br_p = jnp.zeros((1, LANES), F32).at[0, :N_EXPERTS].set(br)
    full = lambda a: pl.BlockSpec(a.shape, lambda i: (0, 0))
    gx = gx.reshape(1, D)
    return pl.pallas_call(
        _router_body,
        grid=(T // tm,),
        in_specs=[pl.BlockSpec((tm, D), lambda i: (i, 0)), full(gx), full(wr_h), full(wr_l), full(br_p)],
        out_specs=[pl.BlockSpec((tm * ROW_TILE, LANES), lambda i: (i, 0)),
                   pl.BlockSpec((tm, LANES), lambda i: (i, 0)),
                   pl.BlockSpec((tm, LANES), lambda i: (i, 0))],
        out_shape=[jax.ShapeDtypeStruct((T * ROW_TILE, LANES), F32),
                   jax.ShapeDtypeStruct((T, LANES), jnp.int32),
                   jax.ShapeDtypeStruct((T, LANES), F32)],
        compiler_params=_cparams("parallel"),
        name="moe_router",
    )(x, gx, wr_h, wr_l, br_p)


def _expert_body(be_ref, nused_ref, tok0_ref, tokn_ref, dstp_ref, dstc_ref, h_ref, wi_ref, bi_ref, wo_ref, bo_ref,
                 out_ref, xbuf, ybuf, wi16, wo16, gsem, ssem, *, dump_base):
    i = pl.program_id(0)
    nu = nused_ref[0]
    slot = i % 2

    @pl.when((i == 0) | (be_ref[i] != be_ref[jnp.maximum(i - 1, 0)]))
    def _():
        wi16[...] = wi_ref[...].astype(BF16)
        wo16[...] = wo_ref[...].astype(BF16)

    def tile_rows(start):
        return pl.ds(pl.multiple_of(start, ROW_TILE), ROW_TILE)

    def gather(tok_ref, s, j):
        return pltpu.make_async_copy(h_ref.at[tile_rows(tok_ref[0, 0, j])], xbuf.at[s, tile_rows(j * ROW_TILE)],
                                     gsem.at[s])

    def scatter(dst_ref, s, j):
        return pltpu.make_async_copy(ybuf.at[s, tile_rows(j * ROW_TILE)], out_ref.at[tile_rows(dst_ref[0, 0, j])],
                                     ssem.at[s])

    def for_rows(fn):
        def body(j, carry):
            fn(j)
            return carry
        lax.fori_loop(0, MOE_BM, body, 0, unroll=8)

    def wait_gathered(s):
        pltpu.make_async_copy(xbuf.at[s], xbuf.at[s], gsem.at[s]).wait()

    def wait_scattered(s):
        pltpu.make_async_copy(ybuf.at[s], ybuf.at[s], ssem.at[s]).wait()

    @pl.when(i == 0)
    def _():
        for_rows(lambda j: gather(tok0_ref, 0, j).start())
        ybuf[...] = jnp.zeros_like(ybuf)
        pltpu.make_async_copy(ybuf.at[0], out_ref.at[pl.ds(dump_base * ROW_TILE, MOE_BM * ROW_TILE)],
                              ssem.at[0]).start()

    def step(s):
        wait_gathered(s)
        for j in range(MOE_BM):
            gather(tokn_ref, 1 - s, j).start()
            scatter(dstp_ref, 1 - s, j).start()
        hh = _dot(_load_row_tiles(xbuf.at[s], MOE_BM).astype(BF16), wi16[...]) + bi_ref[...]
        glu = jnp.minimum(hh[:, :D_EXPERT], SWIGLU_LIMIT)
        lin = jnp.clip(hh[:, D_EXPERT:], -SWIGLU_LIMIT, SWIGLU_LIMIT)
        act = glu * _sigmoid(SWIGLU_ALPHA * glu) * (lin + 1.0)
        y = _dot(act.astype(BF16), wo16[...]) + bo_ref[...]
        wait_scattered(s)
        _store_row_tiles(ybuf.at[s], y)

        @pl.when(i == nu - 1)
        def _():
            for_rows(lambda j: scatter(dstc_ref, s, j).start())
            wait_scattered(s)
            wait_scattered(1 - s)
            wait_gathered(1 - s)

    for s in range(2):
        pl.when((i < nu) & (slot == s))(functools.partial(step, s))


def expert_ffn(h, tok_buf, slot_dst, block_expert, n_used, w_in, b_in, w_out, b_out, *, layer):
    T, D = h.shape[0] // ROW_TILE, D_MODEL
    P = tok_buf.shape[0]
    nb = P // MOE_BM
    L, E, _, F2 = w_in.shape
    dump_base = TOP_K * T
    dst_ext = jnp.concatenate([dump_base + MOE_BM + jnp.arange(MOE_BM, dtype=jnp.int32), slot_dst]) * ROW_TILE
    dst_ext = dst_ext.reshape(nb + 1, 1, MOE_BM)
    idx_blk = lambda f: pl.BlockSpec((1, 1, MOE_BM), f, memory_space=pltpu.SMEM)
    grid_spec = pltpu.PrefetchScalarGridSpec(
        num_scalar_prefetch=2,
        grid=(nb,),
        in_specs=[idx_blk(lambda i, be, nu: (i, 0, 0)),
                  idx_blk(lambda i, be, nu: (jnp.minimum(i + 1, nb - 1), 0, 0)),
                  idx_blk(lambda i, be, nu: (i, 0, 0)),
                  idx_blk(lambda i, be, nu: (i + 1, 0, 0)),
                  pl.BlockSpec(memory_space=pl.ANY),
                  pl.BlockSpec((None, None, D, F2), lambda i, be, nu: (layer, be[i], 0, 0)),
                  pl.BlockSpec((None, None, 1, F2), lambda i, be, nu: (layer, be[i], 0, 0)),
                  pl.BlockSpec((None, None, F2 // 2, D), lambda i, be, nu: (layer, be[i], 0, 0)),
                  pl.BlockSpec((None, None, 1, D), lambda i, be, nu: (layer, be[i], 0, 0))],
        out_specs=pl.BlockSpec(memory_space=pl.ANY),
        scratch_shapes=[pltpu.VMEM((2, MOE_BM * ROW_TILE, LANES), F32), pltpu.VMEM((2, MOE_BM * ROW_TILE, LANES), F32),
                        pltpu.VMEM((D, F2), BF16), pltpu.VMEM((F2 // 2, D), BF16),
                        pltpu.SemaphoreType.DMA((2,)), pltpu.SemaphoreType.DMA((2,))],
    )
    tok3 = (tok_buf * ROW_TILE).reshape(nb, 1, MOE_BM)
    return pl.pallas_call(
        functools.partial(_expert_body, dump_base=dump_base),
        grid_spec=grid_spec,
        out_shape=jax.ShapeDtypeStruct(((dump_base + 2 * MOE_BM) * ROW_TILE, LANES), F32),
        compiler_params=_cparams("arbitrary"),
        name="expert_ffn",
    )(block_expert, n_used, tok3, tok3, dst_ext, dst_ext, h,
      w_in, b_in.reshape(L, E, 1, F2), w_out, b_out.reshape(L, E, 1, D))


def _moe_combine_body(x_ref, y0_ref, y1_ref, y2_ref, y3_ref, gate_ref, out_ref):
    g = gate_ref[...]
    acc = x_ref[...]
    for k, y_ref in enumerate((y0_ref, y1_ref, y2_ref, y3_ref)):
        acc = acc + g[:, k:k + 1] * _load_row_tiles(y_ref, x_ref.shape[0])
    out_ref[...] = acc


def moe_combine(x, y4, gates, *, tm):
    T, D = x.shape
    nblk = T // tm
    ysp = lambda k: pl.BlockSpec((tm * ROW_TILE, LANES), lambda i: (k * nblk + i, 0))
    return pl.pallas_call(
        _moe_combine_body,
        grid=(nblk,),
        in_specs=[pl.BlockSpec((tm, D), lambda i: (i, 0)), ysp(0), ysp(1), ysp(2), ysp(3),
                  pl.BlockSpec((tm, LANES), lambda i: (i, 0))],
        out_specs=pl.BlockSpec((tm, D), lambda i: (i, 0)),
        out_shape=jax.ShapeDtypeStruct((T, D), F32),
        compiler_params=_cparams("parallel"),
        name="moe_combine",
    )(x, y4, y4, y4, y4, gates)


def _pack_w_in(w_in):
    L, D, _ = w_in.shape
    o_alpha = 2048
    o_b = 2056
    o_c = o_b + 2304
    o_low = o_c + 1536
    o_gates = o_low + GLA_RANK
    c = w_in[:, :, o_c:o_low]
    parts = [w_in[:, :, o_gates:],
             w_in[:, :, 0:2048],
             w_in[:, :, o_b:o_c],
             w_in[:, :, o_alpha:o_b], w_in[:, :, o_low:o_gates],
             jnp.zeros((L, D, 256 - 8 - GLA_RANK), w_in.dtype),
             c[:, :, 512:1024], c[:, :, 1024:1536], c[:, :, 0:256], c[:, :, 256:512]]
    packed = jnp.concatenate(parts, axis=-1)
    assert packed.shape[-1] == N_PACKED
    return packed.astype(BF16)


def _routing_tables(idx, n_tokens):
    A = n_tokens * TOP_K
    P = A + N_EXPERTS * MOE_BM
    e_flat = idx.reshape(A)
    onehot = (e_flat[:, None] == jnp.arange(N_EXPERTS, dtype=jnp.int32)[None, :]).astype(jnp.int32)
    csum = jnp.cumsum(onehot, axis=0)
    rank = jnp.sum(onehot * csum, axis=1) - 1
    counts = csum[-1]
    padded = (counts + MOE_BM - 1) // MOE_BM * MOE_BM
    pad_ends = jnp.cumsum(padded)
    pad_starts = pad_ends - padded
    dest = (pad_starts[e_flat] + rank).astype(jnp.int32)
    slot_a = jnp.full((P,), -1, jnp.int32).at[dest].set(jnp.arange(A, dtype=jnp.int32))
    tok_buf = jnp.maximum(slot_a, 0) // TOP_K
    p = jnp.arange(P, dtype=jnp.int32)
    dump = TOP_K * n_tokens + ((p // MOE_BM) % 2) * MOE_BM + p % MOE_BM
    slot_dst = jnp.where(slot_a < 0, dump, (slot_a % TOP_K) * n_tokens + slot_a // TOP_K)
    block_start = jnp.arange(P // MOE_BM, dtype=jnp.int32) * MOE_BM
    block_expert = jnp.minimum(jnp.sum((pad_ends[None, :] <= block_start[:, None]).astype(jnp.int32), axis=1),
                               N_EXPERTS - 1)
    n_used = (pad_ends[-1:] // MOE_BM).astype(jnp.int32)
    return tok_buf, slot_dst, block_expert, n_used


def kernel(x, mem, positions, norm_mix, w_in, gate_bias, gdn_conv, gdn_a_log, gdn_dt_bias, gdn_norm, swa_q_norm, swa_k_norm, gla_gate_up, gla_gate_bias, gla_norm, w_branch_a, w_branch_b, w_branch_c, w_mix_out, norm_cross, norm_mem, xa_wq, xa_wkv, xa_q_norm, xa_k_norm, xa_wo, norm_ffn, router_w, router_b, moe_w_in, moe_b_in, moe_w_out, moe_b_out):
    B, S, D = x.shape
    T = B * S
    M = mem.shape[1]
    depth = w_in.shape[0]
    W = SWA_HEADS * SWA_DH

    inv_freq = ROPE_THETA ** (-jnp.arange(0, SWA_DH, 2, dtype=F32) / SWA_DH)
    ang = positions.astype(F32).reshape(T, 1) * inv_freq[None, :]
    cos, sin = jnp.cos(ang), jnp.sin(ang)
    cosf = jnp.tile(jnp.concatenate([cos, cos], axis=-1), (1, SWA_HEADS))
    sins = jnp.tile(jnp.concatenate([-sin, sin], axis=-1), (1, SWA_HEADS))

    w_in_p = _pack_w_in(w_in)
    xf = x.reshape(T, D)
    memf = mem.reshape(B * M, D)

    for l in range(depth):
        proj = norm_matmul(xf, norm_mix[l], w_in_p[l], tm=1024, tn=1536, name="in_proj")
        ya = gdn_mixer(proj, gdn_conv[l], gdn_a_log[l], gdn_dt_bias[l], gdn_norm[l], batch=B, seq=S, rows=512)
        qn, kn = qk_norm_rope(proj, cosf, sins, swa_q_norm[l], swa_k_norm[l], tm=1024)
        o_grp, lse_grp = [], []
        for gi, (window, dil) in enumerate(SWA_GROUPS):
            assert window // dil == SWA_BACK
            o_g, lse_g = band_attention(qn[gi], kn[gi], proj, group=gi, batch=B, seq=S, dilation=dil)
            o_grp.append(o_g)
            lse_grp.append(lse_g)
        yc = gla_mixer(proj, gla_gate_up[l], gla_gate_bias[l], gla_norm[l], batch=B, seq=S, rows=512)
        xf = mix_out(xf, proj, gate_bias[l], ya, yc, o_grp, lse_grp,
                     w_branch_a[l].astype(BF16), w_branch_b[l].astype(BF16), w_branch_c[l].astype(BF16),
                     w_mix_out[l].astype(BF16), tm=512)

        kv = norm_matmul(memf, norm_mem[l], xa_wkv[l].astype(BF16), tm=min(1024, B * M), tn=1024, name="mem_kv")
        xf = cross_attention(xf, kv, norm_cross[l], xa_wq[l].astype(BF16), xa_q_norm[l], xa_k_norm[l],
                             xa_wo[l].astype(BF16), batch=B, seq=S, mem_len=M, tm=512)

        h, idx, gates = moe_router(xf, norm_ffn[l], router_w[l], router_b[l], tm=1024)
        tok_buf, slot_dst, block_expert, n_used = _routing_tables(idx[:, :TOP_K], T)
        y4 = expert_ffn(h, tok_buf, slot_dst, block_expert, n_used, moe_w_in, moe_b_in, moe_w_out, moe_b_out,
                        layer=l)
        xf = moe_combine(xf, y4, gates, tm=512)

    return xf.reshape(B, S, D)
```

```python
import functools

import jax
import jax.numpy as jnp
from jax import lax
from jax.experimental import pallas as pl
from jax.experimental.pallas import tpu as pltpu

F32 = jnp.float32
BF16 = jnp.bfloat16
HIGHEST = lax.Precision.HIGHEST

RMS_EPS = 1e-6
L2_EPS = 1e-6
LANES = 128
VMEM_LIMIT = 56 * 1024 * 1024

D_MODEL = 1024
GDN_HEADS, GDN_D, GDN_CONV, CHUNK = 4, 128, 4, 64
SWA_GROUPS = ((128, 1), (512, 4), (2048, 16))
SWA_HEADS, SWA_DH, SWA_BACK = 4, 64, 128
ROPE_THETA = 10000.0
GLA_HEADS, GLA_DK, GLA_DV, GLA_RANK, GLA_TAU = 4, 64, 128, 16, 16.0
GLA_SUB = 8
XA_HEADS, XA_DH = 4, 128
N_EXPERTS, TOP_K, D_EXPERT = 32, 4, 1024
SWIGLU_ALPHA, SWIGLU_LIMIT = 1.702, 7.0
MOE_BM = 256

COL_GATES = 0
COL_AQ, COL_AK, COL_AV, COL_AZ = 3072, 3584, 4096, 4608
COL_BQ, COL_BK, COL_BV = 5120, 5888, 6656
COL_SMALL = 7424
COL_CV, COL_CR, COL_CQ, COL_CK = 7680, 8192, 8704, 8960
N_PACKED = 9216


def _cparams(*sem):
    return pltpu.CompilerParams(dimension_semantics=sem, vmem_limit_bytes=VMEM_LIMIT)


def _sigmoid(x):
    return 1.0 / (1.0 + jnp.exp(-x))


def _silu(x):
    return x * _sigmoid(x)


def _softplus(x):
    return jnp.maximum(x, 0.0) + jnp.log(1.0 + jnp.exp(-jnp.abs(x)))


def _dot(a, b, precision=None):
    return jnp.dot(a, b, preferred_element_type=F32, precision=precision)


def _dot_nt(a, b, precision=None):
    return lax.dot_general(a, b, (((1,), (1,)), ((), ())), preferred_element_type=F32, precision=precision)


def _dot_tn(a, b, precision=None):
    return lax.dot_general(a, b, (((0,), (0,)), ((), ())), preferred_element_type=F32, precision=precision)


def _bdot(a, b):
    return lax.dot_general(a, b, (((2,), (1,)), ((0,), (0,))), preferred_element_type=F32)


def _bdot_nt(a, b):
    return lax.dot_general(a, b, (((2,), (2,)), ((0,), (0,))), preferred_element_type=F32)


def _split_bf16(a):
    ah = a.astype(BF16)
    return ah, (a - ah.astype(F32)).astype(BF16)


def _bdot3(a, b):
    (ah, al), (bh, bl) = a, b
    return _bdot(ah, bh) + (_bdot(ah, bl) + _bdot(al, bh))


def _iota(shape, axis):
    return lax.broadcasted_iota(jnp.int32, shape, axis)


ROW_TILE = D_MODEL // LANES


def _store_row_tiles(ref, x):
    n = x.shape[0]
    for s in range(ROW_TILE):
        ref[pl.ds(s, n, stride=ROW_TILE), :] = x[:, s * LANES:(s + 1) * LANES]


def _load_row_tiles(ref, n):
    return jnp.concatenate([ref[pl.ds(s, n, stride=ROW_TILE), :] for s in range(ROW_TILE)], axis=1)


def _norm_matmul_body(x_ref, g_ref, w_ref, o_ref, h_ref):
    @pl.when(pl.program_id(1) == 0)
    def _():
        x = x_ref[...]
        ms = jnp.mean(x * x, axis=-1, keepdims=True)
        h_ref[...] = (x * lax.rsqrt(ms + RMS_EPS) * g_ref[...]).astype(h_ref.dtype)

    o_ref[...] = _dot(h_ref[...], w_ref[...]).astype(o_ref.dtype)


def norm_matmul(x, gain, w, *, tm, tn, name):
    T, D = x.shape
    N = w.shape[1]
    return pl.pallas_call(
        _norm_matmul_body,
        grid=(T // tm, N // tn),
        in_specs=[pl.BlockSpec((tm, D), lambda i, j: (i, 0)),
                  pl.BlockSpec((1, D), lambda i, j: (0, 0)),
                  pl.BlockSpec((D, tn), lambda i, j: (0, j))],
        out_specs=pl.BlockSpec((tm, tn), lambda i, j: (i, j)),
        out_shape=jax.ShapeDtypeStruct((T, N), F32),
        scratch_shapes=[pltpu.VMEM((tm, D), BF16)],
        compiler_params=_cparams("parallel", "arbitrary"),
        name=name,
    )(x, gain.reshape(1, D), w)


def _gdn_body(q_ref, k_ref, v_ref, z_ref, sm_ref, smt_ref, cq_ref, ck_ref, cv_ref, alog_ref, dtb_ref, gn_ref,
              o_ref, state_ref, tail_ref, u_ref, wq_ref, kd_ref, in_ref, dl_ref, os_ref, *, rows):
    blk = pl.program_id(1)
    nchunk = rows // CHUNK

    @pl.when(blk == 0)
    def _():
        state_ref[...] = jnp.zeros_like(state_ref)
        tail_ref[...] = jnp.zeros_like(tail_ref)

    def conv_silu(x_ref, w_ref, slot):
        x = x_ref[...]
        xp = jnp.concatenate([tail_ref[slot], x], axis=0)
        w = w_ref[...]
        acc = x * w[GDN_CONV - 1:GDN_CONV, :]
        for s in range(1, GDN_CONV):
            acc = acc + pltpu.roll(xp, s, 0)[8:] * w[GDN_CONV - 1 - s:GDN_CONV - s, :]
        tail_ref[slot] = x[rows - 8:rows]
        return _silu(acc)

    q_all = conv_silu(q_ref, cq_ref, 0)
    k_all = conv_silu(k_ref, ck_ref, 1)
    v_all = conv_silu(v_ref, cv_ref, 2)
    sm = sm_ref[...]
    pos_c = _iota((rows, 1), 0) % CHUNK
    pos_r = _iota((1, rows), 1) % CHUNK
    ri = _iota((1, CHUNK, CHUNK), 1)
    ci = _iota((1, CHUNK, CHUNK), 2)
    incl = ri >= ci
    strict = ri > ci
    eye = (ri == ci).astype(F32)
    chunked = lambda t: t.reshape(nchunk, CHUNK, t.shape[-1])

    for hd in range(GDN_HEADS):
        _gdn_chunk_local(hd, q_all, k_all, v_all, sm, smt_ref, alog_ref, dtb_ref, pos_c, pos_r, incl, strict, eye,
                         chunked, u_ref, wq_ref, kd_ref, in_ref, dl_ref, rows=rows)

    def chunk_step(c, carry):
        for hd in range(GDN_HEADS):
            st = state_ref[hd]
            wq_s = _dot(wq_ref[hd, c], st.astype(BF16))
            v16 = (u_ref[hd, c] - wq_s[:CHUNK]).astype(BF16)
            o = wq_s[CHUNK:] + _dot(in_ref[hd, c], v16)
            state_ref[hd] = st * dl_ref[hd, c][:, :1] + _dot_tn(kd_ref[hd, c], v16)
            os_ref[pl.ds(pl.multiple_of(c * CHUNK, CHUNK), CHUNK), pl.ds(hd * GDN_D, GDN_D)] = o
        return carry

    lax.fori_loop(0, nchunk, chunk_step, 0, unroll=True)

    for hd in range(GDN_HEADS):
        cols = pl.ds(hd * GDN_D, GDN_D)
        o = os_ref[:, cols]
        ms = jnp.mean(o * o, axis=-1, keepdims=True)
        y = o * lax.rsqrt(ms + RMS_EPS) * gn_ref[...]
        o_ref[:, cols] = (y * _silu(z_ref[:, cols])).astype(o_ref.dtype)


def _gdn_chunk_local(hd, q_all, k_all, v_all, sm, smt_ref, alog_ref, dtb_ref, pos_c, pos_r, incl, strict, eye,
                     chunked, u_ref, wq_ref, kd_ref, in_ref, dl_ref, *, rows):
    nchunk = rows // CHUNK
    cols = slice(hd * GDN_D, (hd + 1) * GDN_D)
    q, k, v = q_all[:, cols], k_all[:, cols], v_all[:, cols]
    q = q * lax.rsqrt(jnp.sum(q * q, axis=-1, keepdims=True) + L2_EPS) * (GDN_D ** -0.5)
    k = k * lax.rsqrt(jnp.sum(k * k, axis=-1, keepdims=True) + L2_EPS)

    neg_a = -jnp.exp(alog_ref[:, hd:hd + 1])
    dt_b = dtb_ref[:, hd:hd + 1]
    g_col = jnp.broadcast_to(neg_a * _softplus(sm[:, hd:hd + 1] + dt_b), (rows, LANES))
    beta = jnp.broadcast_to(_sigmoid(sm[:, GDN_HEADS + hd:GDN_HEADS + hd + 1]), (rows, LANES))
    g_row = neg_a * _softplus(smt_ref[hd:hd + 1, :] + dt_b)

    step = 1
    while step < CHUNK:
        g_col = g_col + jnp.where(pos_c >= step, pltpu.roll(g_col, step, 0), 0.0)
        g_row = g_row + jnp.where(pos_r >= step, pltpu.roll(g_row, step, 1), 0.0)
        step *= 2
    eg = jnp.exp(g_col)

    q3, k3, g3 = chunked(q), chunked(k), chunked(g_col)
    kb3 = chunked(k * beta)
    eg3 = chunked(eg)
    g_row3 = jnp.stack([g_row[:, c * CHUNK:(c + 1) * CHUNK] for c in range(nchunk)], axis=0)
    decay = jnp.where(incl, jnp.exp(jnp.minimum(g3[:, :, :CHUNK] - g_row3, 0.0)), 0.0)
    kq = jnp.concatenate([kb3, q3], axis=1).astype(BF16)
    s = _bdot_nt(kq, k3.astype(BF16))
    lower = jnp.where(strict, s[:, :CHUNK] * decay, 0.0)
    intra = jnp.where(incl, s[:, CHUNK:] * decay, 0.0)
    pw = -lower
    inv = eye + pw
    pw_s = _split_bf16(pw)
    pw = _bdot3(pw_s, pw_s)
    for _ in range(4):
        inv_s, pw_s = _split_bf16(inv), _split_bf16(pw)
        stacked = tuple(jnp.concatenate([i_, p_], axis=1) for i_, p_ in zip(inv_s, pw_s))
        both = _bdot3(stacked, pw_s)
        inv = inv + both[:, :CHUNK]
        pw = both[:, CHUNK:]
    inv = inv + _bdot3(_split_bf16(inv), _split_bf16(pw))
    rhs = jnp.concatenate([chunked(v * beta), kb3 * eg3], axis=2)
    uw = _bdot3(_split_bf16(inv), _split_bf16(rhs))
    g_last = g3[:, CHUNK - 1:CHUNK, :]
    u_ref[hd] = uw[:, :, :GDN_D]
    wq_ref[hd] = jnp.concatenate([uw[:, :, GDN_D:], q3 * eg3], axis=1).astype(BF16)
    kd_ref[hd] = (k3 * jnp.exp(g_last - g3)).astype(BF16)
    in_ref[hd] = intra.astype(BF16)
    dl_ref[hd] = jnp.exp(g_last)


def gdn_mixer(proj, conv_w, a_log, dt_bias, gnorm, *, batch, seq, rows):
    T = batch * seq
    nblk = seq // rows
    nchunk = rows // CHUNK
    H = GDN_HEADS
    HW = H * GDN_D
    small_t = proj[:, COL_SMALL:COL_SMALL + 2 * H].T
    cb = lambda base: pl.BlockSpec((rows, HW), lambda b, i: (b * nblk + i, base // HW))
    pad = lambda v: jnp.zeros((1, LANES), F32).at[0, :H].set(v)
    wspec = lambda n: pl.BlockSpec((GDN_CONV, HW), lambda b, i: (0, n))
    vec = pl.BlockSpec((1, LANES), lambda b, i: (0, 0))
    return pl.pallas_call(
        functools.partial(_gdn_body, rows=rows),
        grid=(batch, nblk),
        in_specs=[cb(COL_AQ), cb(COL_AK), cb(COL_AV), cb(COL_AZ),
                  pl.BlockSpec((rows, LANES), lambda b, i: (b * nblk + i, COL_SMALL // LANES)),
                  pl.BlockSpec((2 * H, rows), lambda b, i: (0, b * nblk + i)),
                  wspec(0), wspec(1), wspec(2), vec, vec, vec],
        out_specs=pl.BlockSpec((rows, HW), lambda b, i: (b * nblk + i, 0)),
        out_shape=jax.ShapeDtypeStruct((T, HW), BF16),
        scratch_shapes=[pltpu.VMEM((H, GDN_D, GDN_D), F32),
                        pltpu.VMEM((3, 8, HW), F32),
                        pltpu.VMEM((H, nchunk, CHUNK, GDN_D), F32),
                        pltpu.VMEM((H, nchunk, 2 * CHUNK, GDN_D), BF16),
                        pltpu.VMEM((H, nchunk, CHUNK, GDN_D), BF16),
                        pltpu.VMEM((H, nchunk, CHUNK, CHUNK), BF16),
                        pltpu.VMEM((H, nchunk, 1, LANES), F32),
                        pltpu.VMEM((rows, HW), F32)],
        compiler_params=_cparams("parallel", "arbitrary"),
        name="gdn_mixer",
    )(proj, proj, proj, proj, proj, small_t, conv_w, conv_w, conv_w, pad(a_log), pad(dt_bias),
      gnorm.reshape(1, GDN_D))


def _qk_rope_body(*refs):
    ng = len(SWA_GROUPS)
    q_refs, k_refs = refs[0:ng], refs[ng:2 * ng]
    cos_ref, sin_ref, gq_ref, gk_ref, gm_ref = refs[2 * ng:2 * ng + 5]
    outs = refs[2 * ng + 5:]
    qo_refs, ko_refs = outs[0:ng], outs[ng:2 * ng]
    lane = _iota((1, 2 * LANES), 1)
    first_half = (lane % SWA_DH) < (SWA_DH // 2)
    cosf = cos_ref[...]
    sins = sin_ref[...]

    def norm_rope(x, gain):
        sq = x * x
        hi = sq.astype(BF16)
        lo = (sq - hi.astype(F32)).astype(BF16)
        ms = _dot(hi, gm_ref[...]) + _dot(lo, gm_ref[...])
        xn = x * lax.rsqrt(ms + RMS_EPS) * gain
        other = jnp.where(first_half, pltpu.roll(xn, 2 * LANES - SWA_DH // 2, 1),
                          pltpu.roll(xn, SWA_DH // 2, 1))
        return xn * cosf + other * sins

    for g in range(ng):
        qo_refs[g][...] = norm_rope(q_refs[g][...], gq_ref[...]) * (SWA_DH ** -0.5)
        ko_refs[g][...] = norm_rope(k_refs[g][...], gk_ref[...])


def qk_norm_rope(proj, cosf, sins, q_gain, k_gain, *, tm):
    T = proj.shape[0]
    W = SWA_HEADS * SWA_DH
    ng = len(SWA_GROUPS)
    grp = jnp.arange(W) // SWA_DH
    gmean = ((grp[:, None] == grp[None, :]).astype(F32) / SWA_DH).astype(BF16)
    tile = lambda v: jnp.tile(v, SWA_HEADS).reshape(1, W)
    vec = pl.BlockSpec((1, W), lambda i: (0, 0))
    col = lambda base, g: pl.BlockSpec((tm, W), lambda i: (i, base // W + g))
    row = pl.BlockSpec((tm, W), lambda i: (i, 0))
    out = jax.ShapeDtypeStruct((T, W), F32)
    res = pl.pallas_call(
        _qk_rope_body,
        grid=(T // tm,),
        in_specs=([col(COL_BQ, g) for g in range(ng)] + [col(COL_BK, g) for g in range(ng)]
                  + [row, row, vec, vec, pl.BlockSpec((W, W), lambda i: (0, 0))]),
        out_specs=[row] * (2 * ng),
        out_shape=[out] * (2 * ng),
        compiler_params=_cparams("parallel"),
        name="qk_norm_rope",
    )(*([proj] * (2 * ng)), cosf, sins, tile(q_gain), tile(k_gain), gmean)
    return res[0:ng], res[ng:2 * ng]


SWA_TILE = 2 * SWA_BACK


def _band_attn_body(qin_ref, kin_ref, v_ref, cos_ref, sin_ref, gq_ref, gk_ref, gm_ref, o_ref, lse_ref,
                    q_ref, k_ref, *, dilation):
    d = dilation
    c = SWA_BACK
    seq = q_ref.shape[0]
    ntile = seq // d // SWA_TILE
    lane = _iota((1, LANES), 1)

    first_half = (lane % SWA_DH) < (SWA_DH // 2)

    def norm_rope(x, gain):
        sq = x * x
        hi, lo = _split_bf16(sq)
        ms = _dot(hi, gm_ref[...]) + _dot(lo, gm_ref[...])
        xn = x * lax.rsqrt(ms + RMS_EPS) * gain
        other = jnp.where(first_half, pltpu.roll(xn, LANES - SWA_DH // 2, 1), pltpu.roll(xn, SWA_DH // 2, 1))
        return xn * cos_ref[...] + other * sin_ref[...]

    q_ref[...] = norm_rope(qin_ref[...], gq_ref[...]) * (SWA_DH ** -0.5)
    k_ref[...] = norm_rope(kin_ref[...], gk_ref[...])

    def band(nkeys, lead):
        dist = _iota((SWA_TILE, nkeys), 0) + lead - _iota((SWA_TILE, nkeys), 1)
        return (dist >= 0) & (dist <= SWA_BACK)

    valid_first = band(SWA_TILE, 0)
    valid_next = band(c + SWA_TILE, c)

    def rows_of(start, n):
        return pl.ds(start, n) if d == 1 else pl.ds(start, n, stride=d)

    for r in range(d):
        for j in range(ntile):
            start = r + d * SWA_TILE * j
            q = q_ref[rows_of(start, SWA_TILE), :]
            if j == 0:
                keys, valid = rows_of(start, SWA_TILE), valid_first
            else:
                keys, valid = rows_of(start - d * c, c + SWA_TILE), valid_next
            kk = k_ref[keys, :].astype(BF16)
            vv = v_ref[keys, :].astype(BF16)
            o_acc = jnp.zeros((SWA_TILE, LANES), F32)
            lse_acc = jnp.zeros((SWA_TILE, LANES), F32)
            for h2 in range(LANES // SWA_DH):
                hmask = (lane // SWA_DH) == h2
                qh = jnp.where(hmask, q, 0.0).astype(BF16)
                s = jnp.where(valid, _dot_nt(qh, kk), -jnp.inf)
                m = jnp.max(s, axis=-1, keepdims=True)
                p = jnp.exp(s - m)
                l = jnp.sum(p, axis=-1, keepdims=True)
                pv = _dot(p.astype(BF16), vv) / l
                o_acc = jnp.where(hmask, pv, o_acc)
                lse_acc = jnp.where(hmask, m + jnp.log(l), lse_acc)
            o_ref[rows_of(start, SWA_TILE), :] = o_acc
            lse_ref[rows_of(start, SWA_TILE), :] = lse_acc


def band_attention(proj, cosf, sins, q_gain, k_gain, *, group, batch, seq, dilation):
    T = proj.shape[0]
    W = SWA_HEADS * SWA_DH
    assert (seq // dilation) % SWA_TILE == 0
    blk = pl.BlockSpec((seq, LANES), lambda b, hp: (b, hp))
    col = lambda base: pl.BlockSpec((seq, LANES), lambda b, hp: (b, (base + group * W) // LANES + hp))
    tab = pl.BlockSpec((seq, LANES), lambda b, hp: (b, 0))
    vec = pl.BlockSpec((1, LANES), lambda b, hp: (0, 0))
    grp = jnp.arange(LANES) // SWA_DH
    gmean = ((grp[:, None] == grp[None, :]).astype(F32) / SWA_DH).astype(BF16)
    tile = lambda v: jnp.tile(v, LANES // SWA_DH).reshape(1, LANES)
    out = jax.ShapeDtypeStruct((T, W), F32)
    return pl.pallas_call(
        functools.partial(_band_attn_body, dilation=dilation),
        grid=(batch, W // LANES),
        in_specs=[col(COL_BQ), col(COL_BK), col(COL_BV), tab, tab, vec, vec,
                  pl.BlockSpec((LANES, LANES), lambda b, hp: (0, 0))],
        out_specs=[blk, blk],
        out_shape=[out, out],
        scratch_shapes=[pltpu.VMEM((seq, LANES), F32), pltpu.VMEM((seq, LANES), F32)],
        compiler_params=_cparams("parallel", "parallel"),
        name=f"band_attention_d{dilation}",
    )(proj, proj, proj, cosf, sins, tile(q_gain), tile(k_gain), gmean)


def _gla_body(q_ref, k_ref, v_ref, r_ref, sm_ref, wg_ref, bg_ref, gn_ref, o_ref,
              state_ref, qs_ref, ks_ref, la_ref, os_ref, *, rows):
    blk = pl.program_id(1)
    nchunk = rows // CHUNK
    C = CHUNK

    @pl.when(blk == 0)
    def _():
        state_ref[...] = jnp.zeros_like(state_ref)

    x = _dot(sm_ref[...].astype(BF16), wg_ref[...]) + bg_ref[...]
    la_ref[...] = (jnp.minimum(x, 0.0) - jnp.log(1.0 + jnp.exp(-jnp.abs(x)))) * (1.0 / GLA_TAU)
    qs_ref[...] = q_ref[...] * (GLA_DK ** -0.5)
    ks_ref[...] = k_ref[...]

    ri = _iota((C, C), 0)
    ci = _iota((C, C), 1)
    tril = (ri >= ci).astype(F32)
    lane = _iota((1, LANES), 1)
    lrow = _iota((LANES, C), 0)
    gsum = [((lrow // GLA_DK) == h2).astype(BF16) for h2 in range(2)]
    hmask = [(lane // GLA_DK) == h2 for h2 in range(2)]
    nsub = C // GLA_SUB

    def chunk_step(c, carry):
        r0 = pl.multiple_of(c * C, C)
        for pair in range(GLA_HEADS // 2):
            cols = pl.ds(pair * LANES, LANES)
            qc = qs_ref[pl.ds(r0, C), cols]
            kc = ks_ref[pl.ds(r0, C), cols]
            la = la_ref[pl.ds(r0, C), cols]
            bcum = _dot(tril, la, HIGHEST)
            b_last = bcum[C - 1:C, :]
            s_rows = [[jnp.zeros((GLA_SUB, C), F32)] for _ in range(2)]
            for sb in range(1, nsub):
                lo = sb * GLA_SUB
                bref = bcum[lo - 1:lo, :]
                q_sb = (qc[lo:lo + GLA_SUB] * jnp.exp(bcum[lo:lo + GLA_SUB] - bref))
                k_sb = (kc * jnp.exp(jnp.minimum(bref - bcum, 0.0))).astype(BF16)
                for h2 in range(2):
                    qm = jnp.where(hmask[h2], q_sb, 0.0).astype(BF16)
                    s_rows[h2].append(_dot_nt(qm, k_sb))
            scores = []
            for h2 in range(2):
                s_off = jnp.concatenate(s_rows[h2], axis=0)
                scores.append(jnp.where((ri // GLA_SUB) > (ci // GLA_SUB), s_off, 0.0))
            for off in range(GLA_SUB):
                if off == 0:
                    prod = qc * kc
                else:
                    k_sh = pltpu.roll(kc, off, 0)
                    b_sh = pltpu.roll(bcum, off, 0)
                    prod = qc * k_sh * jnp.exp(jnp.minimum(bcum - b_sh, 0.0))
                p16 = prod.astype(BF16)
                on_diag = ((ri - ci) == off) & ((ri // GLA_SUB) == (ci // GLA_SUB))
                for h2 in range(2):
                    d = _dot(p16, gsum[h2])
                    scores[h2] = jnp.where(on_diag, d, scores[h2])
            q_dec = qc * jnp.exp(bcum)
            k_dec = kc * jnp.exp(b_last - bcum)
            dec_last = jnp.exp(b_last)
            for h2 in range(2):
                hd = pair * 2 + h2
                vc = v_ref[pl.ds(r0, C), pl.ds(hd * GLA_DV, GLA_DV)].astype(BF16)
                st = state_ref[hd]
                qd = jnp.where(hmask[h2], q_dec, 0.0).astype(BF16)
                kd = jnp.where(hmask[h2], k_dec, 0.0).astype(BF16)
                o = _dot(qd, st.astype(BF16)) + _dot(scores[h2].astype(BF16), vc)
                dl_col = jnp.sum(jnp.where(_iota((LANES, LANES), 0) == _iota((LANES, LANES), 1),
                                           jnp.broadcast_to(dec_last, (LANES, LANES)), 0.0),
                                 axis=-1, keepdims=True)
                state_ref[hd] = st * dl_col + _dot_tn(kd, vc)
                os_ref[pl.ds(r0, C), pl.ds(hd * GLA_DV, GLA_DV)] = o
        return carry

    lax.fori_loop(0, nchunk, chunk_step, 0, unroll=2)

    for hd in range(GLA_HEADS):
        cols = pl.ds(hd * GLA_DV, GLA_DV)
        o = os_ref[:, cols]
        ms = jnp.mean(o * o, axis=-1, keepdims=True)
        y = o * lax.rsqrt(ms + RMS_EPS) * gn_ref[...]
        o_ref[:, cols] = (y * _silu(r_ref[:, cols])).astype(o_ref.dtype)


def gla_mixer(proj, gate_up, gate_bias, gnorm, *, batch, seq, rows):
    T = batch * seq
    nblk = seq // rows
    QW = GLA_HEADS * GLA_DK
    VW = GLA_HEADS * GLA_DV
    row = lambda b, i: b * nblk + i
    wg = jnp.zeros((LANES, QW), F32).at[8:8 + GLA_RANK].set(gate_up).astype(BF16)
    return pl.pallas_call(
        functools.partial(_gla_body, rows=rows),
        grid=(batch, nblk),
        in_specs=[pl.BlockSpec((rows, QW), lambda b, i: (row(b, i), COL_CQ // QW)),
                  pl.BlockSpec((rows, QW), lambda b, i: (row(b, i), COL_CK // QW)),
                  pl.BlockSpec((rows, VW), lambda b, i: (row(b, i), COL_CV // VW)),
                  pl.BlockSpec((rows, VW), lambda b, i: (row(b, i), COL_CR // VW)),
                  pl.BlockSpec((rows, LANES), lambda b, i: (row(b, i), COL_SMALL // LANES)),
                  pl.BlockSpec((LANES, QW), lambda b, i: (0, 0)),
                  pl.BlockSpec((1, QW), lambda b, i: (0, 0)),
                  pl.BlockSpec((1, GLA_DV), lambda b, i: (0, 0))],
        out_specs=pl.BlockSpec((rows, VW), lambda b, i: (row(b, i), 0)),
        out_shape=jax.ShapeDtypeStruct((T, VW), BF16),
        scratch_shapes=[pltpu.VMEM((GLA_HEADS, LANES, GLA_DV), F32),
                        pltpu.VMEM((rows, QW), F32), pltpu.VMEM((rows, QW), F32),
                        pltpu.VMEM((rows, QW), F32), pltpu.VMEM((rows, VW), F32)],
        compiler_params=_cparams("parallel", "arbitrary"),
        name="gla_mixer",
    )(proj, proj, proj, proj, proj, wg, gate_bias.reshape(1, QW), gnorm.reshape(1, GLA_DV))


def _mix_out_body(x_ref, g0_ref, g1_ref, g2_ref, gb_ref, ya_ref, yc_ref,
                  o0_ref, o1_ref, o2_ref, l0_ref, l1_ref, l2_ref,
                  wa_ref, wb_ref, wc_ref, wo_ref, out_ref):
    l0, l1, l2 = l0_ref[...], l1_ref[...], l2_ref[...]
    m = jnp.maximum(jnp.maximum(l0, l1), l2)
    e0, e1, e2 = jnp.exp(l0 - m), jnp.exp(l1 - m), jnp.exp(l2 - m)
    ob = (e0 * o0_ref[...] + e1 * o1_ref[...] + e2 * o2_ref[...]) / (e0 + e1 + e2)
    gb = gb_ref[...]
    y = _sigmoid(g0_ref[...] + gb[0:1]) * _dot(ya_ref[...], wa_ref[...])
    y = y + _sigmoid(g1_ref[...] + gb[1:2]) * _dot(ob.astype(BF16), wb_ref[...])
    y = y + _sigmoid(g2_ref[...] + gb[2:3]) * _dot(yc_ref[...], wc_ref[...])
    out_ref[...] = x_ref[...] + _dot(y.astype(BF16), wo_ref[...])


def mix_out(x, proj, gate_bias, ya, yc, o_grp, lse_grp, wa, wb, wc, wo, *, tm):
    T, D = x.shape
    W = SWA_HEADS * SWA_DH
    rowblk = lambda w: pl.BlockSpec((tm, w), lambda i: (i, 0))
    full = lambda a: pl.BlockSpec(a.shape, lambda i: (0, 0))
    gate = lambda n: pl.BlockSpec((tm, D), lambda i: (i, COL_GATES // D + n))
    return pl.pallas_call(
        _mix_out_body,
        grid=(T // tm,),
        in_specs=[rowblk(D), gate(0), gate(1), gate(2), full(gate_bias), rowblk(ya.shape[1]),
                  rowblk(yc.shape[1]), rowblk(W), rowblk(W), rowblk(W), rowblk(W), rowblk(W), rowblk(W),
                  full(wa), full(wb), full(wc), full(wo)],
        out_specs=rowblk(D),
        out_shape=jax.ShapeDtypeStruct((T, D), F32),
        compiler_params=_cparams("parallel"),
        name="mix_out",
    )(x, proj, proj, proj, gate_bias, ya, yc, *o_grp, *lse_grp, wa, wb, wc, wo)


def _cross_attn_body(x_ref, gx_ref, wq_ref, kv_ref, gq_ref, gk_ref, wo_ref, out_ref):
    x = x_ref[...]
    ms = jnp.mean(x * x, axis=-1, keepdims=True)
    h = (x * lax.rsqrt(ms + RMS_EPS) * gx_ref[...]).astype(BF16)
    q = _dot(h, wq_ref[...])
    kv = kv_ref[...]
    KW = XA_HEADS * XA_DH
    outs = []
    for hd in range(XA_HEADS):
        qh = q[:, hd * XA_DH:(hd + 1) * XA_DH]
        kh = kv[:, hd * XA_DH:(hd + 1) * XA_DH]
        vh = kv[:, KW + hd * XA_DH:KW + (hd + 1) * XA_DH]
        qn = qh * lax.rsqrt(jnp.mean(qh * qh, axis=-1, keepdims=True) + RMS_EPS) * gq_ref[...]
        kn = kh * lax.rsqrt(jnp.mean(kh * kh, axis=-1, keepdims=True) + RMS_EPS) * gk_ref[...]
        s = _dot_nt(qn.astype(BF16), kn.astype(BF16)) * (XA_DH ** -0.5)
        m = jnp.max(s, axis=-1, keepdims=True)
        p = jnp.exp(s - m)
        l = jnp.sum(p, axis=-1, keepdims=True)
        outs.append((_dot(p.astype(BF16), vh.astype(BF16)) / l).astype(BF16))
    o = jnp.concatenate(outs, axis=-1)
    out_ref[...] = x + _dot(o, wo_ref[...])


def cross_attention(x, kv, gx, wq, gq, gk, wo, *, batch, seq, mem_len, tm):
    T, D = x.shape
    per_batch = seq // tm
    full = lambda a: pl.BlockSpec(a.shape, lambda i: (0, 0))
    gx, gq, gk = gx.reshape(1, D), gq.reshape(1, XA_DH), gk.reshape(1, XA_DH)
    return pl.pallas_call(
        _cross_attn_body,
        grid=(T // tm,),
        in_specs=[pl.BlockSpec((tm, D), lambda i: (i, 0)), full(gx), full(wq),
                  pl.BlockSpec((mem_len, kv.shape[1]), lambda i: (i // per_batch, 0)),
                  full(gq), full(gk), full(wo)],
        out_specs=pl.BlockSpec((tm, D), lambda i: (i, 0)),
        out_shape=jax.ShapeDtypeStruct((T, D), F32),
        compiler_params=_cparams("parallel"),
        name="cross_attention",
    )(x, gx, wq, kv, gq, gk, wo)


def _router_body(x_ref, gx_ref, wrh_ref, wrl_ref, br_ref, h_ref, idx_ref, gate_ref):
    x = x_ref[...]
    ms = jnp.mean(x * x, axis=-1, keepdims=True)
    h = x * lax.rsqrt(ms + RMS_EPS) * gx_ref[...]
    _store_row_tiles(h_ref, h)
    lane = _iota((1, LANES), 1)
    hh, hl = _split_bf16(h)
    logits = _dot(hh, wrh_ref[...]) + (_dot(hh, wrl_ref[...]) + _dot(hl, wrh_ref[...])) + br_ref[...]
    logits = jnp.where(lane < N_EXPERTS, logits, -jnp.inf)
    idx_out = jnp.zeros(logits.shape, jnp.int32)
    val_out = jnp.full(logits.shape, -jnp.inf, F32)
    for k in range(TOP_K):
        m = jnp.max(logits, axis=-1, keepdims=True)
        sel = jnp.min(jnp.where(logits == m, lane, LANES), axis=-1, keepdims=True)
        idx_out = jnp.where(lane == k, sel, idx_out)
        val_out = jnp.where(lane == k, m, val_out)
        logits = jnp.where(lane == sel, -jnp.inf, logits)
    top = jnp.max(val_out, axis=-1, keepdims=True)
    e = jnp.exp(val_out - top)
    idx_ref[...] = idx_out
    gate_ref[...] = e / jnp.sum(e, axis=-1, keepdims=True)


def moe_router(x, gx, wr, br, *, tm):
    T, D = x.shape
    wr_h, wr_l = _split_bf16(jnp.zeros((D, LANES), F32).at[:, :N_EXPERTS].set(wr))
    br_p = jnp.zeros((1, LANES), F32).at[0, :N_EXPERTS].set(br)
    full = lambda a: pl.BlockSpec(a.shape, lambda i: (0, 0))
    gx = gx.reshape(1, D)
    return pl.pallas_call(
        _router_body,
        grid=(T // tm,),
        in_specs=[pl.BlockSpec((tm, D), lambda i: (i, 0)), full(gx), full(wr_h), full(wr_l), full(br_p)],
        out_specs=[pl.BlockSpec((tm * ROW_TILE, LANES), lambda i: (i, 0)),
                   pl.BlockSpec((tm, LANES), lambda i: (i, 0)),
                   pl.BlockSpec((tm, LANES), lambda i: (i, 0))],
        out_shape=[jax.ShapeDtypeStruct((T * ROW_TILE, LANES), F32),
                   jax.ShapeDtypeStruct((T, LANES), jnp.int32),
                   jax.ShapeDtypeStruct((T, LANES), F32)],
        compiler_params=_cparams("parallel"),
        name="moe_router",
    )(x, gx, wr_h, wr_l, br_p)


def _expert_body(be_ref, nused_ref, tok0_ref, tokn_ref, dstp_ref, dstc_ref, h_ref, wi_ref, bi_ref, wo_ref, bo_ref,
                 out_ref, xbuf, ybuf, wi16, wo16, gsem, ssem, *, dump_base):
    i = pl.program_id(0)
    nu = nused_ref[0]
    slot = i % 2

    @pl.when((i == 0) | (be_ref[i] != be_ref[jnp.maximum(i - 1, 0)]))
    def _():
        wi16[...] = wi_ref[...].astype(BF16)
        wo16[...] = wo_ref[...].astype(BF16)

    def tile_rows(start):
        return pl.ds(pl.multiple_of(start, ROW_TILE), ROW_TILE)

    def gather(tok_ref, s, j):
        return pltpu.make_async_copy(h_ref.at[tile_rows(tok_ref[0, 0, j])], xbuf.at[s, tile_rows(j * ROW_TILE)],
                                     gsem.at[s])

    def scatter(dst_ref, s, j):
        return pltpu.make_async_copy(ybuf.at[s, tile_rows(j * ROW_TILE)], out_ref.at[tile_rows(dst_ref[0, 0, j])],
                                     ssem.at[s])

    def for_rows(fn):
        def body(j, carry):
            fn(j)
            return carry
        lax.fori_loop(0, MOE_BM, body, 0, unroll=8)

    def wait_gathered(s):
        pltpu.make_async_copy(xbuf.at[s], xbuf.at[s], gsem.at[s]).wait()

    def wait_scattered(s):
        pltpu.make_async_copy(ybuf.at[s], ybuf.at[s], ssem.at[s]).wait()

    @pl.when(i == 0)
    def _():
        for_rows(lambda j: gather(tok0_ref, 0, j).start())
        ybuf[...] = jnp.zeros_like(ybuf)
        pltpu.make_async_copy(ybuf.at[0], out_ref.at[pl.ds(dump_base * ROW_TILE, MOE_BM * ROW_TILE)],
                              ssem.at[0]).start()

    def step(s):
        wait_gathered(s)
        for j in range(MOE_BM):
            gather(tokn_ref, 1 - s, j).start()
            scatter(dstp_ref, 1 - s, j).start()
        hh = _dot(_load_row_tiles(xbuf.at[s], MOE_BM).astype(BF16), wi16[...]) + bi_ref[...]
        glu = jnp.minimum(hh[:, :D_EXPERT], SWIGLU_LIMIT)
        lin = jnp.clip(hh[:, D_EXPERT:], -SWIGLU_LIMIT, SWIGLU_LIMIT)
        act = glu * _sigmoid(SWIGLU_ALPHA * glu) * (lin + 1.0)
        y = _dot(act.astype(BF16), wo16[...]) + bo_ref[...]
        wait_scattered(s)
        _store_row_tiles(ybuf.at[s], y)

        @pl.when(i == nu - 1)
        def _():
            for_rows(lambda j: scatter(dstc_ref, s, j).start())
            wait_scattered(s)
            wait_scattered(1 - s)
            wait_gathered(1 - s)

    for s in range(2):
        pl.when((i < nu) & (slot == s))(functools.partial(step, s))


def expert_ffn(h, tok_buf, slot_dst, block_expert, n_used, w_in, b_in, w_out, b_out, *, layer):
    T, D = h.shape[0] // ROW_TILE, D_MODEL
    P = tok_buf.shape[0]
    nb = P // MOE_BM
    L, E, _, F2 = w_in.shape
    dump_base = TOP_K * T
    dst_ext = jnp.concatenate([dump_base + MOE_BM + jnp.arange(MOE_BM, dtype=jnp.int32), slot_dst]) * ROW_TILE
    dst_ext = dst_ext.reshape(nb + 1, 1, MOE_BM)
    idx_blk = lambda f: pl.BlockSpec((1, 1, MOE_BM), f, memory_space=pltpu.SMEM)
    grid_spec = pltpu.PrefetchScalarGridSpec(
        num_scalar_prefetch=2,
        grid=(nb,),
        in_specs=[idx_blk(lambda i, be, nu: (i, 0, 0)),
                  idx_blk(lambda i, be, nu: (jnp.minimum(i + 1, nb - 1), 0, 0)),
                  idx_blk(lambda i, be, nu: (i, 0, 0)),
                  idx_blk(lambda i, be, nu: (i + 1, 0, 0)),
                  pl.BlockSpec(memory_space=pl.ANY),
                  pl.BlockSpec((None, None, D, F2), lambda i, be, nu: (layer, be[i], 0, 0)),
                  pl.BlockSpec((None, None, 1, F2), lambda i, be, nu: (layer, be[i], 0, 0)),
                  pl.BlockSpec((None, None, F2 // 2, D), lambda i, be, nu: (layer, be[i], 0, 0)),
                  pl.BlockSpec((None, None, 1, D), lambda i, be, nu: (layer, be[i], 0, 0))],
        out_specs=pl.BlockSpec(memory_space=pl.ANY),
        scratch_shapes=[pltpu.VMEM((2, MOE_BM * ROW_TILE, LANES), F32), pltpu.VMEM((2, MOE_BM * ROW_TILE, LANES), F32),
                        pltpu.VMEM((D, F2), BF16), pltpu.VMEM((F2 // 2, D), BF16),
                        pltpu.SemaphoreType.DMA((2,)), pltpu.SemaphoreType.DMA((2,))],
    )
    tok3 = (tok_buf * ROW_TILE).reshape(nb, 1, MOE_BM)
    return pl.pallas_call(
        functools.partial(_expert_body, dump_base=dump_base),
        grid_spec=grid_spec,
        out_shape=jax.ShapeDtypeStruct(((dump_base + 2 * MOE_BM) * ROW_TILE, LANES), F32),
        compiler_params=_cparams("arbitrary"),
        name="expert_ffn",
    )(block_expert, n_used, tok3, tok3, dst_ext, dst_ext, h,
      w_in, b_in.reshape(L, E, 1, F2), w_out, b_out.reshape(L, E, 1, D))


def _moe_combine_body(x_ref, y0_ref, y1_ref, y2_ref, y3_ref, gate_ref, out_ref):
    g = gate_ref[...]
    acc = x_ref[...]
    for k, y_ref in enumerate((y0_ref, y1_ref, y2_ref, y3_ref)):
        acc = acc + g[:, k:k + 1] * _load_row_tiles(y_ref, x_ref.shape[0])
    out_ref[...] = acc


def moe_combine(x, y4, gates, *, tm):
    T, D = x.shape
    nblk = T // tm
    ysp = lambda k: pl.BlockSpec((tm * ROW_TILE, LANES), lambda i: (k * nblk + i, 0))
    return pl.pallas_call(
        _moe_combine_body,
        grid=(nblk,),
        in_specs=[pl.BlockSpec((tm, D), lambda i: (i, 0)), ysp(0), ysp(1), ysp(2), ysp(3),
                  pl.BlockSpec((tm, LANES), lambda i: (i, 0))],
        out_specs=pl.BlockSpec((tm, D), lambda i: (i, 0)),
        out_shape=jax.ShapeDtypeStruct((T, D), F32),
        compiler_params=_cparams("parallel"),
        name="moe_combine",
    )(x, y4, y4, y4, y4, gates)


def _pack_w_in(w_in):
    L, D, _ = w_in.shape
    o_alpha = 2048
    o_b = 2056
    o_c = o_b + 2304
    o_low = o_c + 1536
    o_gates = o_low + GLA_RANK
    c = w_in[:, :, o_c:o_low]
    parts = [w_in[:, :, o_gates:],
             w_in[:, :, 0:2048],
             w_in[:, :, o_b:o_c],
             w_in[:, :, o_alpha:o_b], w_in[:, :, o_low:o_gates],
             jnp.zeros((L, D, 256 - 8 - GLA_RANK), w_in.dtype),
             c[:, :, 512:1024], c[:, :, 1024:1536], c[:, :, 0:256], c[:, :, 256:512]]
    packed = jnp.concatenate(parts, axis=-1)
    assert packed.shape[-1] == N_PACKED
    return packed.astype(BF16)


def _routing_tables(idx, n_tokens):
    A = n_tokens * TOP_K
    P = A + N_EXPERTS * MOE_BM
    e_flat = idx.reshape(A)
    onehot = (e_flat[:, None] == jnp.arange(N_EXPERTS, dtype=jnp.int32)[None, :]).astype(jnp.int32)
    csum = jnp.cumsum(onehot, axis=0)
    rank = jnp.sum(onehot * csum, axis=1) - 1
    counts = csum[-1]
    padded = (counts + MOE_BM - 1) // MOE_BM * MOE_BM
    pad_ends = jnp.cumsum(padded)
    pad_starts = pad_ends - padded
    dest = (pad_starts[e_flat] + rank).astype(jnp.int32)
    slot_a = jnp.full((P,), -1, jnp.int32).at[dest].set(jnp.arange(A, dtype=jnp.int32))
    tok_buf = jnp.maximum(slot_a, 0) // TOP_K
    p = jnp.arange(P, dtype=jnp.int32)
    dump = TOP_K * n_tokens + ((p // MOE_BM) % 2) * MOE_BM + p % MOE_BM
    slot_dst = jnp.where(slot_a < 0, dump, (slot_a % TOP_K) * n_tokens + slot_a // TOP_K)
    block_start = jnp.arange(P // MOE_BM, dtype=jnp.int32) * MOE_BM
    block_expert = jnp.minimum(jnp.sum((pad_ends[None, :] <= block_start[:, None]).astype(jnp.int32), axis=1),
                               N_EXPERTS - 1)
    n_used = (pad_ends[-1:] // MOE_BM).astype(jnp.int32)
    return tok_buf, slot_dst, block_expert, n_used


def kernel(x, mem, positions, norm_mix, w_in, gate_bias, gdn_conv, gdn_a_log, gdn_dt_bias, gdn_norm, swa_q_norm, swa_k_norm, gla_gate_up, gla_gate_bias, gla_norm, w_branch_a, w_branch_b, w_branch_c, w_mix_out, norm_cross, norm_mem, xa_wq, xa_wkv, xa_q_norm, xa_k_norm, xa_wo, norm_ffn, router_w, router_b, moe_w_in, moe_b_in, moe_w_out, moe_b_out):
    B, S, D = x.shape
    T = B * S
    M = mem.shape[1]
    depth = w_in.shape[0]
    W = SWA_HEADS * SWA_DH

    inv_freq = ROPE_THETA ** (-jnp.arange(0, SWA_DH, 2, dtype=F32) / SWA_DH)
    ang = positions.astype(F32).reshape(T, 1) * inv_freq[None, :]
    cos, sin = jnp.cos(ang), jnp.sin(ang)
    cosf = jnp.tile(jnp.concatenate([cos, cos], axis=-1), (1, SWA_HEADS))
    sins = jnp.tile(jnp.concatenate([-sin, sin], axis=-1), (1, SWA_HEADS))

    w_in_p = _pack_w_in(w_in)
    xf = x.reshape(T, D)
    memf = mem.reshape(B * M, D)

    for l in range(depth):
        proj = norm_matmul(xf, norm_mix[l], w_in_p[l], tm=1024, tn=1536, name="in_proj")
        ya = gdn_mixer(proj, gdn_conv[l], gdn_a_log[l], gdn_dt_bias[l], gdn_norm[l], batch=B, seq=S, rows=512)
        o_grp, lse_grp = [], []
        for gi, (window, dil) in enumerate(SWA_GROUPS):
            assert window // dil == SWA_BACK
            o_g, lse_g = band_attention(proj, cosf, sins, swa_q_norm[l], swa_k_norm[l],
                                        group=gi, batch=B, seq=S, dilation=dil)
            o_grp.append(o_g)
            lse_grp.append(lse_g)
        yc = gla_mixer(proj, gla_gate_up[l], gla_gate_bias[l], gla_norm[l], batch=B, seq=S, rows=512)
        xf = mix_out(xf, proj, gate_bias[l], ya, yc, o_grp, lse_grp,
                     w_branch_a[l].astype(BF16), w_branch_b[l].astype(BF16), w_branch_c[l].astype(BF16),
                     w_mix_out[l].astype(BF16), tm=512)

        kv = norm_matmul(memf, norm_mem[l], xa_wkv[l].astype(BF16), tm=min(1024, B * M), tn=1024, name="mem_kv")
        xf = cross_attention(xf, kv, norm_cross[l], xa_wq[l].astype(BF16), xa_q_norm[l], xa_k_norm[l],
                             xa_wo[l].astype(BF16), batch=B, seq=S, mem_len=M, tm=512)

        h, idx, gates = moe_router(xf, norm_ffn[l], router_w[l], router_b[l], tm=1024)
        tok_buf, slot_dst, block_expert, n_used = _routing_tables(idx[:, :TOP_K], T)
        y4 = expert_ffn(h, tok_buf, slot_dst, block_expert, n_used, moe_w_in, moe_b_in, moe_w_out, moe_b_out,
                        layer=l)
        xf = moe_combine(xf, y4, gates, tm=512)

    return xf.reshape(B, S, D)
```
